```python
import math
import jax, jax.numpy as jnp
from jax import lax
import numpy as np

D_MODEL = 2048
BATCH = 8
SEQ = 2048
DEPTH = 2

DN_HEADS = 4
DN_DK = 128
DN_DV = 128
DIFF_HEADS = 8
DIFF_DH = 64
DIFF_DV = 2 * DIFF_DH
ML_HEADS = 4
ML_DH = 128
DN_W = DN_HEADS * DN_DV
DIFF_W = DIFF_HEADS * DIFF_DV
ML_W = ML_HEADS * ML_DH
D_MIX = DN_W + DIFF_W + ML_W
CONV_K = 4
DN_CHUNK = 64
ML_CHUNK = 64
Q_BLOCK = 128
SPLIT_SIZES = (3 * DN_W, 2 * ML_W, DN_W, DN_HEADS, DN_HEADS, 3 * DIFF_W, ML_W, ML_W, ML_HEADS, ML_HEADS)
N_IN = sum(SPLIT_SIZES)
D_FF = 7 * D_MODEL // 2
N_EXPERTS = 8
TOP_K = 2
N_DENSE = (DEPTH + 1) // 2
N_MOE = DEPTH // 2
PLE_DIM = 256
EPS = 1e-6

kernel_name = "hybrid_deltanet_diffattn_mlstm_moe_block"


def rms_norm(x, w):
    xf = x.astype(jnp.float32)
    y = xf * lax.rsqrt(jnp.mean(xf * xf, axis=-1, keepdims=True) + EPS)
    return (y * w.astype(jnp.float32)).astype(x.dtype)


def l2_normalize(x):
    xf = x.astype(jnp.float32)
    return xf * lax.rsqrt(jnp.sum(xf * xf, axis=-1, keepdims=True) + EPS)


def causal_dwconv_silu(x, w):
    y = lax.conv_general_dilated(
        x, w[:, None, :].astype(x.dtype), window_strides=(1,),
        padding=[(w.shape[0] - 1, 0)], dimension_numbers=("NWC", "WIO", "NWC"),
        feature_group_count=x.shape[-1])
    return jax.nn.silu(y)


def to_chunks(t, chunk):
    b, s, h = t.shape[:3]
    t = t.astype(jnp.float32).reshape((b, s // chunk, chunk, h) + t.shape[3:])
    return jnp.moveaxis(t, 3, 1)


def from_chunks(o):
    b, h, n, c, d = o.shape
    return jnp.moveaxis(o, 1, 3).reshape(b, n * c, h, d)


def gated_delta_rule(q, k, v, g, beta):
    B, S, H, dk = q.shape
    dv = v.shape[-1]
    C = DN_CHUNK
    q = to_chunks(l2_normalize(q), C) * (dk ** -0.5)
    k = to_chunks(l2_normalize(k), C)
    v = to_chunks(v, C)
    beta = to_chunks(beta, C)
    g = jnp.cumsum(to_chunks(g, C), axis=-1)
    tri_incl = jnp.tril(jnp.ones((C, C), bool))
    tri_strict = jnp.tril(jnp.ones((C, C), bool), -1)
    decay = jnp.exp(jnp.where(tri_incl, g[..., :, None] - g[..., None, :], -jnp.inf))
    k_beta = k * beta[..., None]
    a = jnp.where(tri_strict, jnp.einsum('bhnid,bhnjd->bhnij', k_beta, k) * decay, 0.0)
    rhs = jnp.concatenate([v * beta[..., None], k_beta * jnp.exp(g)[..., None]], axis=-1)
    sol = lax.linalg.triangular_solve(a, rhs, left_side=True, lower=True, unit_diagonal=True)
    u, w = sol[..., :dv], sol[..., dv:]
    attn = jnp.einsum('bhnid,bhnjd->bhnij', q, k) * decay
    q_g = q * jnp.exp(g)[..., None]
    k_g = k * jnp.exp(g[..., -1:] - g)[..., None]
    g_last = jnp.exp(g[..., -1])

    def step(state, xs):
        u_c, w_c, attn_c, qg_c, kg_c, gl_c = xs
        v_new = u_c - w_c @ state
        o = qg_c @ state + attn_c @ v_new
        state = state * gl_c[..., None, None] + jnp.swapaxes(kg_c, -1, -2) @ v_new
        return state, o

    xs = tuple(jnp.moveaxis(t, 2, 0) for t in (u, w, attn, q_g, k_g, g_last))
    _, o = lax.scan(step, jnp.zeros((B, H, dk, dv), jnp.float32), xs)
    return from_chunks(jnp.moveaxis(o, 0, 2))


def mlstm_chunked(q, k, v, i_pre, log_f):
    B, S, H, d = q.shape
    C = ML_CHUNK
    q = to_chunks(q, C)
    k = to_chunks(k, C) * (d ** -0.5)
    v = to_chunks(v, C)
    i_pre = to_chunks(i_pre, C)
    b = jnp.cumsum(to_chunks(log_f, C), axis=-1)
    tri = jnp.tril(jnp.ones((C, C), bool))
    dlog = jnp.where(tri, b[..., :, None] - b[..., None, :] + i_pre[..., None, :], -jnp.inf)
    dmax = jnp.max(dlog, axis=-1)
    qk = jnp.einsum('bhnid,bhnjd->bhnij', q, k)
    a = b[..., -1:] - b + i_pre
    a_max = jnp.max(a, axis=-1)
    b_last = b[..., -1]

    def step(carry, xs):
        c_st, n_st, m_st = carry
        q_c, k_c, v_c, qk_c, dlog_c, dmax_c, b_c, a_c, amax_c, bl_c = xs
        m_t = jnp.maximum(dmax_c, b_c + m_st[..., None])
        s = qk_c * jnp.exp(dlog_c - m_t[..., None])
        inter = jnp.exp(b_c + m_st[..., None] - m_t)
        num = s @ v_c + inter[..., None] * (q_c @ c_st)
        den = jnp.sum(s, axis=-1) + inter * jnp.einsum('bhcd,bhd->bhc', q_c, n_st)
        h = num / jnp.maximum(jnp.abs(den), jnp.exp(-m_t))[..., None]
        m_new = jnp.maximum(bl_c + m_st, amax_c)
        carry_scale = jnp.exp(bl_c + m_st - m_new)
        wk = k_c * jnp.exp(a_c - m_new[..., None])[..., None]
        c_new = carry_scale[..., None, None] * c_st + jnp.swapaxes(wk, -1, -2) @ v_c
        n_new = carry_scale[..., None] * n_st + jnp.sum(wk, axis=-2)
        return (c_new, n_new, m_new), h

    xs = tuple(jnp.moveaxis(t, 2, 0) for t in (q, k, v, qk, dlog, dmax, b, a, a_max, b_last))
    init = (jnp.zeros((B, H, d, d), jnp.float32), jnp.zeros((B, H, d), jnp.float32),
            jnp.zeros((B, H), jnp.float32))
    _, h = lax.scan(step, init, xs)
    return from_chunks(jnp.moveaxis(h, 0, 2))


def diff_attention(q, k, v, lam):
    B, S, H, _, dh = q.shape
    nb = S // Q_BLOCK
    slopes = 2.0 ** (-8.0 * jnp.arange(1, H + 1, dtype=jnp.float32) / H)
    qb = q.reshape(B, nb, Q_BLOCK, H, 2, dh).transpose(1, 0, 3, 4, 2, 5) * (dh ** -0.5)
    kt = k.transpose(0, 2, 3, 1, 4)
    vt = v.transpose(0, 2, 1, 3)
    pos_k = jnp.arange(S)

    def block(args):
        q_blk, blk = args
        pos_q = blk * Q_BLOCK + jnp.arange(Q_BLOCK)
        dist = pos_q[:, None] - pos_k[None, :]
        s = jnp.einsum('bhjqd,bhjkd->bhjqk', q_blk, kt).astype(jnp.float32)
        s = s - slopes[:, None, None, None] * dist.astype(jnp.float32)
        s = jnp.where(dist >= 0, s, -jnp.inf)
        pr = jax.nn.softmax(s, axis=-1)
        a = pr[:, :, 0] - lam * pr[:, :, 1]
        return jnp.einsum('bhqk,bhkd->bhqd', a.astype(vt.dtype), vt)

    o = lax.map(block, (qb, jnp.arange(nb)))
    return o.transpose(1, 0, 3, 2, 4).reshape(B, S, H, -1)


def hybrid_mixer(hn, w_in, conv_dn, conv_ml, dn_a_log, dn_dt_bias, dn_norm,
                 lq1, lk1, lq2, lk2, diff_norm, ml_i_bias, ml_f_bias, ml_norm, lambda_init):
    B, S, _ = hn.shape
    f32 = jnp.float32
    z = hn @ w_in
    split_points = [int(c) for c in np.cumsum(SPLIT_SIZES)[:-1]]
    (dn_qkv, ml_qk, dn_z, dn_b, dn_a, diff_qkv, ml_v, ml_o, ml_i, ml_f) = jnp.split(z, split_points, axis=-1)

    dn_qkv = causal_dwconv_silu(dn_qkv, conv_dn)
    dq, dkk, dvv = jnp.split(dn_qkv, 3, axis=-1)
    beta = jax.nn.sigmoid(dn_b.astype(f32))
    g = -jnp.exp(dn_a_log) * jax.nn.softplus(dn_a.astype(f32) + dn_dt_bias)
    o_dn = gated_delta_rule(dq.reshape(B, S, DN_HEADS, DN_DK), dkk.reshape(B, S, DN_HEADS, DN_DK),
                            dvv.reshape(B, S, DN_HEADS, DN_DV), g, beta)
    o_dn = rms_norm(o_dn, dn_norm) * jax.nn.silu(dn_z.reshape(B, S, DN_HEADS, DN_DV).astype(f32))

    aq, ak, av = jnp.split(diff_qkv, 3, axis=-1)
    lam = (jnp.exp(jnp.sum(lq1.astype(f32) * lk1.astype(f32)))
           - jnp.exp(jnp.sum(lq2.astype(f32) * lk2.astype(f32))) + lambda_init)
    o_diff = diff_attention(aq.reshape(B, S, DIFF_HEADS, 2, DIFF_DH), ak.reshape(B, S, DIFF_HEADS, 2, DIFF_DH),
                            av.reshape(B, S, DIFF_HEADS, DIFF_DV), lam)
    o_diff = rms_norm(o_diff, diff_norm).astype(f32) * (1.0 - lambda_init)

    ml_qk = causal_dwconv_silu(ml_qk, conv_ml)
    mq, mk = jnp.split(ml_qk, 2, axis=-1)
    i_pre = ml_i.astype(f32) + ml_i_bias
    log_f = jax.nn.log_sigmoid(ml_f.astype(f32) + ml_f_bias)
    o_ml = mlstm_chunked(mq.reshape(B, S, ML_HEADS, ML_DH), mk.reshape(B, S, ML_HEADS, ML_DH),
                         ml_v.reshape(B, S, ML_HEADS, ML_DH), i_pre, log_f)
    o_ml = jax.nn.sigmoid(ml_o.reshape(B, S, ML_HEADS, ML_DH).astype(f32)) * rms_norm(o_ml, ml_norm)

    return jnp.concatenate([o_dn.reshape(B, S, DN_W), o_diff.reshape(B, S, DIFF_W),
                            o_ml.reshape(B, S, ML_W)], axis=-1).astype(hn.dtype)


def swiglu(x, w_gate, w_up, w_down):
    return (jax.nn.silu(x @ w_gate) * (x @ w_up)) @ w_down


def moe_swiglu(x, router, w_gate, w_up, w_down):
    B, S, D = x.shape
    xt = x.reshape(B * S, D)
    logits = (xt @ router).astype(jnp.float32)
    top_v, top_i = lax.top_k(logits, TOP_K)
    top_w = jax.nn.softmax(top_v, axis=-1)
    gates = jnp.sum(jax.nn.one_hot(top_i, N_EXPERTS, dtype=jnp.float32) * top_w[..., None], axis=1)
    y = jnp.zeros_like(xt)
    for e in range(N_EXPERTS):
        y = y + gates[:, e:e + 1].astype(xt.dtype) * swiglu(xt, w_gate[e], w_up[e], w_down[e])
    return y.reshape(B, S, D)


def setup_inputs(seed: int = 0) -> dict:
    key = jax.random.key(seed)
    ks = iter(jax.random.split(key, 32))
    f32 = jnp.float32

    def nrm(shape, scale):
        return jax.random.normal(next(ks), shape, f32) * scale

    def gain(shape):
        return 1.0 + nrm(shape, 0.05)

    x = nrm((BATCH, SEQ, D_MODEL), 1.0)
    p = nrm((DEPTH, BATCH, SEQ, PLE_DIM), 1.0)
    attn_norm = gain((DEPTH, D_MODEL))
    w_in = nrm((DEPTH, D_MODEL, N_IN), D_MODEL ** -0.5)
    conv_dn = nrm((DEPTH, CONV_K, 3 * DN_W), CONV_K ** -0.5)
    conv_ml = nrm((DEPTH, CONV_K, 2 * ML_W), CONV_K ** -0.5)
    dn_a_log = jnp.log(jax.random.uniform(next(ks), (DEPTH, DN_HEADS), f32, 1.0, 16.0))
    dn_dt_bias = nrm((DEPTH, DN_HEADS), 0.1)
    dn_norm = gain((DEPTH, DN_DV))
    diff_lq1 = nrm((DEPTH, DIFF_DH), 0.1)
    diff_lk1 = nrm((DEPTH, DIFF_DH), 0.1)
    diff_lq2 = nrm((DEPTH, DIFF_DH), 0.1)
    diff_lk2 = nrm((DEPTH, DIFF_DH), 0.1)
    diff_norm = gain((DEPTH, DIFF_DV))
    ml_i_bias = nrm((DEPTH, ML_HEADS), 0.1)
    ml_f_bias = 3.0 + 3.0 * jax.random.uniform(next(ks), (DEPTH, ML_HEADS), f32)
    ml_norm = gain((DEPTH, ML_DH))
    w_out = nrm((DEPTH, D_MIX, D_MODEL), D_MIX ** -0.5)
    ffn_norm = gain((DEPTH, D_MODEL))
    dense_w_gate = nrm((N_DENSE, D_MODEL, D_FF), D_MODEL ** -0.5)
    dense_w_up = nrm((N_DENSE, D_MODEL, D_FF), D_MODEL ** -0.5)
    dense_w_down = nrm((N_DENSE, D_FF, D_MODEL), D_FF ** -0.5)
    router = nrm((N_MOE, D_MODEL, N_EXPERTS), D_MODEL ** -0.5)
    moe_w_gate = nrm((N_MOE, N_EXPERTS, D_MODEL, D_FF), D_MODEL ** -0.5)
    moe_w_up = nrm((N_MOE, N_EXPERTS, D_MODEL, D_FF), D_MODEL ** -0.5)
    moe_w_down = nrm((N_MOE, N_EXPERTS, D_FF, D_MODEL), D_FF ** -0.5)
    ple_norm = gain((DEPTH, D_MODEL))
    ple_proj = nrm((DEPTH, PLE_DIM, D_MODEL), PLE_DIM ** -0.5)
    ple_gate = nrm((DEPTH, D_MODEL, D_MODEL), D_MODEL ** -0.5)
    final_norm = gain((D_MODEL,))
    return {"x": x, "p": p, "attn_norm": attn_norm, "w_in": w_in, "conv_dn": conv_dn,
            "conv_ml": conv_ml, "dn_a_log": dn_a_log, "dn_dt_bias": dn_dt_bias, "dn_norm": dn_norm,
            "diff_lq1": diff_lq1, "diff_lk1": diff_lk1, "diff_lq2": diff_lq2, "diff_lk2": diff_lk2,
            "diff_norm": diff_norm, "ml_i_bias": ml_i_bias, "ml_f_bias": ml_f_bias, "ml_norm": ml_norm,
            "w_out": w_out, "ffn_norm": ffn_norm, "dense_w_gate": dense_w_gate, "dense_w_up": dense_w_up,
            "dense_w_down": dense_w_down, "router": router, "moe_w_gate": moe_w_gate,
            "moe_w_up": moe_w_up, "moe_w_down": moe_w_down, "ple_norm": ple_norm,
            "ple_proj": ple_proj, "ple_gate": ple_gate, "final_norm": final_norm}


def reference(x, p, attn_norm, w_in, conv_dn, conv_ml, dn_a_log, dn_dt_bias, dn_norm,
              diff_lq1, diff_lk1, diff_lq2, diff_lk2, diff_norm, ml_i_bias, ml_f_bias, ml_norm,
              w_out, ffn_norm, dense_w_gate, dense_w_up, dense_w_down, router, moe_w_gate,
              moe_w_up, moe_w_down, ple_norm, ple_proj, ple_gate, final_norm):
    h = x
    for i in range(DEPTH):
        lambda_init = 0.8 - 0.6 * math.exp(-0.3 * i)
        hn = rms_norm(h, attn_norm[i])
        mix = hybrid_mixer(hn, w_in[i], conv_dn[i], conv_ml[i], dn_a_log[i], dn_dt_bias[i], dn_norm[i],
                           diff_lq1[i], diff_lk1[i], diff_lq2[i], diff_lk2[i], diff_norm[i],
                           ml_i_bias[i], ml_f_bias[i], ml_norm[i], lambda_init)
        h = h + mix @ w_out[i]
        hn = rms_norm(h, ffn_norm[i])
        j = i // 2
        if i % 2 == 0:
            f = swiglu(hn, dense_w_gate[j], dense_w_up[j], dense_w_down[j])
        else:
            f = moe_swiglu(hn, router[j], moe_w_gate[j], moe_w_up[j], moe_w_down[j])
        h = h + f
        gate = jax.nn.sigmoid(rms_norm(h, ple_norm[i]) @ ple_gate[i])
        h = h + (p[i] @ ple_proj[i]) * gate
    return rms_norm(h, final_norm)
```

```python
import functools
import math

import jax
import jax.numpy as jnp
from jax import lax
from jax.experimental import pallas as pl
from jax.experimental.pallas import tpu as pltpu

F32 = jnp.float32
BF16 = jnp.bfloat16
EPS = 1e-6
LANE = 128
NEG = -1e30

DN_HEADS = 4
DIFF_HEADS = 8
ML_HEADS = 4
HEAD_W = 128
DIFF_DH = 64
CHUNK = 64
CONV_K = 4
N_EXPERTS = 8
VMEM_LIMIT = 56 * 1024 * 1024

CB_DN_Q, CB_DN_K, CB_DN_V = 0, 4, 8
CB_ML_Q, CB_ML_K = 12, 16
CB_DN_Z = 20
CB_AQ, CB_AK, CB_AV = 24, 32, 40
CB_ML_V, CB_ML_O = 48, 52
N_BIG = 56 * LANE
SL_DN_B, SL_DN_A, SL_ML_I, SL_ML_F = 0, 4, 8, 12


def _params(sem):
    return pltpu.CompilerParams(dimension_semantics=sem, vmem_limit_bytes=VMEM_LIMIT)


def _dot(a, b):
    return jnp.dot(a.astype(BF16), b.astype(BF16), preferred_element_type=F32)


def _dot_nt(a, b):
    return lax.dot_general(a.astype(BF16), b.astype(BF16), (((1,), (1,)), ((), ())),
                           preferred_element_type=F32)


def _dot_tn(a, b):
    return lax.dot_general(a.astype(BF16), b.astype(BF16), (((0,), (0,)), ((), ())),
                           preferred_element_type=F32)


def _dot_hi(a, b):
    return jnp.dot(a, b, preferred_element_type=F32, precision=lax.Precision.HIGHEST)


def _rms(x, w):
    return x * lax.rsqrt(jnp.mean(x * x, axis=-1, keepdims=True) + EPS) * w


def _sigmoid(x):
    return 1.0 / (1.0 + jnp.exp(-x))


def _silu(x):
    return x * _sigmoid(x)


def _softplus(x):
    return jnp.maximum(x, 0.0) + jnp.log(1.0 + jnp.exp(-jnp.abs(x)))


def _log_sigmoid(x):
    return -_softplus(-x)


def _tile(m, t):
    t = min(m, t)
    assert m % t == 0
    return t


def _norm_proj_kernel(x_ref, nw_ref, w_ref, ws_ref, o_ref, os_ref, xs_ref):
    @pl.when(pl.program_id(1) == 0)
    def _():
        xs_ref[...] = _rms(x_ref[...], nw_ref[...]).astype(BF16)
        os_ref[...] = jnp.dot(xs_ref[...], ws_ref[...], preferred_element_type=F32)

    o_ref[...] = jnp.dot(xs_ref[...], w_ref[...], preferred_element_type=F32)


def norm_proj(x, nw, w_big, w_small, tm=512, tn=1024):
    m, d = x.shape
    n = w_big.shape[1]
    tm, tn = _tile(m, tm), _tile(n, tn)
    return pl.pallas_call(
        _norm_proj_kernel,
        grid=(m // tm, n // tn),
        in_specs=[pl.BlockSpec((tm, d), lambda i, j: (i, 0)),
                  pl.BlockSpec((1, d), lambda i, j: (0, 0)),
                  pl.BlockSpec((d, tn), lambda i, j: (0, j)),
                  pl.BlockSpec((d, LANE), lambda i, j: (0, 0))],
        out_specs=[pl.BlockSpec((tm, tn), lambda i, j: (i, j)),
                   pl.BlockSpec((tm, LANE), lambda i, j: (i, 0))],
        out_shape=[jax.ShapeDtypeStruct((m, n), F32), jax.ShapeDtypeStruct((m, LANE), F32)],
        scratch_shapes=[pltpu.VMEM((tm, d), BF16)],
        compiler_params=_params(("parallel", "arbitrary")),
        name="norm_proj",
    )(x, nw.reshape(1, d), w_big, w_small)


def _conv_silu(x, w):
    s = x.shape[0]
    row = lax.broadcasted_iota(jnp.int32, x.shape, 0)
    y = x * w[CONV_K - 1:CONV_K, :]
    for back in range(1, CONV_K):
        shifted = jnp.where(row >= back, pltpu.roll(x, back, axis=0), 0.0)
        y = y + shifted * w[CONV_K - 1 - back:CONV_K - back, :]
    del s
    return _silu(y)


def _lane_pick(x, lane):
    li = lax.broadcasted_iota(jnp.int32, x.shape, 1)
    return jnp.sum(jnp.where(li == lane, x, 0.0), axis=-1, keepdims=True)


def _chunk_masks():
    r = lax.broadcasted_iota(jnp.int32, (CHUNK, CHUNK), 0)
    c = lax.broadcasted_iota(jnp.int32, (CHUNK, CHUNK), 1)
    return r, c


def _deltanet_kernel(alog_ref, dtb_ref, q_ref, k_ref, v_ref, zg_ref, zs_ref, cq_ref, ck_ref, cv_ref,
                     nw_ref, o_ref, qn_s, kn_s, vv_s, be_s, g_s, u_s, w_s, at_s, qg_s, kg_s, gl_s):
    h = pl.program_id(1)
    s_len = q_ref.shape[0]
    n_chunks = s_len // CHUNK

    q = _conv_silu(q_ref[...], cq_ref[...])
    k = _conv_silu(k_ref[...], ck_ref[...])
    vv_s[...] = _conv_silu(v_ref[...], cv_ref[...])
    qn_s[...] = q * lax.rsqrt(jnp.sum(q * q, axis=-1, keepdims=True) + EPS) * (HEAD_W ** -0.5)
    kn_s[...] = k * lax.rsqrt(jnp.sum(k * k, axis=-1, keepdims=True) + EPS)
    zs = zs_ref[...]
    be_s[...] = _sigmoid(_lane_pick(zs, SL_DN_B + h))
    rate = jnp.exp(jnp.full((1, 1), alog_ref[h], F32))
    g_s[...] = -rate * _softplus(_lane_pick(zs, SL_DN_A + h) + dtb_ref[h])

    r, c = _chunk_masks()
    low_incl = (r >= c).astype(F32)
    strict_up = (r > c).astype(F32)
    tri_incl = r >= c
    tri_strict = r > c

    def prep(ci, carry):
        rows = pl.ds(pl.multiple_of(ci * CHUNK, CHUNK), CHUNK)
        qc, kc, vc = qn_s[rows, :], kn_s[rows, :], vv_s[rows, :]
        beta, g = be_s[rows, :], g_s[rows, :]
        gc = _dot_hi(low_incl, jnp.broadcast_to(g, (CHUNK, HEAD_W)))
        dmat = _dot_hi(low_incl, jnp.broadcast_to(g, (CHUNK, CHUNK)) * strict_up)
        decay = jnp.where(tri_incl, jnp.exp(dmat), 0.0)
        kb = kc * beta
        a = jnp.where(tri_strict, _dot_nt(kb, kc) * decay, 0.0)
        at_s[rows, :] = _dot_nt(qc, kc) * decay
        eg = jnp.exp(gc)
        x = jnp.concatenate([vc * beta, kb * eg], axis=1)
        x = x - _dot_hi(a, x)
        p = a
        for _ in range(5):
            p = _dot_hi(p, p)
            x = x + _dot_hi(p, x)
        u_s[rows, :] = x[:, :HEAD_W]
        w_s[rows, :] = x[:, HEAD_W:]
        g_last = gc[CHUNK - 1:CHUNK, :]
        qg_s[rows, :] = qc * eg
        kg_s[rows, :] = kc * jnp.exp(g_last - gc)
        gl_s[pl.ds(pl.multiple_of(ci * 8, 8), 8), :] = jnp.broadcast_to(jnp.exp(g_last), (8, HEAD_W))
        return carry

    lax.fori_loop(0, n_chunks, prep, 0)

    nw = nw_ref[...]

    def scan(ci, state):
        rows = pl.ds(pl.multiple_of(ci * CHUNK, CHUNK), CHUNK)
        sb = state.astype(BF16)
        v_new = u_s[rows, :] - _dot(w_s[rows, :], sb)
        o = _dot(qg_s[rows, :], sb) + _dot(at_s[rows, :], v_new)
        gl = gl_s[pl.ds(pl.multiple_of(ci * 8, 8), 1), :]
        state = state * gl + _dot_tn(kg_s[rows, :], v_new)
        o_ref[rows, :] = (_rms(o, nw) * _silu(zg_ref[rows, :])).astype(o_ref.dtype)
        return state

    lax.fori_loop(0, n_chunks, scan, jnp.zeros((HEAD_W, HEAD_W), F32))


def deltanet(zb, zs, conv_dn, a_log, dt_bias, dn_norm, batch, seq):
    m = zb.shape[0]
    blk = lambda off: pl.BlockSpec((seq, HEAD_W), lambda b, h: (b, off + h))
    cblk = lambda off: pl.BlockSpec((CONV_K, HEAD_W), lambda b, h: (0, off + h))
    smem = pl.BlockSpec(memory_space=pltpu.SMEM)
    vm = lambda w: pltpu.VMEM((seq, w), F32)
    return pl.pallas_call(
        _deltanet_kernel,
        grid=(batch, DN_HEADS),
        in_specs=[smem, smem, blk(CB_DN_Q), blk(CB_DN_K), blk(CB_DN_V), blk(CB_DN_Z),
                  pl.BlockSpec((seq, LANE), lambda b, h: (b, 0)),
                  cblk(0), cblk(4), cblk(8),
                  pl.BlockSpec((1, HEAD_W), lambda b, h: (0, 0))],
        out_specs=pl.BlockSpec((seq, HEAD_W), lambda b, h: (b, h)),
        out_shape=jax.ShapeDtypeStruct((m, DN_HEADS * HEAD_W), BF16),
        scratch_shapes=[vm(HEAD_W), vm(HEAD_W), vm(HEAD_W), vm(1), vm(1), vm(HEAD_W), vm(HEAD_W),
                        vm(CHUNK), vm(HEAD_W), vm(HEAD_W),
                        pltpu.VMEM((seq // CHUNK * 8, HEAD_W), F32)],
        compiler_params=_params(("parallel", "parallel")),
        name="deltanet",
    )(a_log, dt_bias, zb, zb, zb, zb, zs, conv_dn, conv_dn, conv_dn, dn_norm.reshape(1, HEAD_W))


def _mlstm_kernel(ib_ref, fb_ref, q_ref, k_ref, v_ref, og_ref, zs_ref, cq_ref, ck_ref, nw_ref,
                  o_ref, q_s, k_s, ip_s, lf_s):
    h = pl.program_id(1)
    s_len = q_ref.shape[0]
    n_chunks = s_len // CHUNK

    q_s[...] = _conv_silu(q_ref[...], cq_ref[...])
    k_s[...] = _conv_silu(k_ref[...], ck_ref[...]) * (HEAD_W ** -0.5)
    zs = zs_ref[...]
    ip_s[...] = _lane_pick(zs, SL_ML_I + h) + ib_ref[h]
    lf_s[...] = _log_sigmoid(_lane_pick(zs, SL_ML_F + h) + fb_ref[h])

    r, c = _chunk_masks()
    low_incl = (r >= c).astype(F32)
    strict_up = (r > c).astype(F32)
    eye = (r == c).astype(F32)
    tri_incl = r >= c
    nw = nw_ref[...]

    def step(ci, carry):
        c_st, n_st, m_st = carry
        rows = pl.ds(pl.multiple_of(ci * CHUNK, CHUNK), CHUNK)
        qc, kc, vc = q_s[rows, :], k_s[rows, :], v_ref[rows, :]
        ip, lf = ip_s[rows, :], lf_s[rows, :]
        b = _dot_hi(low_incl, jnp.broadcast_to(lf, (CHUNK, HEAD_W)))
        dl = _dot_hi(low_incl, jnp.broadcast_to(lf, (CHUNK, CHUNK)) * strict_up
                     + jnp.broadcast_to(ip, (CHUNK, CHUNK)) * eye)
        dlog = jnp.where(tri_incl, dl, NEG)
        dmax = jnp.max(dlog, axis=-1, keepdims=True)
        qk = _dot_nt(qc, kc)
        b_col = b[:, :1]
        b_last = b[CHUNK - 1:CHUNK, :1]
        a = b_last - b + ip
        a_max = jnp.max(a, axis=0, keepdims=True)[:, :1]

        m_t = jnp.maximum(dmax, b_col + m_st)
        s = jnp.where(tri_incl, qk * jnp.exp(dlog - m_t), 0.0)
        inter = jnp.exp(b_col + m_st - m_t)
        num = _dot(s, vc) + inter * _dot(qc, c_st)
        den = jnp.sum(s, axis=-1, keepdims=True) + inter * jnp.sum(qc * n_st, axis=-1, keepdims=True)
        hh = num / jnp.maximum(jnp.abs(den), jnp.exp(-m_t))
        m_new = jnp.maximum(b_last + m_st, a_max)
        scale = jnp.exp(b_last + m_st - m_new)
        wk = kc * jnp.exp(a - m_new)
        c_new = scale * c_st + _dot_tn(wk, vc)
        n_new = scale * n_st + jnp.sum(wk, axis=0, keepdims=True)
        o_ref[rows, :] = (_sigmoid(og_ref[rows, :]) * _rms(hh, nw)).astype(o_ref.dtype)
        return c_new, n_new, m_new

    init = (jnp.zeros((HEAD_W, HEAD_W), F32), jnp.zeros((1, HEAD_W), F32), jnp.zeros((1, 1), F32))
    lax.fori_loop(0, n_chunks, step, init)


def mlstm(zb, zs, conv_ml, i_bias, f_bias, ml_norm, batch, seq):
    m = zb.shape[0]
    blk = lambda off: pl.BlockSpec((seq, HEAD_W), lambda b, h: (b, off + h))
    cblk = lambda off: pl.BlockSpec((CONV_K, HEAD_W), lambda b, h: (0, off + h))
    smem = pl.BlockSpec(memory_space=pltpu.SMEM)
    vm = lambda w: pltpu.VMEM((seq, w), F32)
    return pl.pallas_call(
        _mlstm_kernel,
        grid=(batch, ML_HEADS),
        in_specs=[smem, smem, blk(CB_ML_Q), blk(CB_ML_K), blk(CB_ML_V), blk(CB_ML_O),
                  pl.BlockSpec((seq, LANE), lambda b, h: (b, 0)),
                  cblk(0), cblk(4),
                  pl.BlockSpec((1, HEAD_W), lambda b, h: (0, 0))],
        out_specs=pl.BlockSpec((seq, HEAD_W), lambda b, h: (b, h)),
        out_shape=jax.ShapeDtypeStruct((m, ML_HEADS * HEAD_W), BF16),
        scratch_shapes=[vm(HEAD_W), vm(HEAD_W), vm(1), vm(1)],
        compiler_params=_params(("parallel", "parallel")),
        name="mlstm",
    )(i_bias, f_bias, zb, zb, zb, zb, zs, conv_ml, conv_ml, ml_norm.reshape(1, HEAD_W))


def _diff_attn_kernel(lq1_ref, lk1_ref, lq2_ref, lk2_ref, q_ref, k_ref, v_ref, nw_ref, o_ref,
                      *, tq, tk, lambda_init):
    h = pl.program_id(1)
    qi = pl.program_id(2)
    lam = (jnp.exp(jnp.sum(lq1_ref[...] * lk1_ref[...], axis=-1, keepdims=True))
           - jnp.exp(jnp.sum(lq2_ref[...] * lk2_ref[...], axis=-1, keepdims=True)) + lambda_init)
    slope = jnp.exp2(jnp.full((1, 1), -(h + 1).astype(F32) * (8.0 / DIFF_HEADS), F32))

    lane = lax.broadcasted_iota(jnp.int32, (tq, HEAD_W), 1)
    q = q_ref[...] * (DIFF_DH ** -0.5)
    q0 = jnp.where(lane < DIFF_DH, q, 0.0).astype(BF16)
    q1 = jnp.where(lane >= DIFF_DH, q, 0.0).astype(BF16)
    rel = (lax.broadcasted_iota(jnp.int32, (tq, tk), 1)
           - lax.broadcasted_iota(jnp.int32, (tq, tk), 0))
    rel_bias = rel.astype(F32) * slope

    def block(kj, carry, masked):
        rows = pl.ds(pl.multiple_of(kj * tk, tk), tk)
        kb = k_ref[rows, :].astype(BF16)
        vb = v_ref[rows, :].astype(BF16)
        off = ((kj * tk - qi * tq).astype(F32)) * slope
        out = []
        for qm, (m_i, l_i, acc) in zip((q0, q1), carry):
            s = lax.dot_general(qm, kb, (((1,), (1,)), ((), ())), preferred_element_type=F32)
            s = s + rel_bias + off
            if masked:
                s = jnp.where(rel <= 0, s, NEG)
            m_new = jnp.maximum(m_i, jnp.max(s, axis=-1, keepdims=True))
            alpha = jnp.exp(m_i - m_new)
            p = jnp.exp(s - m_new)
            l_new = alpha * l_i + jnp.sum(p, axis=-1, keepdims=True)
            acc_new = alpha * acc + jnp.dot(p.astype(BF16), vb, preferred_element_type=F32)
            out.append((m_new, l_new, acc_new))
        return tuple(out)

    one = (jnp.full((tq, 1), NEG, F32), jnp.zeros((tq, 1), F32), jnp.zeros((tq, HEAD_W), F32))
    assert tq == tk
    carry = lax.fori_loop(0, qi, lambda kj, cr: block(kj, cr, False), (one, one))
    carry = block(qi, carry, True)
    (_, l0, a0), (_, l1, a1) = carry
    o = a0 / l0 - lam * (a1 / l1)
    o_ref[...] = (_rms(o, nw_ref[...]) * (1.0 - lambda_init)).astype(o_ref.dtype)


def diff_attention(zb, lq1, lk1, lq2, lk2, diff_norm, lambda_init, batch, seq, tq=256):
    m = zb.shape[0]
    tq = _tile(seq, tq)
    nq = seq // tq
    vec = pl.BlockSpec((1, DIFF_DH), lambda b, h, i: (0, 0))
    return pl.pallas_call(
        functools.partial(_diff_attn_kernel, tq=tq, tk=tq, lambda_init=lambda_init),
        grid=(batch, DIFF_HEADS, nq),
        in_specs=[vec, vec, vec, vec,
                  pl.BlockSpec((tq, HEAD_W), lambda b, h, i: (b * nq + i, CB_AQ + h)),
                  pl.BlockSpec((seq, HEAD_W), lambda b, h, i: (b, CB_AK + h)),
                  pl.BlockSpec((seq, HEAD_W), lambda b, h, i: (b, CB_AV + h)),
                  pl.BlockSpec((1, HEAD_W), lambda b, h, i: (0, 0))],
        out_specs=pl.BlockSpec((tq, HEAD_W), lambda b, h, i: (b * nq + i, h)),
        out_shape=jax.ShapeDtypeStruct((m, DIFF_HEADS * HEAD_W), BF16),
        compiler_params=_params(("parallel", "parallel", "arbitrary")),
        name="diff_attn",
    )(lq1.reshape(1, -1), lk1.reshape(1, -1), lq2.reshape(1, -1), lk2.reshape(1, -1),
      zb, zb, zb, diff_norm.reshape(1, HEAD_W))


def _out_proj_kernel(a_ref, b_ref, c_ref, wa_ref, wb_ref, wc_ref, h_ref, o_ref):
    acc = jnp.dot(a_ref[...], wa_ref[...], preferred_element_type=F32)
    acc += jnp.dot(b_ref[...], wb_ref[...], preferred_element_type=F32)
    acc += jnp.dot(c_ref[...], wc_ref[...], preferred_element_type=F32)
    o_ref[...] = h_ref[...] + acc


def out_proj(o_dn, o_diff, o_ml, wa, wb, wc, h, tm=512):
    m, d = h.shape
    tm = _tile(m, tm)
    row = lambda w: pl.BlockSpec((tm, w), lambda i: (i, 0))
    full = lambda w: pl.BlockSpec((w, d), lambda i: (0, 0))
    return pl.pallas_call(
        _out_proj_kernel,
        grid=(m // tm,),
        in_specs=[row(o_dn.shape[1]), row(o_diff.shape[1]), row(o_ml.shape[1]),
                  full(wa.shape[0]), full(wb.shape[0]), full(wc.shape[0]), row(d)],
        out_specs=row(d),
        out_shape=jax.ShapeDtypeStruct((m, d), F32),
        compiler_params=_params(("parallel",)),
        name="out_proj",
    )(o_dn, o_diff, o_ml, wa, wb, wc, h)


def _dense_ffn_kernel(h_ref, nw_ref, wg_ref, wu_ref, wd_ref, o_ref, xs_ref, acc_ref):
    f = pl.program_id(1)

    @pl.when(f == 0)
    def _():
        xs_ref[...] = _rms(h_ref[...], nw_ref[...]).astype(BF16)
        acc_ref[...] = jnp.zeros_like(acc_ref)

    xs = xs_ref[...]
    g = jnp.dot(xs, wg_ref[...], preferred_element_type=F32)
    u = jnp.dot(xs, wu_ref[...], preferred_element_type=F32)
    acc_ref[...] += jnp.dot((_silu(g) * u).astype(BF16), wd_ref[...], preferred_element_type=F32)

    @pl.when(f == pl.num_programs(1) - 1)
    def _():
        o_ref[...] = h_ref[...] + acc_ref[...]


def dense_ffn(h, nw, wg, wu, wd, tm=512, tf=512):
    m, d = h.shape
    dff = wg.shape[1]
    tm, tf = _tile(m, tm), _tile(dff, tf)
    return pl.pallas_call(
        _dense_ffn_kernel,
        grid=(m // tm, dff // tf),
        in_specs=[pl.BlockSpec((tm, d), lambda i, f: (i, 0)),
                  pl.BlockSpec((1, d), lambda i, f: (0, 0)),
                  pl.BlockSpec((d, tf), lambda i, f: (0, f)),
                  pl.BlockSpec((d, tf), lambda i, f: (0, f)),
                  pl.BlockSpec((tf, d), lambda i, f: (f, 0))],
        out_specs=pl.BlockSpec((tm, d), lambda i, f: (i, 0)),
        out_shape=jax.ShapeDtypeStruct((m, d), F32),
        scratch_shapes=[pltpu.VMEM((tm, d), BF16), pltpu.VMEM((tm, d), F32)],
        compiler_params=_params(("parallel", "arbitrary")),
        name="dense_ffn",
    )(h, nw.reshape(1, d), wg, wu, wd)


def _router_kernel(h_ref, nw_ref, wr_ref, xs_ref, idx_ref, wts_ref, cnt_ref, carry_ref):
    i = pl.program_id(0)
    tm = h_ref.shape[0]

    @pl.when(i == 0)
    def _():
        carry_ref[...] = jnp.zeros_like(carry_ref)

    xn = _rms(h_ref[...], nw_ref[...])
    xs_ref[...] = xn
    logits = _dot_hi(xn, wr_ref[...])
    lane = lax.broadcasted_iota(jnp.int32, logits.shape, 1)
    logits = jnp.where(lane < N_EXPERTS, logits, NEG)
    m1 = jnp.max(logits, axis=-1, keepdims=True)
    e1 = jnp.min(jnp.where(logits == m1, lane, LANE), axis=-1, keepdims=True)
    rest = jnp.where(lane == e1, NEG, logits)
    m2 = jnp.max(rest, axis=-1, keepdims=True)
    e2 = jnp.min(jnp.where(rest == m2, lane, LANE), axis=-1, keepdims=True)
    ex = jnp.exp(m2 - m1)
    w1 = 1.0 / (1.0 + ex)
    w2 = ex * w1
    oh1 = (lane == e1).astype(F32)
    oh2 = (lane == e2).astype(F32)
    oh = oh1 + oh2
    r = lax.broadcasted_iota(jnp.int32, (tm, tm), 0)
    c = lax.broadcasted_iota(jnp.int32, (tm, tm), 1)
    before = jnp.dot((r > c).astype(BF16), oh.astype(BF16), preferred_element_type=F32) + carry_ref[...]
    rank1 = jnp.sum(before * oh1, axis=-1, keepdims=True)
    rank2 = jnp.sum(before * oh2, axis=-1, keepdims=True)
    carry_ref[...] += jnp.sum(oh, axis=0, keepdims=True)
    cnt_ref[...] = jnp.broadcast_to(carry_ref[...], cnt_ref.shape)
    idx = jnp.where(lane == 0, e1, jnp.where(lane == 1, e2, 0))
    idx = jnp.where(lane == 2, rank1.astype(jnp.int32), jnp.where(lane == 3, rank2.astype(jnp.int32), idx))
    idx_ref[...] = idx
    wts_ref[...] = jnp.where(lane == 0, w1, jnp.where(lane == 1, w2, 0.0))


def moe_router(h, nw, router, tm=512):
    m, d = h.shape
    tm = _tile(m, tm)
    wr = jnp.zeros((d, LANE), F32).at[:, :N_EXPERTS].set(router)
    return pl.pallas_call(
        _router_kernel,
        grid=(m // tm,),
        in_specs=[pl.BlockSpec((tm, d), lambda i: (i, 0)),
                  pl.BlockSpec((1, d), lambda i: (0, 0)),
                  pl.BlockSpec((d, LANE), lambda i: (0, 0))],
        out_specs=[pl.BlockSpec((tm, d), lambda i: (i, 0)),
                   pl.BlockSpec((tm, LANE), lambda i: (i, 0)),
                   pl.BlockSpec((tm, LANE), lambda i: (i, 0)),
                   pl.BlockSpec((8, LANE), lambda i: (0, 0))],
        out_shape=[jax.ShapeDtypeStruct((m, d), F32),
                   jax.ShapeDtypeStruct((m, LANE), jnp.int32),
                   jax.ShapeDtypeStruct((m, LANE), F32),
                   jax.ShapeDtypeStruct((8, LANE), F32)],
        scratch_shapes=[pltpu.VMEM((1, LANE), F32)],
        compiler_params=_params(("arbitrary",)),
        name="moe_router",
    )(h, nw.reshape(1, d), wr)


def _dispatch_kernel(dest_ref, xs_ref, init_ref, out_ref, sem):
    del init_ref
    tm = xs_ref.shape[0]

    def copy(t, k):
        return pltpu.make_async_copy(xs_ref.at[pl.ds(t, 1), :],
                                     out_ref.at[pl.ds(dest_ref[2 * t + k], 1), :], sem)

    def start(t, c):
        copy(t, 0).start()
        copy(t, 1).start()
        return c

    def wait(t, c):
        copy(t, 0).wait()
        copy(t, 1).wait()
        return c

    lax.fori_loop(0, tm, start, 0)
    lax.fori_loop(0, tm, wait, 0)


def moe_dispatch(xs, dest, n_sorted, tm=256):
    m, d = xs.shape
    tm = _tile(m, tm)
    init = jnp.zeros((n_sorted, d), xs.dtype)
    return pl.pallas_call(
        _dispatch_kernel,
        grid=(m // tm,),
        in_specs=[pl.BlockSpec((2 * tm,), lambda i: (i,), memory_space=pltpu.SMEM),
                  pl.BlockSpec((tm, d), lambda i: (i, 0)),
                  pl.BlockSpec(memory_space=pl.ANY)],
        out_specs=pl.BlockSpec(memory_space=pl.ANY),
        out_shape=jax.ShapeDtypeStruct((n_sorted, d), xs.dtype),
        scratch_shapes=[pltpu.SemaphoreType.DMA(())],
        input_output_aliases={2: 0},
        compiler_params=_params(("arbitrary",)),
        name="moe_dispatch",
    )(dest, xs, init)


def _grouped_ffn_kernel(te_ref, na_ref, x_ref, wg_ref, wu_ref, wd_ref, o_ref, acc_ref):
    del te_ref
    i, f = pl.program_id(0), pl.program_id(1)
    active = i < na_ref[0]

    @pl.when(f == 0)
    def _():
        acc_ref[...] = jnp.zeros_like(acc_ref)

    @pl.when(active)
    def _():
        xs = x_ref[...].astype(BF16)
        g = jnp.dot(xs, wg_ref[...], preferred_element_type=F32)
        u = jnp.dot(xs, wu_ref[...], preferred_element_type=F32)
        acc_ref[...] += jnp.dot((_silu(g) * u).astype(BF16), wd_ref[...], preferred_element_type=F32)

    @pl.when(f == pl.num_programs(1) - 1)
    def _():
        o_ref[...] = acc_ref[...]


def grouped_ffn(xsorted, tile_expert, n_active, wg, wu, wd, tm, tf=512):
    ns, d = xsorted.shape
    dff = wg.shape[2]
    tf = _tile(dff, tf)
    nf = dff // tf
    n_tiles = ns // tm

    def fidx(i, f, na):
        return jnp.where(i < na[0], f, nf - 1)

    return pl.pallas_call(
        _grouped_ffn_kernel,
        grid_spec=pltpu.PrefetchScalarGridSpec(
            num_scalar_prefetch=2,
            grid=(n_tiles, nf),
            in_specs=[pl.BlockSpec((tm, d), lambda i, f, te, na: (i, 0)),
                      pl.BlockSpec((None, d, tf), lambda i, f, te, na: (te[i], 0, fidx(i, f, na))),
                      pl.BlockSpec((None, d, tf), lambda i, f, te, na: (te[i], 0, fidx(i, f, na))),
                      pl.BlockSpec((None, tf, d), lambda i, f, te, na: (te[i], fidx(i, f, na), 0))],
            out_specs=pl.BlockSpec((tm, d), lambda i, f, te, na: (i, 0)),
            scratch_shapes=[pltpu.VMEM((tm, d), F32)]),
        out_shape=jax.ShapeDtypeStruct((ns, d), F32),
        compiler_params=_params(("arbitrary", "arbitrary")),
        name="grouped_ffn",
    )(tile_expert, n_active, xsorted, wg, wu, wd)


def _combine_kernel(dest_ref, ys_ref, h_ref, wts_ref, o_ref, y0_ref, y1_ref, sem):
    tm = h_ref.shape[0]

    def copy(t, k):
        dst = (y0_ref, y1_ref)[k]
        return pltpu.make_async_copy(ys_ref.at[pl.ds(dest_ref[2 * t + k], 1), :],
                                     dst.at[pl.ds(t, 1), :], sem)

    def start(t, c):
        copy(t, 0).start()
        copy(t, 1).start()
        return c

    def wait(t, c):
        copy(t, 0).wait()
        copy(t, 1).wait()
        return c

    lax.fori_loop(0, tm, start, 0)
    lax.fori_loop(0, tm, wait, 0)
    wts = wts_ref[...]
    o_ref[...] = h_ref[...] + wts[:, 0:1] * y0_ref[...] + wts[:, 1:2] * y1_ref[...]


def moe_combine(ysorted, dest, h, wts, tm=256):
    m, d = h.shape
    tm = _tile(m, tm)
    return pl.pallas_call(
        _combine_kernel,
        grid=(m // tm,),
        in_specs=[pl.BlockSpec((2 * tm,), lambda i: (i,), memory_space=pltpu.SMEM),
                  pl.BlockSpec(memory_space=pl.ANY),
                  pl.BlockSpec((tm, d), lambda i: (i, 0)),
                  pl.BlockSpec((tm, LANE), lambda i: (i, 0))],
        out_specs=pl.BlockSpec((tm, d), lambda i: (i, 0)),
        out_shape=jax.ShapeDtypeStruct((m, d), F32),
        scratch_shapes=[pltpu.VMEM((tm, d), F32), pltpu.VMEM((tm, d), F32),
                        pltpu.SemaphoreType.DMA(())],
        compiler_params=_params(("arbitrary",)),
        name="moe_combine",
    )(dest, ysorted, h, wts)


def moe_ffn(h, nw, router, wg, wu, wd, tmg=512):
    m, d = h.shape
    tmg = min(tmg, m)
    xs, idx, wts, cnt = moe_router(h, nw, router)
    counts = cnt[0, :N_EXPERTS].astype(jnp.int32)
    padded = (counts + tmg - 1) // tmg * tmg
    ends = jnp.cumsum(padded)
    offsets = ends - padded
    n_tiles = (2 * m) // tmg + N_EXPERTS
    n_sorted = n_tiles * tmg
    dest = (offsets[idx[:, 0:2]] + idx[:, 2:4]).reshape(-1)
    tile_start = jnp.arange(n_tiles, dtype=jnp.int32) * tmg
    tile_expert = jnp.minimum(jnp.sum(tile_start[:, None] >= ends[None, :], axis=1), N_EXPERTS - 1)
    n_active = (ends[-1] // tmg).reshape(1).astype(jnp.int32)
    xsorted = moe_dispatch(xs, dest, n_sorted)
    ysorted = grouped_ffn(xsorted, tile_expert.astype(jnp.int32), n_active, wg, wu, wd, tmg)
    return moe_combine(ysorted, dest, h, wts)


def _ple_kernel(h_ref, p_ref, nw_ref, wg_ref, wp_ref, fw_ref, o_ref, *, final):
    h = h_ref[...]
    gate = _sigmoid(jnp.dot(_rms(h, nw_ref[...]).astype(BF16), wg_ref[...], preferred_element_type=F32))
    out = h + jnp.dot(p_ref[...].astype(BF16), wp_ref[...], preferred_element_type=F32) * gate
    o_ref[...] = _rms(out, fw_ref[...]) if final else out


def ple(h, p, nw, wg, wp, fw, final, tm=512):
    m, d = h.shape
    pd = p.shape[1]
    tm = _tile(m, tm)
    return pl.pallas_call(
        functools.partial(_ple_kernel, final=final),
        grid=(m // tm,),
        in_specs=[pl.BlockSpec((tm, d), lambda i: (i, 0)),
                  pl.BlockSpec((tm, pd), lambda i: (i, 0)),
                  pl.BlockSpec((1, d), lambda i: (0, 0)),
                  pl.BlockSpec((d, d), lambda i: (0, 0)),
                  pl.BlockSpec((pd, d), lambda i: (0, 0)),
                  pl.BlockSpec((1, d), lambda i: (0, 0))],
        out_specs=pl.BlockSpec((tm, d), lambda i: (i, 0)),
        out_shape=jax.ShapeDtypeStruct((m, d), F32),
        compiler_params=_params(("parallel",)),
        name="ple",
    )(h, p, nw.reshape(1, d), wg, wp, fw.reshape(1, d))


def _split_w_in(w):
    d = w.shape[0]
    big = jnp.concatenate([w[:, :3072], w[:, 3080:7176]], axis=1).astype(BF16)
    small = jnp.concatenate([w[:, 3072:3080], w[:, 7176:7184],
                             jnp.zeros((d, LANE - 16), w.dtype)], axis=1).astype(BF16)
    return big, small


def kernel(x, p, attn_norm, w_in, conv_dn, conv_ml, dn_a_log, dn_dt_bias, dn_norm, diff_lq1, diff_lk1, diff_lq2, diff_lk2, diff_norm, ml_i_bias, ml_f_bias, ml_norm, w_out, ffn_norm, dense_w_gate, dense_w_up, dense_w_down, router, moe_w_gate, moe_w_up, moe_w_down, ple_norm, ple_proj, ple_gate, final_norm):
    batch, seq, d = x.shape
    depth = w_in.shape[0]
    m = batch * seq
    h = x.reshape(m, d)
    dn_w, diff_w = DN_HEADS * HEAD_W, DIFF_HEADS * HEAD_W
    for i in range(depth):
        lambda_init = 0.8 - 0.6 * math.exp(-0.3 * i)
        w_big, w_small = _split_w_in(w_in[i])
        zb, zs = norm_proj(h, attn_norm[i], w_big, w_small)
        o_dn = deltanet(zb, zs, conv_dn[i], dn_a_log[i], dn_dt_bias[i], dn_norm[i], batch, seq)
        o_diff = diff_attention(zb, diff_lq1[i], diff_lk1[i], diff_lq2[i], diff_lk2[i], diff_norm[i],
                                lambda_init, batch, seq)
        o_ml = mlstm(zb, zs, conv_ml[i], ml_i_bias[i], ml_f_bias[i], ml_norm[i], batch, seq)
        wo = w_out[i].astype(BF16)
        h = out_proj(o_dn, o_diff, o_ml, wo[:dn_w], wo[dn_w:dn_w + diff_w], wo[dn_w + diff_w:], h)
        j = i // 2
        if i % 2 == 0:
            h = dense_ffn(h, ffn_norm[i], dense_w_gate[j].astype(BF16), dense_w_up[j].astype(BF16),
                          dense_w_down[j].astype(BF16))
        else:
            h = moe_ffn(h, ffn_norm[i], router[j], moe_w_gate[j].astype(BF16), moe_w_up[j].astype(BF16),
                        moe_w_down[j].astype(BF16))
        h = ple(h, p[i].reshape(m, -1), ple_norm[i], ple_gate[i].astype(BF16), ple_proj[i].astype(BF16),
                final_norm, final=(i == depth - 1))
    return h.reshape(batch, seq, d)
```

```python
import functools
import math

import numpy as np
import jax
import jax.numpy as jnp
from jax import lax
from jax.experimental import pallas as pl
from jax.experimental.pallas import tpu as pltpu

F32 = jnp.float32
BF16 = jnp.bfloat16
EPS = 1e-6
LANE = 128
NEG = -1e30

DN_HEADS = 4
DIFF_HEADS = 8
ML_HEADS = 4
HEAD_W = 128
DIFF_DH = 64
CHUNK = 64
CONV_K = 4
N_EXPERTS = 8
VMEM_LIMIT = 56 * 1024 * 1024

CB_DN_Q, CB_DN_K, CB_DN_V = 0, 4, 8
CB_ML_Q, CB_ML_K = 12, 16
CB_DN_Z = 20
CB_AQ, CB_AK, CB_AV = 24, 32, 40
CB_ML_V, CB_ML_O = 48, 52
N_BIG = 56 * LANE
SL_DN_B, SL_DN_A, SL_ML_I, SL_ML_F = 0, 4, 8, 12


def _params(sem):
    return pltpu.CompilerParams(dimension_semantics=sem, vmem_limit_bytes=VMEM_LIMIT)


def _dot(a, b):
    return jnp.dot(a.astype(BF16), b.astype(BF16), preferred_element_type=F32)


def _dot_nt(a, b):
    return lax.dot_general(a.astype(BF16), b.astype(BF16), (((1,), (1,)), ((), ())),
                           preferred_element_type=F32)


def _dot_tn(a, b):
    return lax.dot_general(a.astype(BF16), b.astype(BF16), (((0,), (0,)), ((), ())),
                           preferred_element_type=F32)


def _dot_hi(a, b):
    return jnp.dot(a, b, preferred_element_type=F32, precision=lax.Precision.HIGHEST)


def _rms(x, w):
    return x * lax.rsqrt(jnp.mean(x * x, axis=-1, keepdims=True) + EPS) * w


def _sigmoid(x):
    return 1.0 / (1.0 + jnp.exp(-x))


def _silu(x):
    return x * _sigmoid(x)


def _softplus(x):
    return jnp.maximum(x, 0.0) + jnp.log(1.0 + jnp.exp(-jnp.abs(x)))


def _log_sigmoid(x):
    return -_softplus(-x)


def _tile(m, t):
    t = min(m, t)
    assert m % t == 0
    return t


def _norm_proj_kernel(x_ref, nw_ref, w_ref, ws_ref, o_ref, os_ref, xs_ref):
    @pl.when(pl.program_id(1) == 0)
    def _():
        xs_ref[...] = _rms(x_ref[...], nw_ref[...]).astype(BF16)
        os_ref[...] = jnp.dot(xs_ref[...], ws_ref[...], preferred_element_type=F32)

    o_ref[...] = jnp.dot(xs_ref[...], w_ref[...], preferred_element_type=F32).astype(o_ref.dtype)


def norm_proj(x, nw, w_big, w_small, tm=512, tn=1024):
    m, d = x.shape
    n = w_big.shape[1]
    tm, tn = _tile(m, tm), _tile(n, tn)
    return pl.pallas_call(
        _norm_proj_kernel,
        grid=(m // tm, n // tn),
        in_specs=[pl.BlockSpec((tm, d), lambda i, j: (i, 0)),
                  pl.BlockSpec((1, d), lambda i, j: (0, 0)),
                  pl.BlockSpec((d, tn), lambda i, j: (0, j)),
                  pl.BlockSpec((d, LANE), lambda i, j: (0, 0))],
        out_specs=[pl.BlockSpec((tm, tn), lambda i, j: (i, j)),
                   pl.BlockSpec((tm, LANE), lambda i, j: (i, 0))],
        out_shape=[jax.ShapeDtypeStruct((m, n), BF16), jax.ShapeDtypeStruct((m, LANE), F32)],
        scratch_shapes=[pltpu.VMEM((tm, d), BF16)],
        compiler_params=_params(("parallel", "arbitrary")),
        name="norm_proj",
    )(x, nw.reshape(1, d), w_big, w_small)


def _chunk_conv_silu(ref, w, ci, rows):
    cur = ref[rows, :].astype(F32)
    prev_rows = pl.ds(pl.multiple_of(jnp.maximum(ci * CHUNK - 16, 0), 16), 16)
    prev = jnp.where(ci > 0, ref[prev_rows, :].astype(F32)[8:16, :], 0.0)
    ext = jnp.concatenate([prev, cur], axis=0)
    y = cur * w[CONV_K - 1:CONV_K, :]
    for back in range(1, CONV_K):
        y = y + pltpu.roll(ext, back, axis=0)[8:8 + CHUNK, :] * w[CONV_K - 1 - back:CONV_K - back, :]
    return _silu(y)


def _dot2(l_bf16, x):
    hi = x.astype(BF16)
    lo = (x - hi.astype(F32)).astype(BF16)
    return (jnp.dot(l_bf16, hi, preferred_element_type=F32)
            + jnp.dot(l_bf16, lo, preferred_element_type=F32))


def _chunk_masks():
    r = lax.broadcasted_iota(jnp.int32, (CHUNK, CHUNK), 0)
    c = lax.broadcasted_iota(jnp.int32, (CHUNK, CHUNK), 1)
    return r, c


def _chunk_rows(ci):
    return pl.ds(pl.multiple_of(ci * CHUNK, CHUNK), CHUNK)


assert DN_HEADS == ML_HEADS
HEADS = range(DN_HEADS)
COLS = [slice(h * HEAD_W, (h + 1) * HEAD_W) for h in HEADS]


def _deltanet_kernel(alog_ref, dtb_ref, q_ref, k_ref, v_ref, zg_ref, zs_ref, cw_ref, nw_ref, o_ref,
                     u_s, w_s, at_s, qg_s, kg_s, gl_s):
    n_chunks = q_ref.shape[0] // CHUNK
    width = DN_HEADS * HEAD_W
    r, c = _chunk_masks()
    low_incl = (r >= c).astype(BF16)
    strict_up = (r > c).astype(F32)
    tri_incl = r >= c
    tri_strict = r > c
    cw = cw_ref[...]

    def prep(ci, carry):
        rows = _chunk_rows(ci)
        qa = _chunk_conv_silu(q_ref, cw[:, :width], ci, rows)
        ka = _chunk_conv_silu(k_ref, cw[:, width:2 * width], ci, rows)
        va = _chunk_conv_silu(v_ref, cw[:, 2 * width:], ci, rows)
        zs = zs_ref[rows, :]
        qc, kc, kb, kcb, beta, g = [], [], [], [], [], []
        for h in HEADS:
            qh, kh = qa[:, COLS[h]], ka[:, COLS[h]]
            qc.append(qh * (lax.rsqrt(jnp.sum(qh * qh, axis=-1, keepdims=True) + EPS) * (HEAD_W ** -0.5)))
            kc.append(kh * lax.rsqrt(jnp.sum(kh * kh, axis=-1, keepdims=True) + EPS))
            beta.append(_sigmoid(zs[:, SL_DN_B + h:SL_DN_B + h + 1]))
            rate = jnp.exp(jnp.full((1, 1), alog_ref[h], F32))
            g.append(-rate * _softplus(zs[:, SL_DN_A + h:SL_DN_A + h + 1] + dtb_ref[h]))
            kb.append(kc[h] * beta[h])
            kcb.append(kc[h].astype(BF16))
        dmat = [_dot2(low_incl, jnp.broadcast_to(g[h], (CHUNK, CHUNK)) * strict_up) for h in HEADS]
        kk = [_dot_nt(kb[h], kcb[h]) for h in HEADS]
        qk = [_dot_nt(qc[h], kcb[h]) for h in HEADS]
        gc = [dmat[h][:, :1] + g[h][:1, :] for h in HEADS]
        decay = [jnp.where(tri_incl, jnp.exp(dmat[h]), 0.0) for h in HEADS]
        eg = [jnp.exp(gc[h]) for h in HEADS]
        a = [jnp.where(tri_strict, kk[h] * decay[h], 0.0).astype(BF16) for h in HEADS]
        x = [jnp.concatenate([va[:, COLS[h]] * beta[h], kb[h] * eg[h]], axis=1) for h in HEADS]
        ax = [_dot(a[h], x[h]) for h in HEADS]
        p = [_dot(a[h], a[h]).astype(BF16) for h in HEADS]
        x = [x[h] - ax[h] for h in HEADS]
        for step in range(5):
            px = [_dot(p[h], x[h]) for h in HEADS]
            if step < 4:
                p = [_dot(p[h], p[h]).astype(BF16) for h in HEADS]
            x = [x[h] + px[h] for h in HEADS]
        for h in HEADS:
            at_s[rows, h * HEAD_W:h * HEAD_W + CHUNK] = (qk[h] * decay[h]).astype(BF16)
            u_s[rows, COLS[h]] = x[h][:, :HEAD_W]
            w_s[rows, COLS[h]] = x[h][:, HEAD_W:].astype(BF16)
            g_last = gc[h][CHUNK - 1:CHUNK, :]
            qg_s[rows, COLS[h]] = (qc[h] * eg[h]).astype(BF16)
            kg_s[rows, COLS[h]] = (kc[h] * jnp.exp(g_last - gc[h])).astype(BF16)
            gl_s[pl.ds(pl.multiple_of(ci * 8, 8), 8), COLS[h]] = jnp.broadcast_to(jnp.exp(g_last), (8, HEAD_W))
        return carry

    lax.fori_loop(0, n_chunks, prep, 0)

    nw = nw_ref[...]

    def scan(ci, states):
        rows = _chunk_rows(ci)
        sb = [states[h].astype(BF16) for h in HEADS]
        ws = [jnp.dot(w_s[rows, COLS[h]], sb[h], preferred_element_type=F32) for h in HEADS]
        qs = [jnp.dot(qg_s[rows, COLS[h]], sb[h], preferred_element_type=F32) for h in HEADS]
        vb = [(u_s[rows, COLS[h]] - ws[h]).astype(BF16) for h in HEADS]
        av = [jnp.dot(at_s[rows, h * HEAD_W:h * HEAD_W + CHUNK], vb[h], preferred_element_type=F32)
              for h in HEADS]
        kv = [_dot_tn(kg_s[rows, COLS[h]], vb[h]) for h in HEADS]
        out = []
        for h in HEADS:
            gl = gl_s[pl.ds(pl.multiple_of(ci * 8, 8), 1), COLS[h]]
            out.append(states[h] * gl + kv[h])
            o_ref[rows, COLS[h]] = (_rms(qs[h] + av[h], nw)
                                    * _silu(zg_ref[rows, COLS[h]].astype(F32))).astype(o_ref.dtype)
        return tuple(out)

    lax.fori_loop(0, n_chunks, scan, tuple(jnp.zeros((HEAD_W, HEAD_W), F32) for _ in range(DN_HEADS)))


def deltanet(zb, zs, conv_dn, a_log, dt_bias, dn_norm, batch, seq):
    m = zb.shape[0]
    width = DN_HEADS * HEAD_W
    blk = lambda cb: pl.BlockSpec((seq, width), lambda b: (b, cb // DN_HEADS))
    smem = pl.BlockSpec(memory_space=pltpu.SMEM)
    return pl.pallas_call(
        _deltanet_kernel,
        grid=(batch,),
        in_specs=[smem, smem, blk(CB_DN_Q), blk(CB_DN_K), blk(CB_DN_V), blk(CB_DN_Z),
                  pl.BlockSpec((seq, LANE), lambda b: (b, 0)),
                  pl.BlockSpec((CONV_K, 3 * width), lambda b: (0, 0)),
                  pl.BlockSpec((1, HEAD_W), lambda b: (0, 0))],
        out_specs=pl.BlockSpec((seq, width), lambda b: (b, 0)),
        out_shape=jax.ShapeDtypeStruct((m, width), BF16),
        scratch_shapes=[pltpu.VMEM((seq, width), F32), pltpu.VMEM((seq, width), BF16),
                        pltpu.VMEM((seq, width), BF16), pltpu.VMEM((seq, width), BF16),
                        pltpu.VMEM((seq, width), BF16),
                        pltpu.VMEM((seq // CHUNK * 8, width), F32)],
        compiler_params=_params(("parallel",)),
        name="deltanet",
    )(a_log, dt_bias, zb, zb, zb, zb, zs, conv_dn, dn_norm.reshape(1, HEAD_W))


def _mlstm_kernel(ib_ref, fb_ref, q_ref, k_ref, v_ref, og_ref, zs_ref, cw_ref, nw_ref, o_ref):
    n_chunks = q_ref.shape[0] // CHUNK
    width = ML_HEADS * HEAD_W
    r, c = _chunk_masks()
    low_incl = (r >= c).astype(BF16)
    strict_up = (r > c).astype(F32)
    eye = (r == c).astype(F32)
    tri_incl = r >= c
    nw = nw_ref[...]
    cw = cw_ref[...]

    def step(ci, carry):
        rows = _chunk_rows(ci)
        qa = _chunk_conv_silu(q_ref, cw[:, :width], ci, rows)
        ka = _chunk_conv_silu(k_ref, cw[:, width:], ci, rows) * (HEAD_W ** -0.5)
        zs = zs_ref[rows, :]
        qc = [qa[:, COLS[h]] for h in HEADS]
        kc = [ka[:, COLS[h]] for h in HEADS]
        vc = [v_ref[rows, COLS[h]] for h in HEADS]
        ip = [zs[:, SL_ML_I + h:SL_ML_I + h + 1] + ib_ref[h] for h in HEADS]
        lf = [_log_sigmoid(zs[:, SL_ML_F + h:SL_ML_F + h + 1] + fb_ref[h]) for h in HEADS]
        dl = [_dot2(low_incl, jnp.broadcast_to(lf[h], (CHUNK, CHUNK)) * strict_up
                    + jnp.broadcast_to(ip[h], (CHUNK, CHUNK)) * eye) for h in HEADS]
        qk = [_dot_nt(qc[h], kc[h]) for h in HEADS]
        qcs = [_dot(qc[h], carry[h][0]) for h in HEADS]
        s, wk, inter, m_t, m_new, scale = [], [], [], [], [], []
        for h in HEADS:
            m_st = carry[h][2]
            b = dl[h][:, :1] - ip[h][:1, :] + lf[h][:1, :]
            dlog = jnp.where(tri_incl, dl[h], NEG)
            dmax = jnp.max(dlog, axis=-1, keepdims=True)
            b_last = b[CHUNK - 1:CHUNK, :]
            a = b_last - b + ip[h]
            a_max = jnp.max(a, axis=0, keepdims=True)
            m_t.append(jnp.maximum(dmax, b + m_st))
            s.append(qk[h] * jnp.exp(dlog - m_t[h]))
            inter.append(jnp.exp(b + m_st - m_t[h]))
            m_new.append(jnp.maximum(b_last + m_st, a_max))
            scale.append(jnp.exp(b_last + m_st - m_new[h]))
            wk.append(kc[h] * jnp.exp(a - m_new[h]))
        sv = [_dot(s[h], vc[h]) for h in HEADS]
        kv = [_dot_tn(wk[h], vc[h]) for h in HEADS]
        out = []
        for h in HEADS:
            c_st, n_st, _ = carry[h]
            num = sv[h] + inter[h] * qcs[h]
            den = (jnp.sum(s[h], axis=-1, keepdims=True)
                   + inter[h] * jnp.sum(qc[h] * n_st, axis=-1, keepdims=True))
            hh = num / jnp.maximum(jnp.abs(den), jnp.exp(-m_t[h]))
            out.append((scale[h] * c_st + kv[h],
                        scale[h] * n_st + jnp.sum(wk[h], axis=0, keepdims=True),
                        m_new[h]))
            o_ref[rows, COLS[h]] = (_sigmoid(og_ref[rows, COLS[h]].astype(F32))
                                    * _rms(hh, nw)).astype(o_ref.dtype)
        return tuple(out)

    init = tuple((jnp.zeros((HEAD_W, HEAD_W), F32), jnp.zeros((1, HEAD_W), F32), jnp.zeros((1, 1), F32))
                 for _ in range(ML_HEADS))
    lax.fori_loop(0, n_chunks, step, init)


def mlstm(zb, zs, conv_ml, i_bias, f_bias, ml_norm, batch, seq):
    m = zb.shape[0]
    width = ML_HEADS * HEAD_W
    blk = lambda cb: pl.BlockSpec((seq, width), lambda b: (b, cb // ML_HEADS))
    smem = pl.BlockSpec(memory_space=pltpu.SMEM)
    return pl.pallas_call(
        _mlstm_kernel,
        grid=(batch,),
        in_specs=[smem, smem, blk(CB_ML_Q), blk(CB_ML_K), blk(CB_ML_V), blk(CB_ML_O),
                  pl.BlockSpec((seq, LANE), lambda b: (b, 0)),
                  pl.BlockSpec((CONV_K, 2 * width), lambda b: (0, 0)),
                  pl.BlockSpec((1, HEAD_W), lambda b: (0, 0))],
        out_specs=pl.BlockSpec((seq, width), lambda b: (b, 0)),
        out_shape=jax.ShapeDtypeStruct((m, width), BF16),
        compiler_params=_params(("parallel",)),
        name="mlstm",
    )(i_bias, f_bias, zb, zb, zb, zb, zs, conv_ml, ml_norm.reshape(1, HEAD_W))


LOG2E = math.log2(math.e)
LOG2E_HI = float(np.asarray(LOG2E, dtype=BF16))
LOG2E_LO = LOG2E - LOG2E_HI
SLAB = 128


def _diff_attn_kernel(lq1_ref, lk1_ref, lq2_ref, lk2_ref, q_ref, k_ref, v_ref, nw_ref, o_ref,
                      k0_s, k1_s, *, tq, lambda_init):
    h = pl.program_id(1)
    qi = pl.program_id(2)
    tk = tq
    assert tk <= 256
    slope = jnp.exp2(jnp.full((1, 1), -(h + 1).astype(F32) * (8.0 / DIFF_HEADS), F32))

    @pl.when(qi == 0)
    def _():
        kk = k_ref[...]
        lane = lax.broadcasted_iota(jnp.int32, kk.shape, 1)
        pos = lax.broadcasted_iota(jnp.int32, kk.shape, 0) & (tk - 1)
        ramp = (pos.astype(F32) * slope).astype(BF16)
        k0_s[...] = jnp.where(lane == DIFF_DH, ramp, jnp.where(lane == DIFF_DH + 1, ramp, kk))
        k1_s[...] = jnp.where(lane == 0, ramp, jnp.where(lane == 1, ramp, kk))

    lam = (jnp.exp(jnp.sum(lq1_ref[...] * lk1_ref[...], axis=-1, keepdims=True))
           - jnp.exp(jnp.sum(lq2_ref[...] * lk2_ref[...], axis=-1, keepdims=True)) + lambda_init)
    lane = lax.broadcasted_iota(jnp.int32, (tq, HEAD_W), 1)
    q = q_ref[...].astype(F32) * (DIFF_DH ** -0.5 * LOG2E)
    q0 = jnp.where(lane < DIFF_DH, q,
                   jnp.where(lane == DIFF_DH, LOG2E_HI, jnp.where(lane == DIFF_DH + 1, LOG2E_LO, 0.0)))
    q1 = jnp.where(lane >= DIFF_DH, q, jnp.where(lane == 0, LOG2E_HI, jnp.where(lane == 1, LOG2E_LO, 0.0)))
    q0, q1 = q0.astype(BF16), q1.astype(BF16)
    slope2 = slope * LOG2E
    per_map = tq // SLAB
    n_slab = 2 * per_map
    q_slab = [qm[j * SLAB:(j + 1) * SLAB, :] for qm in (q0, q1) for j in range(per_map)]

    def block(kj, carry, masked):
        rows = pl.ds(pl.multiple_of(kj * tk, tk), tk)
        kb = (k0_s[rows, :], k1_s[rows, :])
        vb = v_ref[rows, :]
        off = slope2 * (kj * tk).astype(F32)

        def scores(i):
            return _dot_nt(q_slab[i], kb[i // per_map])

        def finish(i, s):
            m_i, l_i, acc = carry[i]
            if masked:
                ri = lax.broadcasted_iota(jnp.int32, (SLAB, tk), 0) + (i % per_map) * SLAB
                ci = lax.broadcasted_iota(jnp.int32, (SLAB, tk), 1)
                s = jnp.where(ci <= ri, s, NEG)
            m_new = jnp.maximum(m_i, jnp.max(s, axis=-1, keepdims=True) + off)
            p = jnp.exp2(s - (m_new - off))
            alpha = jnp.exp2(m_i - m_new)
            l_new = alpha * l_i + sum(p[:, j * LANE:(j + 1) * LANE] for j in range(tk // LANE))
            acc_new = alpha * acc + jnp.dot(p.astype(BF16), vb, preferred_element_type=F32)
            return m_new, l_new, acc_new

        out = []
        s_next = scores(0)
        for i in range(n_slab):
            s_cur = s_next
            if i + 1 < n_slab:
                s_next = scores(i + 1)
            out.append(finish(i, s_cur))
        return tuple(out)

    init = tuple((jnp.full((SLAB, 1), NEG, F32), jnp.zeros((SLAB, LANE), F32), jnp.zeros((SLAB, HEAD_W), F32))
                 for _ in range(n_slab))
    carry = lax.fori_loop(0, qi // 2, lambda kp, cr: block(2 * kp + 1, block(2 * kp, cr, False), False), init)
    carry = lax.cond(qi % 2 == 1, lambda cr: block(qi - 1, cr, False), lambda cr: cr, carry)
    carry = block(qi, carry, True)
    nw = nw_ref[...]
    for j in range(per_map):
        (_, l0, a0), (_, l1, a1) = carry[j], carry[per_map + j]
        l0 = jnp.sum(l0, axis=-1, keepdims=True)
        l1 = jnp.sum(l1, axis=-1, keepdims=True)
        o = a0 / l0 - lam * (a1 / l1)
        o_ref[j * SLAB:(j + 1) * SLAB, :] = (_rms(o, nw) * (1.0 - lambda_init)).astype(o_ref.dtype)


def diff_attention(zb, lq1, lk1, lq2, lk2, diff_norm, lambda_init, batch, seq, tq=256):
    m = zb.shape[0]
    tq = _tile(seq, tq)
    nq = seq // tq
    vec = pl.BlockSpec((1, DIFF_DH), lambda b, h, i: (0, 0))
    return pl.pallas_call(
        functools.partial(_diff_attn_kernel, tq=tq, lambda_init=lambda_init),
        grid=(batch, DIFF_HEADS, nq),
        in_specs=[vec, vec, vec, vec,
                  pl.BlockSpec((tq, HEAD_W), lambda b, h, i: (b * nq + i, CB_AQ + h)),
                  pl.BlockSpec((seq, HEAD_W), lambda b, h, i: (b, CB_AK + h)),
                  pl.BlockSpec((seq, HEAD_W), lambda b, h, i: (b, CB_AV + h)),
                  pl.BlockSpec((1, HEAD_W), lambda b, h, i: (0, 0))],
        out_specs=pl.BlockSpec((tq, HEAD_W), lambda b, h, i: (b * nq + i, h)),
        out_shape=jax.ShapeDtypeStruct((m, DIFF_HEADS * HEAD_W), BF16),
        scratch_shapes=[pltpu.VMEM((seq, HEAD_W), BF16), pltpu.VMEM((seq, HEAD_W), BF16)],
        compiler_params=_params(("parallel", "parallel", "arbitrary")),
        name="diff_attn",
    )(lq1.reshape(1, -1), lk1.reshape(1, -1), lq2.reshape(1, -1), lk2.reshape(1, -1),
      zb, zb, zb, diff_norm.reshape(1, HEAD_W))


def _out_proj_kernel(a_ref, b_ref, c_ref, wa_ref, wb_ref, wc_ref, h_ref, o_ref):
    acc = jnp.dot(a_ref[...], wa_ref[...], preferred_element_type=F32)
    acc += jnp.dot(b_ref[...], wb_ref[...], preferred_element_type=F32)
    acc += jnp.dot(c_ref[...], wc_ref[...], preferred_element_type=F32)
    o_ref[...] = h_ref[...] + acc


def out_proj(o_dn, o_diff, o_ml, wa, wb, wc, h, tm=512):
    m, d = h.shape
    tm = _tile(m, tm)
    row = lambda w: pl.BlockSpec((tm, w), lambda i: (i, 0))
    full = lambda w: pl.BlockSpec((w, d), lambda i: (0, 0))
    return pl.pallas_call(
        _out_proj_kernel,
        grid=(m // tm,),
        in_specs=[row(o_dn.shape[1]), row(o_diff.shape[1]), row(o_ml.shape[1]),
                  full(wa.shape[0]), full(wb.shape[0]), full(wc.shape[0]), row(d)],
        out_specs=row(d),
        out_shape=jax.ShapeDtypeStruct((m, d), F32),
        compiler_params=_params(("parallel",)),
        name="out_proj",
    )(o_dn, o_diff, o_ml, wa, wb, wc, h)


def _dense_ffn_kernel(h_ref, nw_ref, wg_ref, wu_ref, wd_ref, o_ref, xs_ref, acc_ref):
    f = pl.program_id(1)

    @pl.when(f == 0)
    def _():
        xs_ref[...] = _rms(h_ref[...], nw_ref[...]).astype(BF16)
        acc_ref[...] = jnp.zeros_like(acc_ref)

    xs = xs_ref[...]
    g = jnp.dot(xs, wg_ref[...], preferred_element_type=F32)
    u = jnp.dot(xs, wu_ref[...], preferred_element_type=F32)
    acc_ref[...] += jnp.dot((_silu(g) * u).astype(BF16), wd_ref[...], preferred_element_type=F32)

    @pl.when(f == pl.num_programs(1) - 1)
    def _():
        o_ref[...] = h_ref[...] + acc_ref[...]


def dense_ffn(h, nw, wg, wu, wd, tm=512, tf=512):
    m, d = h.shape
    dff = wg.shape[1]
    tm, tf = _tile(m, tm), _tile(dff, tf)
    return pl.pallas_call(
        _dense_ffn_kernel,
        grid=(m // tm, dff // tf),
        in_specs=[pl.BlockSpec((tm, d), lambda i, f: (i, 0)),
                  pl.BlockSpec((1, d), lambda i, f: (0, 0)),
                  pl.BlockSpec((d, tf), lambda i, f: (0, f)),
                  pl.BlockSpec((d, tf), lambda i, f: (0, f)),
                  pl.BlockSpec((tf, d), lambda i, f: (f, 0))],
        out_specs=pl.BlockSpec((tm, d), lambda i, f: (i, 0)),
        out_shape=jax.ShapeDtypeStruct((m, d), F32),
        scratch_shapes=[pltpu.VMEM((tm, d), BF16), pltpu.VMEM((tm, d), F32)],
        compiler_params=_params(("parallel", "arbitrary")),
        name="dense_ffn",
    )(h, nw.reshape(1, d), wg, wu, wd)


def _router_kernel(h_ref, nw_ref, wr_ref, xs_ref, idx_ref, wts_ref, cnt_ref, carry_ref):
    i = pl.program_id(0)
    tm = h_ref.shape[0]

    @pl.when(i == 0)
    def _():
        carry_ref[...] = jnp.zeros_like(carry_ref)

    xn = _rms(h_ref[...], nw_ref[...])
    xs_ref[...] = xn
    logits = _dot_hi(xn, wr_ref[...])
    lane = lax.broadcasted_iota(jnp.int32, logits.shape, 1)
    logits = jnp.where(lane < N_EXPERTS, logits, NEG)
    m1 = jnp.max(logits, axis=-1, keepdims=True)
    e1 = jnp.min(jnp.where(logits == m1, lane, LANE), axis=-1, keepdims=True)
    rest = jnp.where(lane == e1, NEG, logits)
    m2 = jnp.max(rest, axis=-1, keepdims=True)
    e2 = jnp.min(jnp.where(rest == m2, lane, LANE), axis=-1, keepdims=True)
    ex = jnp.exp(m2 - m1)
    w1 = 1.0 / (1.0 + ex)
    w2 = ex * w1
    oh1 = (lane == e1).astype(F32)
    oh2 = (lane == e2).astype(F32)
    oh = oh1 + oh2
    r = lax.broadcasted_iota(jnp.int32, (tm, tm), 0)
    c = lax.broadcasted_iota(jnp.int32, (tm, tm), 1)
    before = jnp.dot((r > c).astype(BF16), oh.astype(BF16), preferred_element_type=F32) + carry_ref[...]
    rank1 = jnp.sum(before * oh1, axis=-1, keepdims=True)
    rank2 = jnp.sum(before * oh2, axis=-1, keepdims=True)
    carry_ref[...] += jnp.sum(oh, axis=0, keepdims=True)
    cnt_ref[...] = jnp.broadcast_to(carry_ref[...], cnt_ref.shape)
    idx = jnp.where(lane == 0, e1, jnp.where(lane == 1, e2, 0))
    idx = jnp.where(lane == 2, rank1.astype(jnp.int32), jnp.where(lane == 3, rank2.astype(jnp.int32), idx))
    idx_ref[...] = idx
    wts_ref[...] = jnp.where(lane == 0, w1, jnp.where(lane == 1, w2, 0.0))


def moe_router(h, nw, router, tm=512):
    m, d = h.shape
    tm = _tile(m, tm)
    wr = jnp.zeros((d, LANE), F32).at[:, :N_EXPERTS].set(router)
    return pl.pallas_call(
        _router_kernel,
        grid=(m // tm,),
        in_specs=[pl.BlockSpec((tm, d), lambda i: (i, 0)),
                  pl.BlockSpec((1, d), lambda i: (0, 0)),
                  pl.BlockSpec((d, LANE), lambda i: (0, 0))],
        out_specs=[pl.BlockSpec((tm, d), lambda i: (i, 0)),
                   pl.BlockSpec((tm, LANE), lambda i: (i, 0)),
                   pl.BlockSpec((tm, LANE), lambda i: (i, 0)),
                   pl.BlockSpec((8, LANE), lambda i: (0, 0))],
        out_shape=[jax.ShapeDtypeStruct((m, d), F32),
                   jax.ShapeDtypeStruct((m, LANE), jnp.int32),
                   jax.ShapeDtypeStruct((m, LANE), F32),
                   jax.ShapeDtypeStruct((8, LANE), F32)],
        scratch_shapes=[pltpu.VMEM((1, LANE), F32)],
        compiler_params=_params(("arbitrary",)),
        name="moe_router",
    )(h, nw.reshape(1, d), wr)


def _dispatch_kernel(dest_ref, xs_ref, init_ref, out_ref, sem):
    del init_ref
    tm = xs_ref.shape[0]

    def copy(t, k):
        return pltpu.make_async_copy(xs_ref.at[pl.ds(t, 1), :],
                                     out_ref.at[pl.ds(dest_ref[2 * t + k], 1), :], sem)

    def start(t, c):
        copy(t, 0).start()
        copy(t, 1).start()
        return c

    def wait(t, c):
        copy(t, 0).wait()
        copy(t, 1).wait()
        return c

    lax.fori_loop(0, tm, start, 0)
    lax.fori_loop(0, tm, wait, 0)


def moe_dispatch(xs, dest, n_sorted, tm=256):
    m, d = xs.shape
    tm = _tile(m, tm)
    init = jnp.zeros((n_sorted, d), xs.dtype)
    return pl.pallas_call(
        _dispatch_kernel,
        grid=(m // tm,),
        in_specs=[pl.BlockSpec((2 * tm,), lambda i: (i,), memory_space=pltpu.SMEM),
                  pl.BlockSpec((tm, d), lambda i: (i, 0)),
                  pl.BlockSpec(memory_space=pl.ANY)],
        out_specs=pl.BlockSpec(memory_space=pl.ANY),
        out_shape=jax.ShapeDtypeStruct((n_sorted, d), xs.dtype),
        scratch_shapes=[pltpu.SemaphoreType.DMA(())],
        input_output_aliases={2: 0},
        compiler_params=_params(("arbitrary",)),
        name="moe_dispatch",
    )(dest, xs, init)


def _grouped_ffn_kernel(te_ref, na_ref, x_ref, wg_ref, wu_ref, wd_ref, o_ref, acc_ref):
    del te_ref
    i, f = pl.program_id(0), pl.program_id(1)
    active = i < na_ref[0]

    @pl.when(f == 0)
    def _():
        acc_ref[...] = jnp.zeros_like(acc_ref)

    @pl.when(active)
    def _():
        xs = x_ref[...].astype(BF16)
        g = jnp.dot(xs, wg_ref[...], preferred_element_type=F32)
        u = jnp.dot(xs, wu_ref[...], preferred_element_type=F32)
        acc_ref[...] += jnp.dot((_silu(g) * u).astype(BF16), wd_ref[...], preferred_element_type=F32)

    @pl.when(f == pl.num_programs(1) - 1)
    def _():
        o_ref[...] = acc_ref[...]


def grouped_ffn(xsorted, tile_expert, n_active, wg, wu, wd, tm, tf=512):
    ns, d = xsorted.shape
    dff = wg.shape[2]
    tf = _tile(dff, tf)
    nf = dff // tf
    n_tiles = ns // tm

    def fidx(i, f, na):
        return jnp.where(i < na[0], f, nf - 1)

    return pl.pallas_call(
        _grouped_ffn_kernel,
        grid_spec=pltpu.PrefetchScalarGridSpec(
            num_scalar_prefetch=2,
            grid=(n_tiles, nf),
            in_specs=[pl.BlockSpec((tm, d), lambda i, f, te, na: (i, 0)),
                      pl.BlockSpec((None, d, tf), lambda i, f, te, na: (te[i], 0, fidx(i, f, na))),
                      pl.BlockSpec((None, d, tf), lambda i, f, te, na: (te[i], 0, fidx(i, f, na))),
                      pl.BlockSpec((None, tf, d), lambda i, f, te, na: (te[i], fidx(i, f, na), 0))],
            out_specs=pl.BlockSpec((tm, d), lambda i, f, te, na: (i, 0)),
            scratch_shapes=[pltpu.VMEM((tm, d), F32)]),
        out_shape=jax.ShapeDtypeStruct((ns, d), F32),
        compiler_params=_params(("arbitrary", "arbitrary")),
        name="grouped_ffn",
    )(tile_expert, n_active, xsorted, wg, wu, wd)


def _combine_kernel(dest_ref, ys_ref, h_ref, wts_ref, o_ref, y0_ref, y1_ref, sem):
    tm = h_ref.shape[0]

    def copy(t, k):
        dst = (y0_ref, y1_ref)[k]
        return pltpu.make_async_copy(ys_ref.at[pl.ds(dest_ref[2 * t + k], 1), :],
                                     dst.at[pl.ds(t, 1), :], sem)

    def start(t, c):
        copy(t, 0).start()
        copy(t, 1).start()
        return c

    def wait(t, c):
        copy(t, 0).wait()
        copy(t, 1).wait()
        return c

    lax.fori_loop(0, tm, start, 0)
    lax.fori_loop(0, tm, wait, 0)
    wts = wts_ref[...]
    o_ref[...] = h_ref[...] + wts[:, 0:1] * y0_ref[...] + wts[:, 1:2] * y1_ref[...]


def moe_combine(ysorted, dest, h, wts, tm=256):
    m, d = h.shape
    tm = _tile(m, tm)
    return pl.pallas_call(
        _combine_kernel,
        grid=(m // tm,),
        in_specs=[pl.BlockSpec((2 * tm,), lambda i: (i,), memory_space=pltpu.SMEM),
                  pl.BlockSpec(memory_space=pl.ANY),
                  pl.BlockSpec((tm, d), lambda i: (i, 0)),
                  pl.BlockSpec((tm, LANE), lambda i: (i, 0))],
        out_specs=pl.BlockSpec((tm, d), lambda i: (i, 0)),
        out_shape=jax.ShapeDtypeStruct((m, d), F32),
        scratch_shapes=[pltpu.VMEM((tm, d), F32), pltpu.VMEM((tm, d), F32),
                        pltpu.SemaphoreType.DMA(())],
        compiler_params=_params(("arbitrary",)),
        name="moe_combine",
    )(dest, ysorted, h, wts)


def moe_ffn(h, nw, router, wg, wu, wd, tmg=512):
    m, d = h.shape
    tmg = min(tmg, m)
    xs, idx, wts, cnt = moe_router(h, nw, router)
    counts = cnt[0, :N_EXPERTS].astype(jnp.int32)
    padded = (counts + tmg - 1) // tmg * tmg
    ends = jnp.cumsum(padded)
    offsets = ends - padded
    n_tiles = (2 * m) // tmg + N_EXPERTS
    n_sorted = n_tiles * tmg
    dest = (offsets[idx[:, 0:2]] + idx[:, 2:4]).reshape(-1)
    tile_start = jnp.arange(n_tiles, dtype=jnp.int32) * tmg
    tile_expert = jnp.minimum(jnp.sum(tile_start[:, None] >= ends[None, :], axis=1), N_EXPERTS - 1)
    n_active = (ends[-1] // tmg).reshape(1).astype(jnp.int32)
    xsorted = moe_dispatch(xs, dest, n_sorted)
    ysorted = grouped_ffn(xsorted, tile_expert.astype(jnp.int32), n_active, wg, wu, wd, tmg)
    return moe_combine(ysorted, dest, h, wts)


def _ple_kernel(h_ref, p_ref, nw_ref, wg_ref, wp_ref, fw_ref, o_ref, *, final):
    h = h_ref[...]
    gate = _sigmoid(jnp.dot(_rms(h, nw_ref[...]).astype(BF16), wg_ref[...], preferred_element_type=F32))
    out = h + jnp.dot(p_ref[...].astype(BF16), wp_ref[...], preferred_element_type=F32) * gate
    o_ref[...] = _rms(out, fw_ref[...]) if final else out


def ple(h, p, nw, wg, wp, fw, final, tm=512):
    m, d = h.shape
    pd = p.shape[1]
    tm = _tile(m, tm)
    return pl.pallas_call(
        functools.partial(_ple_kernel, final=final),
        grid=(m // tm,),
        in_specs=[pl.BlockSpec((tm, d), lambda i: (i, 0)),
                  pl.BlockSpec((tm, pd), lambda i: (i, 0)),
                  pl.BlockSpec((1, d), lambda i: (0, 0)),
                  pl.BlockSpec((d, d), lambda i: (0, 0)),
                  pl.BlockSpec((pd, d), lambda i: (0, 0)),
                  pl.BlockSpec((1, d), lambda i: (0, 0))],
        out_specs=pl.BlockSpec((tm, d), lambda i: (i, 0)),
        out_shape=jax.ShapeDtypeStruct((m, d), F32),
        compiler_params=_params(("parallel",)),
        name="ple",
    )(h, p, nw.reshape(1, d), wg, wp, fw.reshape(1, d))


def _split_w_in(w):
    d = w.shape[0]
    big = jnp.concatenate([w[:, :3072], w[:, 3080:7176]], axis=1).astype(BF16)
    small = jnp.concatenate([w[:, 3072:3080], w[:, 7176:7184],
                             jnp.zeros((d, LANE - 16), w.dtype)], axis=1).astype(BF16)
    return big, small


def kernel(x, p, attn_norm, w_in, conv_dn, conv_ml, dn_a_log, dn_dt_bias, dn_norm, diff_lq1, diff_lk1, diff_lq2, diff_lk2, diff_norm, ml_i_bias, ml_f_bias, ml_norm, w_out, ffn_norm, dense_w_gate, dense_w_up, dense_w_down, router, moe_w_gate, moe_w_up, moe_w_down, ple_norm, ple_proj, ple_gate, final_norm):
    batch, seq, d = x.shape
    depth = w_in.shape[0]
    m = batch * seq
    h = x.reshape(m, d)
    dn_w, diff_w = DN_HEADS * HEAD_W, DIFF_HEADS * HEAD_W
    for i in range(depth):
        lambda_init = 0.8 - 0.6 * math.exp(-0.3 * i)
        w_big, w_small = _split_w_in(w_in[i])
        zb, zs = norm_proj(h, attn_norm[i], w_big, w_small)
        o_dn = deltanet(zb, zs, conv_dn[i], dn_a_log[i], dn_dt_bias[i], dn_norm[i], batch, seq)
        o_diff = diff_attention(zb, diff_lq1[i], diff_lk1[i], diff_lq2[i], diff_lk2[i], diff_norm[i],
                                lambda_init, batch, seq)
        o_ml = mlstm(zb, zs, conv_ml[i], ml_i_bias[i], ml_f_bias[i], ml_norm[i], batch, seq)
        wo = w_out[i].astype(BF16)
        h = out_proj(o_dn, o_diff, o_ml, wo[:dn_w], wo[dn_w:dn_w + diff_w], wo[dn_w + diff_w:], h)
        j = i // 2
        if i % 2 == 0:
            h = dense_ffn(h, ffn_norm[i], dense_w_gate[j].astype(BF16), dense_w_up[j].astype(BF16),
                          dense_w_down[j].astype(BF16))
        else:
            h = moe_ffn(h, ffn_norm[i], router[j], moe_w_gate[j].astype(BF16), moe_w_up[j].astype(BF16),
                        moe_w_down[j].astype(BF16))
        h = ple(h, p[i].reshape(m, -1), ple_norm[i], ple_gate[i].astype(BF16), ple_proj[i].astype(BF16),
                final_norm, final=(i == depth - 1))
    return h.reshape(batch, seq, d)
```

```python
import functools
import math

import numpy as np
import jax
import jax.numpy as jnp
from jax import lax
from jax.experimental import pallas as pl
from jax.experimental.pallas import tpu as pltpu

F32 = jnp.float32
BF16 = jnp.bfloat16
EPS = 1e-6
LANE = 128
NEG = -1e30

DN_HEADS = 4
DIFF_HEADS = 8
ML_HEADS = 4
HEAD_W = 128
DIFF_DH = 64
CHUNK = 64
CONV_K = 4
N_EXPERTS = 8
VMEM_LIMIT = 56 * 1024 * 1024

CB_DN_Q, CB_DN_K, CB_DN_V = 0, 4, 8
CB_ML_Q, CB_ML_K = 12, 16
CB_DN_Z = 20
CB_AQ, CB_AK, CB_AV = 24, 32, 40
CB_ML_V, CB_ML_O = 48, 52
N_BIG = 56 * LANE
SL_DN_B, SL_DN_A, SL_ML_I, SL_ML_F = 0, 4, 8, 12


def _params(sem):
    return pltpu.CompilerParams(dimension_semantics=sem, vmem_limit_bytes=VMEM_LIMIT)


def _dot(a, b):
    return jnp.dot(a.astype(BF16), b.astype(BF16), preferred_element_type=F32)


def _dot_nt(a, b):
    return lax.dot_general(a.astype(BF16), b.astype(BF16), (((1,), (1,)), ((), ())),
                           preferred_element_type=F32)


def _dot_tn(a, b):
    return lax.dot_general(a.astype(BF16), b.astype(BF16), (((0,), (0,)), ((), ())),
                           preferred_element_type=F32)


def _dot_hi(a, b):
    return jnp.dot(a, b, preferred_element_type=F32, precision=lax.Precision.HIGHEST)


def _rms(x, w):
    return x * lax.rsqrt(jnp.mean(x * x, axis=-1, keepdims=True) + EPS) * w


def _sigmoid(x):
    return 1.0 / (1.0 + jnp.exp(-x))


def _silu(x):
    return x * _sigmoid(x)


def _softplus(x):
    return jnp.maximum(x, 0.0) + jnp.log(1.0 + jnp.exp(-jnp.abs(x)))


def _log_sigmoid(x):
    return -_softplus(-x)


def _tile(m, t):
    t = min(m, t)
    assert m % t == 0
    return t


def _norm_proj_kernel(x_ref, nw_ref, w_ref, ws_ref, o_ref, os_ref, xs_ref):
    @pl.when(pl.program_id(1) == 0)
    def _():
        xs_ref[...] = _rms(x_ref[...], nw_ref[...]).astype(BF16)
        os_ref[...] = jnp.dot(xs_ref[...], ws_ref[...], preferred_element_type=F32)

    o_ref[...] = jnp.dot(xs_ref[...], w_ref[...], preferred_element_type=F32).astype(o_ref.dtype)


def norm_proj(x, nw, w_big, w_small, tm=1024, tn=1792):
    m, d = x.shape
    n = w_big.shape[1]
    tm, tn = _tile(m, tm), _tile(n, tn)
    return pl.pallas_call(
        _norm_proj_kernel,
        grid=(m // tm, n // tn),
        in_specs=[pl.BlockSpec((tm, d), lambda i, j: (i, 0)),
                  pl.BlockSpec((1, d), lambda i, j: (0, 0)),
                  pl.BlockSpec((d, tn), lambda i, j: (0, j)),
                  pl.BlockSpec((d, LANE), lambda i, j: (0, 0))],
        out_specs=[pl.BlockSpec((tm, tn), lambda i, j: (i, j)),
                   pl.BlockSpec((tm, LANE), lambda i, j: (i, 0))],
        out_shape=[jax.ShapeDtypeStruct((m, n), BF16), jax.ShapeDtypeStruct((m, LANE), F32)],
        scratch_shapes=[pltpu.VMEM((tm, d), BF16)],
        compiler_params=_params(("parallel", "arbitrary")),
        name="norm_proj",
    )(x, nw.reshape(1, d), w_big, w_small)


def _chunk_conv_silu(ref, w, ci, rows):
    cur = ref[rows, :].astype(F32)
    prev_rows = pl.ds(pl.multiple_of(jnp.maximum(ci * CHUNK - 16, 0), 16), 16)
    prev = jnp.where(ci > 0, ref[prev_rows, :].astype(F32)[8:16, :], 0.0)
    ext = jnp.concatenate([prev, cur], axis=0)
    y = cur * w[CONV_K - 1:CONV_K, :]
    for back in range(1, CONV_K):
        y = y + pltpu.roll(ext, back, axis=0)[8:8 + CHUNK, :] * w[CONV_K - 1 - back:CONV_K - back, :]
    return _silu(y)


def _dot2(l_bf16, x):
    hi = x.astype(BF16)
    lo = (x - hi.astype(F32)).astype(BF16)
    return (jnp.dot(l_bf16, hi, preferred_element_type=F32)
            + jnp.dot(l_bf16, lo, preferred_element_type=F32))


def _lane_rows(*rows):
    out = jnp.zeros((8, LANE), F32)
    for r, (lane0, vals) in enumerate(rows):
        out = out.at[r, lane0:lane0 + vals.shape[0]].set(vals.astype(F32))
    return out


def _chunk_masks():
    r = lax.broadcasted_iota(jnp.int32, (CHUNK, CHUNK), 0)
    c = lax.broadcasted_iota(jnp.int32, (CHUNK, CHUNK), 1)
    return r, c


def _chunk_rows(ci):
    return pl.ds(pl.multiple_of(ci * CHUNK, CHUNK), CHUNK)


assert DN_HEADS == ML_HEADS
HEADS = range(DN_HEADS)
COLS = [slice(h * HEAD_W, (h + 1) * HEAD_W) for h in HEADS]
UNROLL = 2


def _deltanet_kernel(gp_ref, q_ref, k_ref, v_ref, zg_ref, zs_ref, cw_ref, nw_ref, o_ref,
                     u_s, w_s, at_s, qg_s, kg_s, gl_s):
    n_chunks = q_ref.shape[0] // CHUNK
    width = DN_HEADS * HEAD_W
    r, c = _chunk_masks()
    low_incl = (r >= c).astype(BF16)
    strict_up = (r > c).astype(F32)
    tri_incl = r >= c
    tri_strict = r > c
    cw = cw_ref[...]
    rate_row = jnp.exp(gp_ref[0:1, :])
    bias_row = gp_ref[1:2, :]

    def prep(cp, carry):
        units, qc, kc, kb, kcb, vbeta, g = [], [], [], [], [], [], []
        for t in range(UNROLL):
            ci = UNROLL * cp + t
            rows = _chunk_rows(ci)
            qa = _chunk_conv_silu(q_ref, cw[:, :width], ci, rows)
            ka = _chunk_conv_silu(k_ref, cw[:, width:2 * width], ci, rows)
            va = _chunk_conv_silu(v_ref, cw[:, 2 * width:], ci, rows)
            zs = zs_ref[rows, :]
            sig = _sigmoid(zs)
            g_all = -rate_row * _softplus(zs + bias_row)
            for h in HEADS:
                units.append((ci, rows, h))
                qh, kh = qa[:, COLS[h]], ka[:, COLS[h]]
                qc.append(qh * (lax.rsqrt(jnp.sum(qh * qh, axis=-1, keepdims=True) + EPS) * (HEAD_W ** -0.5)))
                kc.append(kh * lax.rsqrt(jnp.sum(kh * kh, axis=-1, keepdims=True) + EPS))
                beta = sig[:, SL_DN_B + h:SL_DN_B + h + 1]
                g.append(g_all[:, SL_DN_A + h:SL_DN_A + h + 1])
                kb.append(kc[-1] * beta)
                kcb.append(kc[-1].astype(BF16))
                vbeta.append(va[:, COLS[h]] * beta)
        us = range(len(units))
        dmat = [_dot2(low_incl, jnp.broadcast_to(g[u], (CHUNK, CHUNK)) * strict_up) for u in us]
        kk = [_dot_nt(kb[u], kcb[u]) for u in us]
        qk = [_dot_nt(qc[u], kcb[u]) for u in us]
        gc = [dmat[u][:, :1] + g[u][:1, :] for u in us]
        decay = [jnp.where(tri_incl, jnp.exp(dmat[u]), 0.0) for u in us]
        eg = [jnp.exp(gc[u]) for u in us]
        a = [jnp.where(tri_strict, kk[u] * decay[u], 0.0).astype(BF16) for u in us]
        x = [jnp.concatenate([vbeta[u], kb[u] * eg[u]], axis=1) for u in us]
        ax = [_dot(a[u], x[u]) for u in us]
        p = [_dot(a[u], a[u]).astype(BF16) for u in us]
        x = [x[u] - ax[u] for u in us]
        for step in range(5):
            px = [_dot(p[u], x[u]) for u in us]
            if step < 4:
                p = [_dot(p[u], p[u]).astype(BF16) for u in us]
            x = [x[u] + px[u] for u in us]
        for u, (ci, rows, h) in enumerate(units):
            at_s[rows, h * HEAD_W:h * HEAD_W + CHUNK] = (qk[u] * decay[u]).astype(BF16)
            u_s[rows, COLS[h]] = x[u][:, :HEAD_W]
            w_s[rows, COLS[h]] = x[u][:, HEAD_W:].astype(BF16)
            g_last = gc[u][CHUNK - 1:CHUNK, :]
            qg_s[rows, COLS[h]] = (qc[u] * eg[u]).astype(BF16)
            kg_s[rows, COLS[h]] = (kc[u] * jnp.exp(g_last - gc[u])).astype(BF16)
            gl_s[pl.ds(pl.multiple_of(ci * 8, 8), 8), COLS[h]] = jnp.broadcast_to(jnp.exp(g_last), (8, HEAD_W))
        return carry

    assert n_chunks % UNROLL == 0
    lax.fori_loop(0, n_chunks // UNROLL, prep, 0)

    nw = nw_ref[...]

    def scan(ci, states):
        rows = _chunk_rows(ci)
        sb = [states[h].astype(BF16) for h in HEADS]
        ws = [jnp.dot(w_s[rows, COLS[h]], sb[h], preferred_element_type=F32) for h in HEADS]
        qs = [jnp.dot(qg_s[rows, COLS[h]], sb[h], preferred_element_type=F32) for h in HEADS]
        vb = [(u_s[rows, COLS[h]] - ws[h]).astype(BF16) for h in HEADS]
        av = [jnp.dot(at_s[rows, h * HEAD_W:h * HEAD_W + CHUNK], vb[h], preferred_element_type=F32)
              for h in HEADS]
        kv = [_dot_tn(kg_s[rows, COLS[h]], vb[h]) for h in HEADS]
        out = []
        for h in HEADS:
            gl = gl_s[pl.ds(pl.multiple_of(ci * 8, 8), 1), COLS[h]]
            out.append(states[h] * gl + kv[h])
            o_ref[rows, COLS[h]] = (_rms(qs[h] + av[h], nw)
                                    * _silu(zg_ref[rows, COLS[h]].astype(F32))).astype(o_ref.dtype)
        return tuple(out)

    lax.fori_loop(0, n_chunks, scan, tuple(jnp.zeros((HEAD_W, HEAD_W), F32) for _ in range(DN_HEADS)))


def deltanet(zb, zs, conv_dn, a_log, dt_bias, dn_norm, batch, seq):
    m = zb.shape[0]
    width = DN_HEADS * HEAD_W
    blk = lambda cb: pl.BlockSpec((seq, width), lambda b: (b, cb // DN_HEADS))
    return pl.pallas_call(
        _deltanet_kernel,
        grid=(batch,),
        in_specs=[pl.BlockSpec((8, LANE), lambda b: (0, 0)),
                  blk(CB_DN_Q), blk(CB_DN_K), blk(CB_DN_V), blk(CB_DN_Z),
                  pl.BlockSpec((seq, LANE), lambda b: (b, 0)),
                  pl.BlockSpec((CONV_K, 3 * width), lambda b: (0, 0)),
                  pl.BlockSpec((1, HEAD_W), lambda b: (0, 0))],
        out_specs=pl.BlockSpec((seq, width), lambda b: (b, 0)),
        out_shape=jax.ShapeDtypeStruct((m, width), BF16),
        scratch_shapes=[pltpu.VMEM((seq, width), F32), pltpu.VMEM((seq, width), BF16),
                        pltpu.VMEM((seq, width), BF16), pltpu.VMEM((seq, width), BF16),
                        pltpu.VMEM((seq, width), BF16),
                        pltpu.VMEM((seq // CHUNK * 8, width), F32)],
        compiler_params=_params(("parallel",)),
        name="deltanet",
    )(_lane_rows((SL_DN_A, a_log), (SL_DN_A, dt_bias)), zb, zb, zb, zb, zs, conv_dn,
      dn_norm.reshape(1, HEAD_W))


def _mlstm_kernel(gp_ref, q_ref, k_ref, v_ref, og_ref, zs_ref, cw_ref, nw_ref, o_ref):
    n_chunks = q_ref.shape[0] // CHUNK
    width = ML_HEADS * HEAD_W
    r, c = _chunk_masks()
    low_incl = (r >= c).astype(BF16)
    strict_up = (r > c).astype(F32)
    eye = (r == c).astype(F32)
    tri_incl = r >= c
    nw = nw_ref[...]
    cw = cw_ref[...]

    def local(ci):
        rows = _chunk_rows(ci)
        qa = _chunk_conv_silu(q_ref, cw[:, :width], ci, rows)
        ka = _chunk_conv_silu(k_ref, cw[:, width:], ci, rows) * (HEAD_W ** -0.5)
        zs = zs_ref[rows, :]
        ip_all = zs + gp_ref[0:1, :]
        lf_all = _log_sigmoid(zs + gp_ref[1:2, :])
        qc = [qa[:, COLS[h]] for h in HEADS]
        kc = [ka[:, COLS[h]] for h in HEADS]
        ip = [ip_all[:, SL_ML_I + h:SL_ML_I + h + 1] for h in HEADS]
        lf = [lf_all[:, SL_ML_F + h:SL_ML_F + h + 1] for h in HEADS]
        dl = [_dot2(low_incl, jnp.broadcast_to(lf[h], (CHUNK, CHUNK)) * strict_up
                    + jnp.broadcast_to(ip[h], (CHUNK, CHUNK)) * eye) for h in HEADS]
        qk = [_dot_nt(qc[h], kc[h]) for h in HEADS]
        heads = []
        for h in HEADS:
            b = dl[h][:, :1] - ip[h][:1, :] + lf[h][:1, :]
            dlog = jnp.where(tri_incl, dl[h], NEG)
            b_last = b[CHUNK - 1:CHUNK, :]
            a = b_last - b + ip[h]
            heads.append(dict(qc=qc[h], kc=kc[h], qk=qk[h], b=b, dlog=dlog, b_last=b_last, a=a,
                              dmax=jnp.max(dlog, axis=-1, keepdims=True),
                              a_max=jnp.max(a, axis=0, keepdims=True)))
        return rows, heads

    def advance(rows, heads, carry):
        vc = [v_ref[rows, COLS[h]] for h in HEADS]
        s, wk, inter, m_t, m_new, scale = [], [], [], [], [], []
        for h in HEADS:
            t = heads[h]
            m_st = carry[h][2]
            m_t.append(jnp.maximum(t["dmax"], t["b"] + m_st))
            s.append(t["qk"] * jnp.exp(t["dlog"] - m_t[h]))
            inter.append(jnp.exp(t["b"] + m_st - m_t[h]))
            m_new.append(jnp.maximum(t["b_last"] + m_st, t["a_max"]))
            scale.append(jnp.exp(t["b_last"] + m_st - m_new[h]))
            wk.append(t["kc"] * jnp.exp(t["a"] - m_new[h]))
        sv = [_dot(s[h], vc[h]) for h in HEADS]
        kv = [_dot_tn(wk[h], vc[h]) for h in HEADS]
        qcs = [_dot(heads[h]["qc"], carry[h][0]) for h in HEADS]
        out = []
        for h in HEADS:
            c_st, n_st, _ = carry[h]
            qc = heads[h]["qc"]
            num = sv[h] + inter[h] * qcs[h]
            den = (jnp.sum(s[h], axis=-1, keepdims=True)
                   + inter[h] * jnp.sum(qc * n_st, axis=-1, keepdims=True))
            hh = num / jnp.maximum(jnp.abs(den), jnp.exp(-m_t[h]))
            out.append((scale[h] * c_st + kv[h],
                        scale[h] * n_st + jnp.sum(wk[h], axis=0, keepdims=True),
                        m_new[h]))
            o_ref[rows, COLS[h]] = (_sigmoid(og_ref[rows, COLS[h]].astype(F32))
                                    * _rms(hh, nw)).astype(o_ref.dtype)
        return tuple(out)

    init = tuple((jnp.zeros((HEAD_W, HEAD_W), F32), jnp.zeros((1, HEAD_W), F32), jnp.zeros((1, 1), F32))
                 for _ in range(ML_HEADS))
    lax.fori_loop(0, n_chunks, lambda ci, carry: advance(*local(ci), carry), init)


def mlstm(zb, zs, conv_ml, i_bias, f_bias, ml_norm, batch, seq):
    m = zb.shape[0]
    width = ML_HEADS * HEAD_W
    blk = lambda cb: pl.BlockSpec((seq, width), lambda b: (b, cb // ML_HEADS))
    return pl.pallas_call(
        _mlstm_kernel,
        grid=(batch,),
        in_specs=[pl.BlockSpec((8, LANE), lambda b: (0, 0)),
                  blk(CB_ML_Q), blk(CB_ML_K), blk(CB_ML_V), blk(CB_ML_O),
                  pl.BlockSpec((seq, LANE), lambda b: (b, 0)),
                  pl.BlockSpec((CONV_K, 2 * width), lambda b: (0, 0)),
                  pl.BlockSpec((1, HEAD_W), lambda b: (0, 0))],
        out_specs=pl.BlockSpec((seq, width), lambda b: (b, 0)),
        out_shape=jax.ShapeDtypeStruct((m, width), BF16),
        compiler_params=_params(("parallel",)),
        name="mlstm",
    )(_lane_rows((SL_ML_I, i_bias), (SL_ML_F, f_bias)), zb, zb, zb, zb, zs, conv_ml,
      ml_norm.reshape(1, HEAD_W))


LOG2E = math.log2(math.e)
LOG2E_HI = float(np.asarray(LOG2E, dtype=BF16))
LOG2E_LO = LOG2E - LOG2E_HI
SLAB = 128
RAMP = 256


def _with_lanes(x, base, vals):
    lane = lax.broadcasted_iota(jnp.int32, x.shape, 1)
    for i, v in enumerate(vals):
        x = jnp.where(lane == base + i, v, x)
    return x


def _diff_attn_kernel(lq1_ref, lk1_ref, lq2_ref, lk2_ref, q_ref, k_ref, v_ref, nw_ref, o_ref,
                      k0_s, k1_s, *, tq, lambda_init):
    h = pl.program_id(1)
    qi = pl.program_id(2)
    tk = tq
    assert k_ref.shape[0] <= RAMP * RAMP
    slope = jnp.exp2(jnp.full((1, 1), -(h + 1).astype(F32) * (8.0 / DIFF_HEADS), F32))

    @pl.when(qi == 0)
    def _():
        kk = k_ref[...]
        pos = lax.broadcasted_iota(jnp.int32, kk.shape, 0)
        fine = ((pos & (RAMP - 1)).astype(F32) * slope).astype(BF16)
        coarse = ((pos - (pos & (RAMP - 1))).astype(F32) * slope).astype(BF16)
        k0_s[...] = _with_lanes(kk, DIFF_DH, (fine, fine, coarse, coarse))
        k1_s[...] = _with_lanes(kk, 0, (fine, fine, coarse, coarse))

    lam = (jnp.exp(jnp.sum(lq1_ref[...] * lk1_ref[...], axis=-1, keepdims=True))
           - jnp.exp(jnp.sum(lq2_ref[...] * lk2_ref[...], axis=-1, keepdims=True)) + lambda_init)
    lane = lax.broadcasted_iota(jnp.int32, (tq, HEAD_W), 1)
    q = q_ref[...].astype(F32) * (DIFF_DH ** -0.5 * LOG2E)
    l2e = (LOG2E_HI, LOG2E_LO, LOG2E_HI, LOG2E_LO)
    q0 = _with_lanes(jnp.where(lane < DIFF_DH, q, 0.0), DIFF_DH, l2e).astype(BF16)
    q1 = _with_lanes(jnp.where(lane >= DIFF_DH, q, 0.0), 0, l2e).astype(BF16)
    per_map = tq // SLAB
    n_slab = 2 * per_map
    q_slab = [qm[j * SLAB:(j + 1) * SLAB, :] for qm in (q0, q1) for j in range(per_map)]

    def run_blocks(kjs, masked, carry):
        scores, values = [], []
        for kj in kjs:
            rows = pl.ds(pl.multiple_of(kj * tk, tk), tk)
            kb = (k0_s[rows, :], k1_s[rows, :])
            values.append(v_ref[rows, :])
            scores.append([_dot_nt(q_slab[i], kb[i // per_map]) for i in range(n_slab)])
        for blk in range(len(kjs)):
            out = []
            for i in range(n_slab):
                m_i, l_i, acc = carry[i]
                s = scores[blk][i]
                if masked[blk]:
                    ri = lax.broadcasted_iota(jnp.int32, (SLAB, tk), 0) + (i % per_map) * SLAB
                    ci = lax.broadcasted_iota(jnp.int32, (SLAB, tk), 1)
                    s = jnp.where(ci <= ri, s, NEG)
                m_new = jnp.maximum(m_i, jnp.max(s, axis=-1, keepdims=True))
                p = jnp.exp2(s - m_new)
                alpha = jnp.exp2(m_i - m_new)
                l_new = alpha * l_i + sum(p[:, j * LANE:(j + 1) * LANE] for j in range(tk // LANE))
                acc_new = alpha * acc + jnp.dot(p.astype(BF16), values[blk], preferred_element_type=F32)
                out.append((m_new, l_new, acc_new))
            carry = tuple(out)
        return carry

    init = tuple((jnp.full((SLAB, 1), NEG, F32), jnp.zeros((SLAB, LANE), F32), jnp.zeros((SLAB, HEAD_W), F32))
                 for _ in range(n_slab))
    carry = lax.fori_loop(0, qi // 2,
                          lambda kp, cr: run_blocks((2 * kp, 2 * kp + 1), (False, False), cr), init)
    carry = lax.cond(qi % 2 == 1,
                     functools.partial(run_blocks, (qi - 1, qi), (False, True)),
                     functools.partial(run_blocks, (qi,), (True,)), carry)
    nw = nw_ref[...]
    for j in range(per_map):
        (_, l0, a0), (_, l1, a1) = carry[j], carry[per_map + j]
        l0 = jnp.sum(l0, axis=-1, keepdims=True)
        l1 = jnp.sum(l1, axis=-1, keepdims=True)
        o = a0 / l0 - lam * (a1 / l1)
        o_ref[j * SLAB:(j + 1) * SLAB, :] = (_rms(o, nw) * (1.0 - lambda_init)).astype(o_ref.dtype)


def diff_attention(zb, lq1, lk1, lq2, lk2, diff_norm, lambda_init, batch, seq, tq=256):
    m = zb.shape[0]
    tq = _tile(seq, tq)
    nq = seq // tq
    vec = pl.BlockSpec((1, DIFF_DH), lambda b, h, i: (0, 0))
    return pl.pallas_call(
        functools.partial(_diff_attn_kernel, tq=tq, lambda_init=lambda_init),
        grid=(batch, DIFF_HEADS, nq),
        in_specs=[vec, vec, vec, vec,
                  pl.BlockSpec((tq, HEAD_W), lambda b, h, i: (b * nq + i, CB_AQ + h)),
                  pl.BlockSpec((seq, HEAD_W), lambda b, h, i: (b, CB_AK + h)),
                  pl.BlockSpec((seq, HEAD_W), lambda b, h, i: (b, CB_AV + h)),
                  pl.BlockSpec((1, HEAD_W), lambda b, h, i: (0, 0))],
        out_specs=pl.BlockSpec((tq, HEAD_W), lambda b, h, i: (b * nq + i, h)),
        out_shape=jax.ShapeDtypeStruct((m, DIFF_HEADS * HEAD_W), BF16),
        scratch_shapes=[pltpu.VMEM((seq, HEAD_W), BF16), pltpu.VMEM((seq, HEAD_W), BF16)],
        compiler_params=_params(("parallel", "parallel", "arbitrary")),
        name="diff_attn",
    )(lq1.reshape(1, -1), lk1.reshape(1, -1), lq2.reshape(1, -1), lk2.reshape(1, -1),
      zb, zb, zb, diff_norm.reshape(1, HEAD_W))


def _out_proj_kernel(a_ref, b_ref, c_ref, wa_ref, wb_ref, wc_ref, h_ref, o_ref):
    acc = jnp.dot(a_ref[...], wa_ref[...], preferred_element_type=F32)
    acc += jnp.dot(b_ref[...], wb_ref[...], preferred_element_type=F32)
    acc += jnp.dot(c_ref[...], wc_ref[...], preferred_element_type=F32)
    o_ref[...] = h_ref[...] + acc


def out_proj(o_dn, o_diff, o_ml, wa, wb, wc, h, tm=512):
    m, d = h.shape
    tm = _tile(m, tm)
    row = lambda w: pl.BlockSpec((tm, w), lambda i: (i, 0))
    full = lambda w: pl.BlockSpec((w, d), lambda i: (0, 0))
    return pl.pallas_call(
        _out_proj_kernel,
        grid=(m // tm,),
        in_specs=[row(o_dn.shape[1]), row(o_diff.shape[1]), row(o_ml.shape[1]),
                  full(wa.shape[0]), full(wb.shape[0]), full(wc.shape[0]), row(d)],
        out_specs=row(d),
        out_shape=jax.ShapeDtypeStruct((m, d), F32),
        compiler_params=_params(("parallel",)),
        name="out_proj",
    )(o_dn, o_diff, o_ml, wa, wb, wc, h)


def _dense_ffn_kernel(h_ref, nw_ref, wg_ref, wu_ref, wd_ref, o_ref, xs_ref, acc_ref):
    f = pl.program_id(1)

    @pl.when(f == 0)
    def _():
        xs_ref[...] = _rms(h_ref[...], nw_ref[...]).astype(BF16)
        acc_ref[...] = jnp.zeros_like(acc_ref)

    xs = xs_ref[...]
    g = jnp.dot(xs, wg_ref[...], preferred_element_type=F32)
    u = jnp.dot(xs, wu_ref[...], preferred_element_type=F32)
    acc_ref[...] += jnp.dot((_silu(g) * u).astype(BF16), wd_ref[...], preferred_element_type=F32)

    @pl.when(f == pl.num_programs(1) - 1)
    def _():
        o_ref[...] = h_ref[...] + acc_ref[...]


def dense_ffn(h, nw, wg, wu, wd, tm=512, tf=512):
    m, d = h.shape
    dff = wg.shape[1]
    tm, tf = _tile(m, tm), _tile(dff, tf)
    return pl.pallas_call(
        _dense_ffn_kernel,
        grid=(m // tm, dff // tf),
        in_specs=[pl.BlockSpec((tm, d), lambda i, f: (i, 0)),
                  pl.BlockSpec((1, d), lambda i, f: (0, 0)),
                  pl.BlockSpec((d, tf), lambda i, f: (0, f)),
                  pl.BlockSpec((d, tf), lambda i, f: (0, f)),
                  pl.BlockSpec((tf, d), lambda i, f: (f, 0))],
        out_specs=pl.BlockSpec((tm, d), lambda i, f: (i, 0)),
        out_shape=jax.ShapeDtypeStruct((m, d), F32),
        scratch_shapes=[pltpu.VMEM((tm, d), BF16), pltpu.VMEM((tm, d), F32)],
        compiler_params=_params(("parallel", "arbitrary")),
        name="dense_ffn",
    )(h, nw.reshape(1, d), wg, wu, wd)


def _router_kernel(h_ref, nw_ref, wr_ref, xs_ref, idx_ref, wts_ref, cnt_ref, carry_ref):
    i = pl.program_id(0)
    tm = h_ref.shape[0]

    @pl.when(i == 0)
    def _():
        carry_ref[...] = jnp.zeros_like(carry_ref)

    xn = _rms(h_ref[...], nw_ref[...])
    xs_ref[...] = xn
    logits = _dot_hi(xn, wr_ref[...])
    lane = lax.broadcasted_iota(jnp.int32, logits.shape, 1)
    logits = jnp.where(lane < N_EXPERTS, logits, NEG)
    m1 = jnp.max(logits, axis=-1, keepdims=True)
    e1 = jnp.min(jnp.where(logits == m1, lane, LANE), axis=-1, keepdims=True)
    rest = jnp.where(lane == e1, NEG, logits)
    m2 = jnp.max(rest, axis=-1, keepdims=True)
    e2 = jnp.min(jnp.where(rest == m2, lane, LANE), axis=-1, keepdims=True)
    ex = jnp.exp(m2 - m1)
    w1 = 1.0 / (1.0 + ex)
    w2 = ex * w1
    oh1 = (lane == e1).astype(F32)
    oh2 = (lane == e2).astype(F32)
    oh = oh1 + oh2
    r = lax.broadcasted_iota(jnp.int32, (tm, tm), 0)
    c = lax.broadcasted_iota(jnp.int32, (tm, tm), 1)
    before = jnp.dot((r > c).astype(BF16), oh.astype(BF16), preferred_element_type=F32) + carry_ref[...]
    rank1 = jnp.sum(before * oh1, axis=-1, keepdims=True)
    rank2 = jnp.sum(before * oh2, axis=-1, keepdims=True)
    carry_ref[...] += jnp.sum(oh, axis=0, keepdims=True)
    cnt_ref[...] = jnp.broadcast_to(carry_ref[...], cnt_ref.shape)
    idx = jnp.where(lane == 0, e1, jnp.where(lane == 1, e2, 0))
    idx = jnp.where(lane == 2, rank1.astype(jnp.int32), jnp.where(lane == 3, rank2.astype(jnp.int32), idx))
    idx_ref[...] = idx
    wts_ref[...] = jnp.where(lane == 0, w1, jnp.where(lane == 1, w2, 0.0))


def moe_router(h, nw, router, tm=512):
    m, d = h.shape
    tm = _tile(m, tm)
    wr = jnp.zeros((d, LANE), F32).at[:, :N_EXPERTS].set(router)
    return pl.pallas_call(
        _router_kernel,
        grid=(m // tm,),
        in_specs=[pl.BlockSpec((tm, d), lambda i: (i, 0)),
                  pl.BlockSpec((1, d), lambda i: (0, 0)),
                  pl.BlockSpec((d, LANE), lambda i: (0, 0))],
        out_specs=[pl.BlockSpec((tm, d), lambda i: (i, 0)),
                   pl.BlockSpec((tm, LANE), lambda i: (i, 0)),
                   pl.BlockSpec((tm, LANE), lambda i: (i, 0)),
                   pl.BlockSpec((8, LANE), lambda i: (0, 0))],
        out_shape=[jax.ShapeDtypeStruct((m, d), F32),
                   jax.ShapeDtypeStruct((m, LANE), jnp.int32),
                   jax.ShapeDtypeStruct((m, LANE), F32),
                   jax.ShapeDtypeStruct((8, LANE), F32)],
        scratch_shapes=[pltpu.VMEM((1, LANE), F32)],
        compiler_params=_params(("arbitrary",)),
        name="moe_router",
    )(h, nw.reshape(1, d), wr)


def _dispatch_kernel(dest_ref, xs_ref, init_ref, out_ref, sem):
    del init_ref
    tm = xs_ref.shape[0]

    def copy(t, k):
        return pltpu.make_async_copy(xs_ref.at[pl.ds(t, 1), :],
                                     out_ref.at[pl.ds(dest_ref[2 * t + k], 1), :], sem)

    def start(t, c):
        copy(t, 0).start()
        copy(t, 1).start()
        return c

    def wait(t, c):
        copy(t, 0).wait()
        copy(t, 1).wait()
        return c

    lax.fori_loop(0, tm, start, 0)
    lax.fori_loop(0, tm, wait, 0)


def moe_dispatch(xs, dest, n_sorted, tm=256):
    m, d = xs.shape
    tm = _tile(m, tm)
    init = jnp.zeros((n_sorted, d), xs.dtype)
    return pl.pallas_call(
        _dispatch_kernel,
        grid=(m // tm,),
        in_specs=[pl.BlockSpec((2 * tm,), lambda i: (i,), memory_space=pltpu.SMEM),
                  pl.BlockSpec((tm, d), lambda i: (i, 0)),
                  pl.BlockSpec(memory_space=pl.ANY)],
        out_specs=pl.BlockSpec(memory_space=pl.ANY),
        out_shape=jax.ShapeDtypeStruct((n_sorted, d), xs.dtype),
        scratch_shapes=[pltpu.SemaphoreType.DMA(())],
        input_output_aliases={2: 0},
        compiler_params=_params(("arbitrary",)),
        name="moe_dispatch",
    )(dest, xs, init)


def _grouped_ffn_kernel(te_ref, na_ref, x_ref, wg_ref, wu_ref, wd_ref, o_ref, acc_ref):
    del te_ref
    i, f = pl.program_id(0), pl.program_id(1)
    active = i < na_ref[0]

    @pl.when(f == 0)
    def _():
        acc_ref[...] = jnp.zeros_like(acc_ref)

    @pl.when(active)
    def _():
        xs = x_ref[...].astype(BF16)
        g = jnp.dot(xs, wg_ref[...], preferred_element_type=F32)
        u = jnp.dot(xs, wu_ref[...], preferred_element_type=F32)
        acc_ref[...] += jnp.dot((_silu(g) * u).astype(BF16), wd_ref[...], preferred_element_type=F32)

    @pl.when(f == pl.num_programs(1) - 1)
    def _():
        o_ref[...] = acc_ref[...]


def grouped_ffn(xsorted, tile_expert, n_active, wg, wu, wd, tm, tf=512):
    ns, d = xsorted.shape
    dff = wg.shape[2]
    tf = _tile(dff, tf)
    nf = dff // tf
    n_tiles = ns // tm

    def fidx(i, f, na):
        return jnp.where(i < na[0], f, nf - 1)

    return pl.pallas_call(
        _grouped_ffn_kernel,
        grid_spec=pltpu.PrefetchScalarGridSpec(
            num_scalar_prefetch=2,
            grid=(n_tiles, nf),
            in_specs=[pl.BlockSpec((tm, d), lambda i, f, te, na: (i, 0)),
                      pl.BlockSpec((None, d, tf), lambda i, f, te, na: (te[i], 0, fidx(i, f, na))),
                      pl.BlockSpec((None, d, tf), lambda i, f, te, na: (te[i], 0, fidx(i, f, na))),
                      pl.BlockSpec((None, tf, d), lambda i, f, te, na: (te[i], fidx(i, f, na), 0))],
            out_specs=pl.BlockSpec((tm, d), lambda i, f, te, na: (i, 0)),
            scratch_shapes=[pltpu.VMEM((tm, d), F32)]),
        out_shape=jax.ShapeDtypeStruct((ns, d), F32),
        compiler_params=_params(("arbitrary", "arbitrary")),
        name="grouped_ffn",
    )(tile_expert, n_active, xsorted, wg, wu, wd)


def _combine_kernel(dest_ref, ys_ref, h_ref, wts_ref, o_ref, y0_ref, y1_ref, sem):
    tm = h_ref.shape[0]

    def copy(t, k):
        dst = (y0_ref, y1_ref)[k]
        return pltpu.make_async_copy(ys_ref.at[pl.ds(dest_ref[2 * t + k], 1), :],
                                     dst.at[pl.ds(t, 1), :], sem)

    def start(t, c):
        copy(t, 0).start()
        copy(t, 1).start()
        return c

    def wait(t, c):
        copy(t, 0).wait()
        copy(t, 1).wait()
        return c

    lax.fori_loop(0, tm, start, 0)
    lax.fori_loop(0, tm, wait, 0)
    wts = wts_ref[...]
    o_ref[...] = h_ref[...] + wts[:, 0:1] * y0_ref[...] + wts[:, 1:2] * y1_ref[...]


def moe_combine(ysorted, dest, h, wts, tm=256):
    m, d = h.shape
    tm = _tile(m, tm)
    return pl.pallas_call(
        _combine_kernel,
        grid=(m // tm,),
        in_specs=[pl.BlockSpec((2 * tm,), lambda i: (i,), memory_space=pltpu.SMEM),
                  pl.BlockSpec(memory_space=pl.ANY),
                  pl.BlockSpec((tm, d), lambda i: (i, 0)),
                  pl.BlockSpec((tm, LANE), lambda i: (i, 0))],
        out_specs=pl.BlockSpec((tm, d), lambda i: (i, 0)),
        out_shape=jax.ShapeDtypeStruct((m, d), F32),
        scratch_shapes=[pltpu.VMEM((tm, d), F32), pltpu.VMEM((tm, d), F32),
                        pltpu.SemaphoreType.DMA(())],
        compiler_params=_params(("arbitrary",)),
        name="moe_combine",
    )(dest, ysorted, h, wts)


def moe_ffn(h, nw, router, wg, wu, wd, tmg=512):
    m, d = h.shape
    tmg = min(tmg, m)
    xs, idx, wts, cnt = moe_router(h, nw, router)
    counts = cnt[0, :N_EXPERTS].astype(jnp.int32)
    padded = (counts + tmg - 1) // tmg * tmg
    ends = jnp.cumsum(padded)
    offsets = ends - padded
    n_tiles = (2 * m) // tmg + N_EXPERTS
    n_sorted = n_tiles * tmg
    dest = (offsets[idx[:, 0:2]] + idx[:, 2:4]).reshape(-1)
    tile_start = jnp.arange(n_tiles, dtype=jnp.int32) * tmg
    tile_expert = jnp.minimum(jnp.sum(tile_start[:, None] >= ends[None, :], axis=1), N_EXPERTS - 1)
    n_active = (ends[-1] // tmg).reshape(1).astype(jnp.int32)
    xsorted = moe_dispatch(xs, dest, n_sorted)
    ysorted = grouped_ffn(xsorted, tile_expert.astype(jnp.int32), n_active, wg, wu, wd, tmg)
    return moe_combine(ysorted, dest, h, wts)


def _ple_kernel(h_ref, p_ref, nw_ref, wg_ref, wp_ref, fw_ref, o_ref, *, final):
    h = h_ref[...]
    gate = _sigmoid(jnp.dot(_rms(h, nw_ref[...]).astype(BF16), wg_ref[...], preferred_element_type=F32))
    out = h + jnp.dot(p_ref[...].astype(BF16), wp_ref[...], preferred_element_type=F32) * gate
    o_ref[...] = _rms(out, fw_ref[...]) if final else out


def ple(h, p, nw, wg, wp, fw, final, tm=512):
    m, d = h.shape
    pd = p.shape[1]
    tm = _tile(m, tm)
    return pl.pallas_call(
        functools.partial(_ple_kernel, final=final),
        grid=(m // tm,),
        in_specs=[pl.BlockSpec((tm, d), lambda i: (i, 0)),
                  pl.BlockSpec((tm, pd), lambda i: (i, 0)),
                  pl.BlockSpec((1, d), lambda i: (0, 0)),
                  pl.BlockSpec((d, d), lambda i: (0, 0)),
                  pl.BlockSpec((pd, d), lambda i: (0, 0)),
                  pl.BlockSpec((1, d), lambda i: (0, 0))],
        out_specs=pl.BlockSpec((tm, d), lambda i: (i, 0)),
        out_shape=jax.ShapeDtypeStruct((m, d), F32),
        compiler_params=_params(("parallel",)),
        name="ple",
    )(h, p, nw.reshape(1, d), wg, wp, fw.reshape(1, d))


def _split_w_in(w):
    d = w.shape[0]
    dn_w, diff_w, ml_w = DN_HEADS * HEAD_W, DIFF_HEADS * HEAD_W, ML_HEADS * HEAD_W
    g0 = 3 * dn_w + 2 * ml_w + dn_w
    g1 = g0 + 2 * DN_HEADS
    g2 = g1 + 3 * diff_w + 2 * ml_w
    g3 = g2 + 2 * ML_HEADS
    assert w.shape[1] == g3 and g0 + g2 - g1 == N_BIG
    big = jnp.concatenate([w[:, :g0], w[:, g1:g2]], axis=1).astype(BF16)
    small = jnp.concatenate([w[:, g0:g1], w[:, g2:g3],
                             jnp.zeros((d, LANE - (g1 - g0) - (g3 - g2)), w.dtype)], axis=1).astype(BF16)
    return big, small


def kernel(x, p, attn_norm, w_in, conv_dn, conv_ml, dn_a_log, dn_dt_bias, dn_norm, diff_lq1, diff_lk1, diff_lq2, diff_lk2, diff_norm, ml_i_bias, ml_f_bias, ml_norm, w_out, ffn_norm, dense_w_gate, dense_w_up, dense_w_down, router, moe_w_gate, moe_w_up, moe_w_down, ple_norm, ple_proj, ple_gate, final_norm):
    batch, seq, d = x.shape
    depth = w_in.shape[0]
    m = batch * seq
    h = x.reshape(m, d)
    dn_w, diff_w = DN_HEADS * HEAD_W, DIFF_HEADS * HEAD_W
    for i in range(depth):
        lambda_init = 0.8 - 0.6 * math.exp(-0.3 * i)
        w_big, w_small = _split_w_in(w_in[i])
        zb, zs = norm_proj(h, attn_norm[i], w_big, w_small)
        o_dn = deltanet(zb, zs, conv_dn[i], dn_a_log[i], dn_dt_bias[i], dn_norm[i], batch, seq)
        o_diff = diff_attention(zb, diff_lq1[i], diff_lk1[i], diff_lq2[i], diff_lk2[i], diff_norm[i],
                                lambda_init, batch, seq)
        o_ml = mlstm(zb, zs, conv_ml[i], ml_i_bias[i], ml_f_bias[i], ml_norm[i], batch, seq)
        wo = w_out[i].astype(BF16)
        h = out_proj(o_dn, o_diff, o_ml, wo[:dn_w], wo[dn_w:dn_w + diff_w], wo[dn_w + diff_w:], h)
        j = i // 2
        if i % 2 == 0:
            h = dense_ffn(h, ffn_norm[i], dense_w_gate[j].astype(BF16), dense_w_up[j].astype(BF16),
                          dense_w_down[j].astype(BF16))
        else:
            h = moe_ffn(h, ffn_norm[i], router[j], moe_w_gate[j].astype(BF16), moe_w_up[j].astype(BF16),
                        moe_w_down[j].astype(BF16))
        h = ple(h, p[i].reshape(m, -1), ple_norm[i], ple_gate[i].astype(BF16), ple_proj[i].astype(BF16),
                final_norm, final=(i == depth - 1))
    return h.reshape(batch, seq, d)
```

```python
import functools
import math

import numpy as np
import jax
import jax.numpy as jnp
from jax import lax
from jax.experimental import pallas as pl
from jax.experimental.pallas import tpu as pltpu

F32 = jnp.float32
BF16 = jnp.bfloat16
EPS = 1e-6
LANE = 128
NEG = -1e30

DN_HEADS = 4
DIFF_HEADS = 8
ML_HEADS = 4
HEAD_W = 128
DIFF_DH = 64
CHUNK = 64
CONV_K = 4
N_EXPERTS = 8
VMEM_LIMIT = 56 * 1024 * 1024

CB_DN_Q, CB_DN_K, CB_DN_V = 0, 4, 8
CB_ML_Q, CB_ML_K = 12, 16
CB_DN_Z = 20
CB_AQ, CB_AK, CB_AV = 24, 32, 40
CB_ML_V, CB_ML_O = 48, 52
N_BIG = 56 * LANE
SL_DN_B, SL_DN_A, SL_ML_I, SL_ML_F = 0, 4, 8, 12


def _params(sem):
    return pltpu.CompilerParams(dimension_semantics=sem, vmem_limit_bytes=VMEM_LIMIT)


def _dot(a, b):
    return jnp.dot(a.astype(BF16), b.astype(BF16), preferred_element_type=F32)


def _dot_nt(a, b):
    return lax.dot_general(a.astype(BF16), b.astype(BF16), (((1,), (1,)), ((), ())),
                           preferred_element_type=F32)


def _dot_tn(a, b):
    return lax.dot_general(a.astype(BF16), b.astype(BF16), (((0,), (0,)), ((), ())),
                           preferred_element_type=F32)


def _dot_hi(a, b):
    return jnp.dot(a, b, preferred_element_type=F32, precision=lax.Precision.HIGHEST)


def _rms(x, w):
    return x * lax.rsqrt(jnp.mean(x * x, axis=-1, keepdims=True) + EPS) * w


def _sigmoid(x):
    return 1.0 / (1.0 + jnp.exp(-x))


def _silu(x):
    return x * _sigmoid(x)


def _softplus(x):
    return jnp.maximum(x, 0.0) + jnp.log(1.0 + jnp.exp(-jnp.abs(x)))


def _log_sigmoid(x):
    return -_softplus(-x)


def _tile(m, t):
    t = min(m, t)
    assert m % t == 0
    return t


def _norm_proj_kernel(x_ref, nw_ref, w_ref, ws_ref, o_ref, os_ref, xs_ref):
    @pl.when(pl.program_id(1) == 0)
    def _():
        xs_ref[...] = _rms(x_ref[...], nw_ref[...]).astype(BF16)
        os_ref[...] = jnp.dot(xs_ref[...], ws_ref[...], preferred_element_type=F32)

    o_ref[...] = jnp.dot(xs_ref[...], w_ref[...], preferred_element_type=F32).astype(o_ref.dtype)


def norm_proj(x, nw, w_big, w_small, tm=1024, tn=1792):
    m, d = x.shape
    n = w_big.shape[1]
    tm, tn = _tile(m, tm), _tile(n, tn)
    return pl.pallas_call(
        _norm_proj_kernel,
        grid=(m // tm, n // tn),
        in_specs=[pl.BlockSpec((tm, d), lambda i, j: (i, 0)),
                  pl.BlockSpec((1, d), lambda i, j: (0, 0)),
                  pl.BlockSpec((d, tn), lambda i, j: (0, j)),
                  pl.BlockSpec((d, LANE), lambda i, j: (0, 0))],
        out_specs=[pl.BlockSpec((tm, tn), lambda i, j: (i, j)),
                   pl.BlockSpec((tm, LANE), lambda i, j: (i, 0))],
        out_shape=[jax.ShapeDtypeStruct((m, n), BF16), jax.ShapeDtypeStruct((m, LANE), F32)],
        scratch_shapes=[pltpu.VMEM((tm, d), BF16)],
        compiler_params=_params(("parallel", "arbitrary")),
        name="norm_proj",
    )(x, nw.reshape(1, d), w_big, w_small)


def _chunk_conv_silu(ref, w, ci, rows):
    cur = ref[rows, :].astype(F32)
    prev_rows = pl.ds(pl.multiple_of(jnp.maximum(ci * CHUNK - 16, 0), 16), 16)
    prev = jnp.where(ci > 0, ref[prev_rows, :].astype(F32)[8:16, :], 0.0)
    ext = jnp.concatenate([prev, cur], axis=0)
    y = cur * w[CONV_K - 1:CONV_K, :]
    for back in range(1, CONV_K):
        y = y + pltpu.roll(ext, back, axis=0)[8:8 + CHUNK, :] * w[CONV_K - 1 - back:CONV_K - back, :]
    return _silu(y)


def _dot2(l_bf16, x):
    hi = x.astype(BF16)
    lo = (x - hi.astype(F32)).astype(BF16)
    return (jnp.dot(l_bf16, hi, preferred_element_type=F32)
            + jnp.dot(l_bf16, lo, preferred_element_type=F32))


def _lane_rows(*rows):
    out = jnp.zeros((8, LANE), F32)
    for r, (lane0, vals) in enumerate(rows):
        out = out.at[r, lane0:lane0 + vals.shape[0]].set(vals.astype(F32))
    return out


def _chunk_masks():
    r = lax.broadcasted_iota(jnp.int32, (CHUNK, CHUNK), 0)
    c = lax.broadcasted_iota(jnp.int32, (CHUNK, CHUNK), 1)
    return r, c


def _chunk_rows(ci):
    return pl.ds(pl.multiple_of(ci * CHUNK, CHUNK), CHUNK)


assert DN_HEADS == ML_HEADS
HEADS = range(DN_HEADS)
COLS = [slice(h * HEAD_W, (h + 1) * HEAD_W) for h in HEADS]
UNROLL = 2


def _deltanet_kernel(gp_ref, q_ref, k_ref, v_ref, zg_ref, zs_ref, cw_ref, nw_ref, o_ref,
                     u_s, w_s, at_s, qg_s, kg_s, gl_s):
    n_chunks = q_ref.shape[0] // CHUNK
    width = DN_HEADS * HEAD_W
    r, c = _chunk_masks()
    low_incl = (r >= c).astype(BF16)
    strict_up = (r > c).astype(F32)
    tri_incl = r >= c
    tri_strict = r > c
    cw = cw_ref[...]
    rate_row = jnp.exp(gp_ref[0:1, :])
    bias_row = gp_ref[1:2, :]

    def prep(cp, carry):
        units, qc, kc, kb, kcb, vbeta, g = [], [], [], [], [], [], []
        for t in range(UNROLL):
            ci = UNROLL * cp + t
            rows = _chunk_rows(ci)
            qa = _chunk_conv_silu(q_ref, cw[:, :width], ci, rows)
            ka = _chunk_conv_silu(k_ref, cw[:, width:2 * width], ci, rows)
            va = _chunk_conv_silu(v_ref, cw[:, 2 * width:], ci, rows)
            zs = zs_ref[rows, :]
            sig = _sigmoid(zs)
            g_all = -rate_row * _softplus(zs + bias_row)
            for h in HEADS:
                units.append((ci, rows, h))
                qh, kh = qa[:, COLS[h]], ka[:, COLS[h]]
                qc.append(qh * (lax.rsqrt(jnp.sum(qh * qh, axis=-1, keepdims=True) + EPS) * (HEAD_W ** -0.5)))
                kc.append(kh * lax.rsqrt(jnp.sum(kh * kh, axis=-1, keepdims=True) + EPS))
                beta = sig[:, SL_DN_B + h:SL_DN_B + h + 1]
                g.append(g_all[:, SL_DN_A + h:SL_DN_A + h + 1])
                kb.append(kc[-1] * beta)
                kcb.append(kc[-1].astype(BF16))
                vbeta.append(va[:, COLS[h]] * beta)
        us = range(len(units))
        dmat = [_dot2(low_incl, jnp.broadcast_to(g[u], (CHUNK, CHUNK)) * strict_up) for u in us]
        kk = [_dot_nt(kb[u], kcb[u]) for u in us]
        qk = [_dot_nt(qc[u], kcb[u]) for u in us]
        gc = [dmat[u][:, :1] + g[u][:1, :] for u in us]
        decay = [jnp.where(tri_incl, jnp.exp(dmat[u]), 0.0) for u in us]
        eg = [jnp.exp(gc[u]) for u in us]
        a = [jnp.where(tri_strict, kk[u] * decay[u], 0.0).astype(BF16) for u in us]
        x = [jnp.concatenate([vbeta[u], kb[u] * eg[u]], axis=1) for u in us]
        ax = [_dot(a[u], x[u]) for u in us]
        p = [_dot(a[u], a[u]).astype(BF16) for u in us]
        x = [x[u] - ax[u] for u in us]
        for step in range(5):
            px = [_dot(p[u], x[u]) for u in us]
            if step < 4:
                p = [_dot(p[u], p[u]).astype(BF16) for u in us]
            x = [x[u] + px[u] for u in us]
        for u, (ci, rows, h) in enumerate(units):
            at_s[rows, h * HEAD_W:h * HEAD_W + CHUNK] = (qk[u] * decay[u]).astype(BF16)
            u_s[rows, COLS[h]] = x[u][:, :HEAD_W]
            w_s[rows, COLS[h]] = x[u][:, HEAD_W:].astype(BF16)
            g_last = gc[u][CHUNK - 1:CHUNK, :]
            qg_s[rows, COLS[h]] = (qc[u] * eg[u]).astype(BF16)
            kg_s[rows, COLS[h]] = (kc[u] * jnp.exp(g_last - gc[u])).astype(BF16)
            gl_s[pl.ds(pl.multiple_of(ci * 8, 8), 8), COLS[h]] = jnp.broadcast_to(jnp.exp(g_last), (8, HEAD_W))
        return carry

    assert n_chunks % UNROLL == 0
    lax.fori_loop(0, n_chunks // UNROLL, prep, 0)

    nw = nw_ref[...]

    def scan(ci, states):
        rows = _chunk_rows(ci)
        sb = [states[h].astype(BF16) for h in HEADS]
        ws = [jnp.dot(w_s[rows, COLS[h]], sb[h], preferred_element_type=F32) for h in HEADS]
        qs = [jnp.dot(qg_s[rows, COLS[h]], sb[h], preferred_element_type=F32) for h in HEADS]
        vb = [(u_s[rows, COLS[h]] - ws[h]).astype(BF16) for h in HEADS]
        av = [jnp.dot(at_s[rows, h * HEAD_W:h * HEAD_W + CHUNK], vb[h], preferred_element_type=F32)
              for h in HEADS]
        kv = [_dot_tn(kg_s[rows, COLS[h]], vb[h]) for h in HEADS]
        out = []
        for h in HEADS:
            gl = gl_s[pl.ds(pl.multiple_of(ci * 8, 8), 1), COLS[h]]
            out.append(states[h] * gl + kv[h])
            o_ref[rows, COLS[h]] = (_rms(qs[h] + av[h], nw)
                                    * _silu(zg_ref[rows, COLS[h]].astype(F32))).astype(o_ref.dtype)
        return tuple(out)

    lax.fori_loop(0, n_chunks, scan, tuple(jnp.zeros((HEAD_W, HEAD_W), F32) for _ in range(DN_HEADS)))


def deltanet(zb, zs, conv_dn, a_log, dt_bias, dn_norm, batch, seq):
    m = zb.shape[0]
    width = DN_HEADS * HEAD_W
    blk = lambda cb: pl.BlockSpec((seq, width), lambda b: (b, cb // DN_HEADS))
    return pl.pallas_call(
        _deltanet_kernel,
        grid=(batch,),
        in_specs=[pl.BlockSpec((8, LANE), lambda b: (0, 0)),
                  blk(CB_DN_Q), blk(CB_DN_K), blk(CB_DN_V), blk(CB_DN_Z),
                  pl.BlockSpec((seq, LANE), lambda b: (b, 0)),
                  pl.BlockSpec((CONV_K, 3 * width), lambda b: (0, 0)),
                  pl.BlockSpec((1, HEAD_W), lambda b: (0, 0))],
        out_specs=pl.BlockSpec((seq, width), lambda b: (b, 0)),
        out_shape=jax.ShapeDtypeStruct((m, width), BF16),
        scratch_shapes=[pltpu.VMEM((seq, width), F32), pltpu.VMEM((seq, width), BF16),
                        pltpu.VMEM((seq, width), BF16), pltpu.VMEM((seq, width), BF16),
                        pltpu.VMEM((seq, width), BF16),
                        pltpu.VMEM((seq // CHUNK * 8, width), F32)],
        compiler_params=_params(("parallel",)),
        name="deltanet",
    )(_lane_rows((SL_DN_A, a_log), (SL_DN_A, dt_bias)), zb, zb, zb, zb, zs, conv_dn,
      dn_norm.reshape(1, HEAD_W))


def _mlstm_kernel(gp_ref, q_ref, k_ref, v_ref, og_ref, zs_ref, cw_ref, nw_ref, o_ref):
    n_chunks = q_ref.shape[0] // CHUNK
    width = ML_HEADS * HEAD_W
    r, c = _chunk_masks()
    low_incl = (r >= c).astype(BF16)
    strict_up = (r > c).astype(F32)
    eye = (r == c).astype(F32)
    tri_incl = r >= c
    nw = nw_ref[...]
    cw = cw_ref[...]

    def local(ci):
        rows = _chunk_rows(ci)
        qa = _chunk_conv_silu(q_ref, cw[:, :width], ci, rows)
        ka = _chunk_conv_silu(k_ref, cw[:, width:], ci, rows) * (HEAD_W ** -0.5)
        zs = zs_ref[rows, :]
        ip_all = zs + gp_ref[0:1, :]
        lf_all = _log_sigmoid(zs + gp_ref[1:2, :])
        qc = [qa[:, COLS[h]] for h in HEADS]
        kc = [ka[:, COLS[h]] for h in HEADS]
        ip = [ip_all[:, SL_ML_I + h:SL_ML_I + h + 1] for h in HEADS]
        lf = [lf_all[:, SL_ML_F + h:SL_ML_F + h + 1] for h in HEADS]
        dl = [_dot2(low_incl, jnp.broadcast_to(lf[h], (CHUNK, CHUNK)) * strict_up
                    + jnp.broadcast_to(ip[h], (CHUNK, CHUNK)) * eye) for h in HEADS]
        qk = [_dot_nt(qc[h], kc[h]) for h in HEADS]
        heads = []
        for h in HEADS:
            b = dl[h][:, :1] - ip[h][:1, :] + lf[h][:1, :]
            dlog = jnp.where(tri_incl, dl[h], NEG)
            b_last = b[CHUNK - 1:CHUNK, :]
            a = b_last - b + ip[h]
            heads.append(dict(qc=qc[h], kc=kc[h], qk=qk[h], b=b, dlog=dlog, b_last=b_last, a=a,
                              dmax=jnp.max(dlog, axis=-1, keepdims=True),
                              a_max=jnp.max(a, axis=0, keepdims=True)))
        return rows, heads

    def advance(rows, heads, carry):
        vc = [v_ref[rows, COLS[h]] for h in HEADS]
        s, wk, inter, m_t, m_new, scale = [], [], [], [], [], []
        for h in HEADS:
            t = heads[h]
            m_st = carry[h][2]
            m_t.append(jnp.maximum(t["dmax"], t["b"] + m_st))
            s.append(t["qk"] * jnp.exp(t["dlog"] - m_t[h]))
            inter.append(jnp.exp(t["b"] + m_st - m_t[h]))
            m_new.append(jnp.maximum(t["b_last"] + m_st, t["a_max"]))
            scale.append(jnp.exp(t["b_last"] + m_st - m_new[h]))
            wk.append(t["kc"] * jnp.exp(t["a"] - m_new[h]))
        sv = [_dot(s[h], vc[h]) for h in HEADS]
        kv = [_dot_tn(wk[h], vc[h]) for h in HEADS]
        qcs = [_dot(heads[h]["qc"], carry[h][0]) for h in HEADS]
        out = []
        for h in HEADS:
            c_st, n_st, _ = carry[h]
            qc = heads[h]["qc"]
            num = sv[h] + inter[h] * qcs[h]
            den = (jnp.sum(s[h], axis=-1, keepdims=True)
                   + inter[h] * jnp.sum(qc * n_st, axis=-1, keepdims=True))
            hh = num / jnp.maximum(jnp.abs(den), jnp.exp(-m_t[h]))
            out.append((scale[h] * c_st + kv[h],
                        scale[h] * n_st + jnp.sum(wk[h], axis=0, keepdims=True),
                        m_new[h]))
            o_ref[rows, COLS[h]] = (_sigmoid(og_ref[rows, COLS[h]].astype(F32))
                                    * _rms(hh, nw)).astype(o_ref.dtype)
        return tuple(out)

    init = tuple((jnp.zeros((HEAD_W, HEAD_W), F32), jnp.zeros((1, HEAD_W), F32), jnp.zeros((1, 1), F32))
                 for _ in range(ML_HEADS))
    lax.fori_loop(0, n_chunks, lambda ci, carry: advance(*local(ci), carry), init)


def mlstm(zb, zs, conv_ml, i_bias, f_bias, ml_norm, batch, seq):
    m = zb.shape[0]
    width = ML_HEADS * HEAD_W
    blk = lambda cb: pl.BlockSpec((seq, width), lambda b: (b, cb // ML_HEADS))
    return pl.pallas_call(
        _mlstm_kernel,
        grid=(batch,),
        in_specs=[pl.BlockSpec((8, LANE), lambda b: (0, 0)),
                  blk(CB_ML_Q), blk(CB_ML_K), blk(CB_ML_V), blk(CB_ML_O),
                  pl.BlockSpec((seq, LANE), lambda b: (b, 0)),
                  pl.BlockSpec((CONV_K, 2 * width), lambda b: (0, 0)),
                  pl.BlockSpec((1, HEAD_W), lambda b: (0, 0))],
        out_specs=pl.BlockSpec((seq, width), lambda b: (b, 0)),
        out_shape=jax.ShapeDtypeStruct((m, width), BF16),
        compiler_params=_params(("parallel",)),
        name="mlstm",
    )(_lane_rows((SL_ML_I, i_bias), (SL_ML_F, f_bias)), zb, zb, zb, zb, zs, conv_ml,
      ml_norm.reshape(1, HEAD_W))


LOG2E = math.log2(math.e)
LOG2E_HI = float(np.asarray(LOG2E, dtype=BF16))
LOG2E_LO = LOG2E - LOG2E_HI
SLAB = 128
RAMP = 256


def _with_lanes(x, base, vals):
    lane = lax.broadcasted_iota(jnp.int32, x.shape, 1)
    for i, v in enumerate(vals):
        x = jnp.where(lane == base + i, v, x)
    return x


def _diff_attn_kernel(lq1_ref, lk1_ref, lq2_ref, lk2_ref, q_ref, k_ref, v_ref, nw_ref, o_ref,
                      k0_s, k1_s, *, tq, lambda_init):
    h = pl.program_id(1)
    qi = pl.program_id(2)
    tk = RAMP
    per_pair = tq // (2 * tk)
    assert k_ref.shape[0] <= RAMP * RAMP and tq % (2 * tk) == 0
    slope = jnp.exp2(jnp.full((1, 1), -(h + 1).astype(F32) * (8.0 / DIFF_HEADS), F32))

    @pl.when(qi == 0)
    def _():
        kk = k_ref[...]
        pos = lax.broadcasted_iota(jnp.int32, kk.shape, 0)
        fine = ((pos & (RAMP - 1)).astype(F32) * slope).astype(BF16)
        coarse = ((pos - (pos & (RAMP - 1))).astype(F32) * slope).astype(BF16)
        k0_s[...] = _with_lanes(kk, DIFF_DH, (fine, fine, coarse, coarse))
        k1_s[...] = _with_lanes(kk, 0, (fine, fine, coarse, coarse))

    lam = (jnp.exp(jnp.sum(lq1_ref[...] * lk1_ref[...], axis=-1, keepdims=True))
           - jnp.exp(jnp.sum(lq2_ref[...] * lk2_ref[...], axis=-1, keepdims=True)) + lambda_init)
    lane = lax.broadcasted_iota(jnp.int32, (tq, HEAD_W), 1)
    q = q_ref[...].astype(F32) * (DIFF_DH ** -0.5 * LOG2E)
    l2e = (LOG2E_HI, LOG2E_LO, LOG2E_HI, LOG2E_LO)
    q0 = _with_lanes(jnp.where(lane < DIFF_DH, q, 0.0), DIFF_DH, l2e).astype(BF16)
    q1 = _with_lanes(jnp.where(lane >= DIFF_DH, q, 0.0), 0, l2e).astype(BF16)
    per_map = tq // SLAB
    n_slab = 2 * per_map
    q_slab = [qm[j * SLAB:(j + 1) * SLAB, :] for qm in (q0, q1) for j in range(per_map)]

    def run_pair(kp, diag_at, carry):
        def needed(i, blk):
            return diag_at is None or diag_at + blk * tk <= (i % per_map) * SLAB + SLAB - 1

        scores, values = [], []
        for blk in range(2):
            rows = pl.ds(pl.multiple_of((2 * kp + blk) * tk, tk), tk)
            kb = (k0_s[rows, :], k1_s[rows, :])
            values.append(v_ref[rows, :])
            scores.append([_dot_nt(q_slab[i], kb[i // per_map]) if needed(i, blk) else None
                           for i in range(n_slab)])
        for blk in range(2):
            out = []
            for i in range(n_slab):
                if not needed(i, blk):
                    out.append(carry[i])
                    continue
                m_i, l_i, acc = carry[i]
                s = scores[blk][i]
                if diag_at is not None and diag_at + (blk + 1) * tk - 1 > (i % per_map) * SLAB:
                    ri = lax.broadcasted_iota(jnp.int32, (SLAB, tk), 0) + (i % per_map) * SLAB
                    ci = lax.broadcasted_iota(jnp.int32, (SLAB, tk), 1) + (diag_at + blk * tk)
                    s = jnp.where(ci <= ri, s, NEG)
                m_new = jnp.maximum(m_i, jnp.max(s, axis=-1, keepdims=True))
                p = jnp.exp2(s - m_new)
                alpha = jnp.exp2(m_i - m_new)
                l_new = alpha * l_i + sum(p[:, j * LANE:(j + 1) * LANE] for j in range(tk // LANE))
                acc_new = alpha * acc + jnp.dot(p.astype(BF16), values[blk], preferred_element_type=F32)
                out.append((m_new, l_new, acc_new))
            carry = tuple(out)
        return carry

    init = tuple((jnp.full((SLAB, 1), NEG, F32), jnp.zeros((SLAB, LANE), F32), jnp.zeros((SLAB, HEAD_W), F32))
                 for _ in range(n_slab))
    carry = lax.fori_loop(0, qi * per_pair, lambda kp, cr: run_pair(kp, None, cr), init)
    for t in range(per_pair):
        carry = run_pair(qi * per_pair + t, t * 2 * tk, carry)
    nw = nw_ref[...]
    for j in range(per_map):
        (_, l0, a0), (_, l1, a1) = carry[j], carry[per_map + j]
        l0 = jnp.sum(l0, axis=-1, keepdims=True)
        l1 = jnp.sum(l1, axis=-1, keepdims=True)
        o = a0 / l0 - lam * (a1 / l1)
        o_ref[j * SLAB:(j + 1) * SLAB, :] = (_rms(o, nw) * (1.0 - lambda_init)).astype(o_ref.dtype)


def diff_attention(zb, lq1, lk1, lq2, lk2, diff_norm, lambda_init, batch, seq, tq=512):
    m = zb.shape[0]
    tq = _tile(seq, tq)
    nq = seq // tq
    vec = pl.BlockSpec((1, DIFF_DH), lambda b, h, i: (0, 0))
    return pl.pallas_call(
        functools.partial(_diff_attn_kernel, tq=tq, lambda_init=lambda_init),
        grid=(batch, DIFF_HEADS, nq),
        in_specs=[vec, vec, vec, vec,
                  pl.BlockSpec((tq, HEAD_W), lambda b, h, i: (b * nq + i, CB_AQ + h)),
                  pl.BlockSpec((seq, HEAD_W), lambda b, h, i: (b, CB_AK + h)),
                  pl.BlockSpec((seq, HEAD_W), lambda b, h, i: (b, CB_AV + h)),
                  pl.BlockSpec((1, HEAD_W), lambda b, h, i: (0, 0))],
        out_specs=pl.BlockSpec((tq, HEAD_W), lambda b, h, i: (b * nq + i, h)),
        out_shape=jax.ShapeDtypeStruct((m, DIFF_HEADS * HEAD_W), BF16),
        scratch_shapes=[pltpu.VMEM((seq, HEAD_W), BF16), pltpu.VMEM((seq, HEAD_W), BF16)],
        compiler_params=_params(("parallel", "parallel", "arbitrary")),
        name="diff_attn",
    )(lq1.reshape(1, -1), lk1.reshape(1, -1), lq2.reshape(1, -1), lk2.reshape(1, -1),
      zb, zb, zb, diff_norm.reshape(1, HEAD_W))


def _out_proj_kernel(a_ref, b_ref, c_ref, wa_ref, wb_ref, wc_ref, h_ref, o_ref):
    acc = jnp.dot(a_ref[...], wa_ref[...], preferred_element_type=F32)
    acc += jnp.dot(b_ref[...], wb_ref[...], preferred_element_type=F32)
    acc += jnp.dot(c_ref[...], wc_ref[...], preferred_element_type=F32)
    o_ref[...] = h_ref[...] + acc


def out_proj(o_dn, o_diff, o_ml, wa, wb, wc, h, tm=512):
    m, d = h.shape
    tm = _tile(m, tm)
    row = lambda w: pl.BlockSpec((tm, w), lambda i: (i, 0))
    full = lambda w: pl.BlockSpec((w, d), lambda i: (0, 0))
    return pl.pallas_call(
        _out_proj_kernel,
        grid=(m // tm,),
        in_specs=[row(o_dn.shape[1]), row(o_diff.shape[1]), row(o_ml.shape[1]),
                  full(wa.shape[0]), full(wb.shape[0]), full(wc.shape[0]), row(d)],
        out_specs=row(d),
        out_shape=jax.ShapeDtypeStruct((m, d), F32),
        compiler_params=_params(("parallel",)),
        name="out_proj",
    )(o_dn, o_diff, o_ml, wa, wb, wc, h)


def _swiglu_step(xs, wg_ref, wu_ref, wd_ref):
    g = jnp.dot(xs, wg_ref[...].astype(BF16), preferred_element_type=F32)
    u = jnp.dot(xs, wu_ref[...].astype(BF16), preferred_element_type=F32)
    return jnp.dot((_silu(g) * u).astype(BF16), wd_ref[...].astype(BF16), preferred_element_type=F32)


def _dense_ffn_kernel(h_ref, nw_ref, wg_ref, wu_ref, wd_ref, o_ref, xs_ref):
    @pl.when(pl.program_id(1) == 0)
    def _():
        h = h_ref[...]
        xs_ref[...] = _rms(h, nw_ref[...]).astype(BF16)
        o_ref[...] = h

    o_ref[...] += _swiglu_step(xs_ref[...], wg_ref, wu_ref, wd_ref)


def dense_ffn(h, nw, wg, wu, wd, tm=1024, tf=256):
    m, d = h.shape
    dff = wg.shape[1]
    tm, tf = _tile(m, tm), _tile(dff, tf)
    return pl.pallas_call(
        _dense_ffn_kernel,
        grid=(m // tm, dff // tf),
        in_specs=[pl.BlockSpec((tm, d), lambda i, f: (i, 0), pipeline_mode=pl.Buffered(1)),
                  pl.BlockSpec((1, d), lambda i, f: (0, 0)),
                  pl.BlockSpec((d, tf), lambda i, f: (0, f)),
                  pl.BlockSpec((d, tf), lambda i, f: (0, f)),
                  pl.BlockSpec((tf, d), lambda i, f: (f, 0))],
        out_specs=pl.BlockSpec((tm, d), lambda i, f: (i, 0)),
        out_shape=jax.ShapeDtypeStruct((m, d), F32),
        scratch_shapes=[pltpu.VMEM((tm, d), BF16)],
        compiler_params=_params(("parallel", "arbitrary")),
        name="dense_ffn",
    )(h, nw.reshape(1, d), wg, wu, wd)


def _router_kernel(h_ref, nw_ref, wr_ref, xs_ref, idx_ref, wts_ref, cnt_ref, carry_ref):
    i = pl.program_id(0)
    tm = h_ref.shape[0]

    @pl.when(i == 0)
    def _():
        carry_ref[...] = jnp.zeros_like(carry_ref)

    xn = _rms(h_ref[...], nw_ref[...])
    xs_ref[...] = xn
    logits = _dot_hi(xn, wr_ref[...])
    lane = lax.broadcasted_iota(jnp.int32, logits.shape, 1)
    logits = jnp.where(lane < N_EXPERTS, logits, NEG)
    m1 = jnp.max(logits, axis=-1, keepdims=True)
    e1 = jnp.min(jnp.where(logits == m1, lane, LANE), axis=-1, keepdims=True)
    rest = jnp.where(lane == e1, NEG, logits)
    m2 = jnp.max(rest, axis=-1, keepdims=True)
    e2 = jnp.min(jnp.where(rest == m2, lane, LANE), axis=-1, keepdims=True)
    ex = jnp.exp(m2 - m1)
    w1 = 1.0 / (1.0 + ex)
    w2 = ex * w1
    oh1 = (lane == e1).astype(F32)
    oh2 = (lane == e2).astype(F32)
    oh = oh1 + oh2
    r = lax.broadcasted_iota(jnp.int32, (tm, tm), 0)
    c = lax.broadcasted_iota(jnp.int32, (tm, tm), 1)
    before = jnp.dot((r > c).astype(BF16), oh.astype(BF16), preferred_element_type=F32) + carry_ref[...]
    rank1 = jnp.sum(before * oh1, axis=-1, keepdims=True)
    rank2 = jnp.sum(before * oh2, axis=-1, keepdims=True)
    carry_ref[...] += jnp.sum(oh, axis=0, keepdims=True)
    cnt_ref[...] = jnp.broadcast_to(carry_ref[...], cnt_ref.shape)
    idx = jnp.where(lane == 0, e1, jnp.where(lane == 1, e2, 0))
    idx = jnp.where(lane == 2, rank1.astype(jnp.int32), jnp.where(lane == 3, rank2.astype(jnp.int32), idx))
    idx_ref[...] = idx
    wts_ref[...] = jnp.where(lane == 0, w1, jnp.where(lane == 1, w2, 0.0))


def moe_router(h, nw, router, tm=512):
    m, d = h.shape
    tm = _tile(m, tm)
    wr = jnp.zeros((d, LANE), F32).at[:, :N_EXPERTS].set(router)
    return pl.pallas_call(
        _router_kernel,
        grid=(m // tm,),
        in_specs=[pl.BlockSpec((tm, d), lambda i: (i, 0)),
                  pl.BlockSpec((1, d), lambda i: (0, 0)),
                  pl.BlockSpec((d, LANE), lambda i: (0, 0))],
        out_specs=[pl.BlockSpec((tm, d), lambda i: (i, 0)),
                   pl.BlockSpec((tm, LANE), lambda i: (i, 0)),
                   pl.BlockSpec((tm, LANE), lambda i: (i, 0)),
                   pl.BlockSpec((8, LANE), lambda i: (0, 0))],
        out_shape=[jax.ShapeDtypeStruct((m, d), F32),
                   jax.ShapeDtypeStruct((m, LANE), jnp.int32),
                   jax.ShapeDtypeStruct((m, LANE), F32),
                   jax.ShapeDtypeStruct((8, LANE), F32)],
        scratch_shapes=[pltpu.VMEM((1, LANE), F32)],
        compiler_params=_params(("arbitrary",)),
        name="moe_router",
    )(h, nw.reshape(1, d), wr)


def _dispatch_kernel(dest_ref, xs_ref, init_ref, out_ref, sem):
    del init_ref
    tm = xs_ref.shape[0]

    def copy(t, k):
        return pltpu.make_async_copy(xs_ref.at[pl.ds(t, 1), :],
                                     out_ref.at[pl.ds(dest_ref[2 * t + k], 1), :], sem)

    def start(t, c):
        copy(t, 0).start()
        copy(t, 1).start()
        return c

    def wait(t, c):
        copy(t, 0).wait()
        copy(t, 1).wait()
        return c

    lax.fori_loop(0, tm, start, 0)
    lax.fori_loop(0, tm, wait, 0)


def moe_dispatch(xs, dest, n_sorted, tm=256):
    m, d = xs.shape
    tm = _tile(m, tm)
    init = jnp.zeros((n_sorted, d), xs.dtype)
    return pl.pallas_call(
        _dispatch_kernel,
        grid=(m // tm,),
        in_specs=[pl.BlockSpec((2 * tm,), lambda i: (i,), memory_space=pltpu.SMEM),
                  pl.BlockSpec((tm, d), lambda i: (i, 0)),
                  pl.BlockSpec(memory_space=pl.ANY)],
        out_specs=pl.BlockSpec(memory_space=pl.ANY),
        out_shape=jax.ShapeDtypeStruct((n_sorted, d), xs.dtype),
        scratch_shapes=[pltpu.SemaphoreType.DMA(())],
        input_output_aliases={2: 0},
        compiler_params=_params(("arbitrary",)),
        name="moe_dispatch",
    )(dest, xs, init)


def _grouped_ffn_kernel(te_ref, nv_ref, x_ref, wg_ref, wu_ref, wd_ref, o_ref, xs_ref):
    del te_ref
    i, f = pl.program_id(0), pl.program_id(1)

    @pl.when(f == 0)
    def _():
        xs_ref[...] = x_ref[...].astype(BF16)
        o_ref[...] = jnp.zeros_like(o_ref)

    @pl.when(nv_ref[i] > 0)
    def _():
        o_ref[...] += _swiglu_step(xs_ref[...], wg_ref, wu_ref, wd_ref)


def grouped_ffn(xsorted, tile_expert, n_valid, wg, wu, wd, tm, tf=256):
    ns, d = xsorted.shape
    dff = wg.shape[2]
    tf = _tile(dff, tf)
    nf = dff // tf
    n_tiles = ns // tm

    def fidx(i, f, nv):
        return jnp.where(nv[i] > 0, f, nf - 1)

    return pl.pallas_call(
        _grouped_ffn_kernel,
        grid_spec=pltpu.PrefetchScalarGridSpec(
            num_scalar_prefetch=2,
            grid=(n_tiles, nf),
            in_specs=[pl.BlockSpec((tm, d), lambda i, f, te, nv: (i, 0), pipeline_mode=pl.Buffered(1)),
                      pl.BlockSpec((None, d, tf), lambda i, f, te, nv: (te[i], 0, fidx(i, f, nv))),
                      pl.BlockSpec((None, d, tf), lambda i, f, te, nv: (te[i], 0, fidx(i, f, nv))),
                      pl.BlockSpec((None, tf, d), lambda i, f, te, nv: (te[i], fidx(i, f, nv), 0))],
            out_specs=pl.BlockSpec((tm, d), lambda i, f, te, nv: (i, 0)),
            scratch_shapes=[pltpu.VMEM((tm, d), BF16)]),
        out_shape=jax.ShapeDtypeStruct((ns, d), F32),
        compiler_params=_params(("arbitrary", "arbitrary")),
        name="grouped_ffn",
    )(tile_expert, n_valid, xsorted, wg, wu, wd)


def _combine_kernel(dest_ref, ys_ref, h_ref, wts_ref, o_ref, y0_ref, y1_ref, sem):
    tm = h_ref.shape[0]

    def copy(t, k):
        dst = (y0_ref, y1_ref)[k]
        return pltpu.make_async_copy(ys_ref.at[pl.ds(dest_ref[2 * t + k], 1), :],
                                     dst.at[pl.ds(t, 1), :], sem)

    def start(t, c):
        copy(t, 0).start()
        copy(t, 1).start()
        return c

    def wait(t, c):
        copy(t, 0).wait()
        copy(t, 1).wait()
        return c

    lax.fori_loop(0, tm, start, 0)
    lax.fori_loop(0, tm, wait, 0)
    wts = wts_ref[...]
    o_ref[...] = h_ref[...] + wts[:, 0:1] * y0_ref[...] + wts[:, 1:2] * y1_ref[...]


def moe_combine(ysorted, dest, h, wts, tm=256):
    m, d = h.shape
    tm = _tile(m, tm)
    return pl.pallas_call(
        _combine_kernel,
        grid=(m // tm,),
        in_specs=[pl.BlockSpec((2 * tm,), lambda i: (i,), memory_space=pltpu.SMEM),
                  pl.BlockSpec(memory_space=pl.ANY),
                  pl.BlockSpec((tm, d), lambda i: (i, 0)),
                  pl.BlockSpec((tm, LANE), lambda i: (i, 0))],
        out_specs=pl.BlockSpec((tm, d), lambda i: (i, 0)),
        out_shape=jax.ShapeDtypeStruct((m, d), F32),
        scratch_shapes=[pltpu.VMEM((tm, d), F32), pltpu.VMEM((tm, d), F32),
                        pltpu.SemaphoreType.DMA(())],
        compiler_params=_params(("arbitrary",)),
        name="moe_combine",
    )(dest, ysorted, h, wts)


def moe_ffn(h, nw, router, wg, wu, wd, tmg=1024):
    m, d = h.shape
    tmg = min(tmg, m)
    xs, idx, wts, cnt = moe_router(h, nw, router)
    counts = cnt[0, :N_EXPERTS].astype(jnp.int32)
    padded = (counts + tmg - 1) // tmg * tmg
    ends = jnp.cumsum(padded)
    offsets = ends - padded
    n_tiles = (2 * m) // tmg + N_EXPERTS
    n_sorted = n_tiles * tmg
    dest = (offsets[idx[:, 0:2]] + idx[:, 2:4]).reshape(-1)
    tile_start = jnp.arange(n_tiles, dtype=jnp.int32) * tmg
    tile_expert = jnp.minimum(jnp.sum(tile_start[:, None] >= ends[None, :], axis=1), N_EXPERTS - 1)
    n_valid = jnp.clip((offsets + counts)[tile_expert] - tile_start, 0, tmg)
    xsorted = moe_dispatch(xs, dest, n_sorted)
    ysorted = grouped_ffn(xsorted, tile_expert.astype(jnp.int32), n_valid.astype(jnp.int32), wg, wu, wd, tmg)
    return moe_combine(ysorted, dest, h, wts)


def _ple_kernel(h_ref, p_ref, nw_ref, wg_ref, wp_ref, fw_ref, o_ref, *, final):
    h = h_ref[...]
    gate = _sigmoid(jnp.dot(_rms(h, nw_ref[...]).astype(BF16), wg_ref[...], preferred_element_type=F32))
    out = h + jnp.dot(p_ref[...].astype(BF16), wp_ref[...], preferred_element_type=F32) * gate
    o_ref[...] = _rms(out, fw_ref[...]) if final else out


def ple(h, p, nw, wg, wp, fw, final, tm=512):
    m, d = h.shape
    pd = p.shape[1]
    tm = _tile(m, tm)
    return pl.pallas_call(
        functools.partial(_ple_kernel, final=final),
        grid=(m // tm,),
        in_specs=[pl.BlockSpec((tm, d), lambda i: (i, 0)),
                  pl.BlockSpec((tm, pd), lambda i: (i, 0)),
                  pl.BlockSpec((1, d), lambda i: (0, 0)),
                  pl.BlockSpec((d, d), lambda i: (0, 0)),
                  pl.BlockSpec((pd, d), lambda i: (0, 0)),
                  pl.BlockSpec((1, d), lambda i: (0, 0))],
        out_specs=pl.BlockSpec((tm, d), lambda i: (i, 0)),
        out_shape=jax.ShapeDtypeStruct((m, d), F32),
        compiler_params=_params(("parallel",)),
        name="ple",
    )(h, p, nw.reshape(1, d), wg, wp, fw.reshape(1, d))


def _split_w_in(w):
    d = w.shape[0]
    dn_w, diff_w, ml_w = DN_HEADS * HEAD_W, DIFF_HEADS * HEAD_W, ML_HEADS * HEAD_W
    g0 = 3 * dn_w + 2 * ml_w + dn_w
    g1 = g0 + 2 * DN_HEADS
    g2 = g1 + 3 * diff_w + 2 * ml_w
    g3 = g2 + 2 * ML_HEADS
    assert w.shape[1] == g3 and g0 + g2 - g1 == N_BIG
    big = jnp.concatenate([w[:, :g0], w[:, g1:g2]], axis=1).astype(BF16)
    small = jnp.concatenate([w[:, g0:g1], w[:, g2:g3],
                             jnp.zeros((d, LANE - (g1 - g0) - (g3 - g2)), w.dtype)], axis=1).astype(BF16)
    return big, small


def kernel(x, p, attn_norm, w_in, conv_dn, conv_ml, dn_a_log, dn_dt_bias, dn_norm, diff_lq1, diff_lk1, diff_lq2, diff_lk2, diff_norm, ml_i_bias, ml_f_bias, ml_norm, w_out, ffn_norm, dense_w_gate, dense_w_up, dense_w_down, router, moe_w_gate, moe_w_up, moe_w_down, ple_norm, ple_proj, ple_gate, final_norm):
    batch, seq, d = x.shape
    depth = w_in.shape[0]
    m = batch * seq
    h = x.reshape(m, d)
    dn_w, diff_w = DN_HEADS * HEAD_W, DIFF_HEADS * HEAD_W
    for i in range(depth):
        lambda_init = 0.8 - 0.6 * math.exp(-0.3 * i)
        w_big, w_small = _split_w_in(w_in[i])
        zb, zs = norm_proj(h, attn_norm[i], w_big, w_small)
        o_dn = deltanet(zb, zs, conv_dn[i], dn_a_log[i], dn_dt_bias[i], dn_norm[i], batch, seq)
        o_diff = diff_attention(zb, diff_lq1[i], diff_lk1[i], diff_lq2[i], diff_lk2[i], diff_norm[i],
                                lambda_init, batch, seq)
        o_ml = mlstm(zb, zs, conv_ml[i], ml_i_bias[i], ml_f_bias[i], ml_norm[i], batch, seq)
        wo = w_out[i].astype(BF16)
        h = out_proj(o_dn, o_diff, o_ml, wo[:dn_w], wo[dn_w:dn_w + diff_w], wo[dn_w + diff_w:], h)
        j = i // 2
        if i % 2 == 0:
            h = dense_ffn(h, ffn_norm[i], dense_w_gate[j], dense_w_up[j], dense_w_down[j])
        else:
            h = moe_ffn(h, ffn_norm[i], router[j], moe_w_gate[j], moe_w_up[j], moe_w_down[j])
        h = ple(h, p[i].reshape(m, -1), ple_norm[i], ple_gate[i].astype(BF16), ple_proj[i].astype(BF16),
                final_norm, final=(i == depth - 1))
    return h.reshape(batch, seq, d)
```

```python
import functools
import math

import numpy as np
import jax
import jax.numpy as jnp
from jax import lax
from jax.experimental import pallas as pl
from jax.experimental.pallas import tpu as pltpu

F32 = jnp.float32
BF16 = jnp.bfloat16
EPS = 1e-6
LANE = 128
NEG = -1e30

DN_HEADS = 4
DIFF_HEADS = 8
ML_HEADS = 4
HEAD_W = 128
DIFF_DH = 64
CHUNK = 64
CONV_K = 4
N_EXPERTS = 8
VMEM_LIMIT = 56 * 1024 * 1024

CB_DN_Q, CB_DN_K, CB_DN_V = 0, 4, 8
CB_ML_Q, CB_ML_K = 12, 16
CB_DN_Z = 20
CB_AQ, CB_AK, CB_AV = 24, 32, 40
CB_ML_V, CB_ML_O = 48, 52
N_BIG = 56 * LANE
SL_DN_B, SL_DN_A, SL_ML_I, SL_ML_F = 0, 4, 8, 12


def _params(sem):
    return pltpu.CompilerParams(dimension_semantics=sem, vmem_limit_bytes=VMEM_LIMIT)


def _dot(a, b):
    return jnp.dot(a.astype(BF16), b.astype(BF16), preferred_element_type=F32)


def _dot_nt(a, b):
    return lax.dot_general(a.astype(BF16), b.astype(BF16), (((1,), (1,)), ((), ())),
                           preferred_element_type=F32)


def _dot_tn(a, b):
    return lax.dot_general(a.astype(BF16), b.astype(BF16), (((0,), (0,)), ((), ())),
                           preferred_element_type=F32)


def _dot_hi(a, b):
    return jnp.dot(a, b, preferred_element_type=F32, precision=lax.Precision.HIGHEST)


def _rms(x, w):
    return x * lax.rsqrt(jnp.mean(x * x, axis=-1, keepdims=True) + EPS) * w


def _sigmoid(x):
    return 1.0 / (1.0 + jnp.exp(-x))


def _silu(x):
    return x * _sigmoid(x)


def _softplus(x):
    return jnp.maximum(x, 0.0) + jnp.log(1.0 + jnp.exp(-jnp.abs(x)))


def _log_sigmoid(x):
    return -_softplus(-x)


def _tile(m, t):
    t = min(m, t)
    assert m % t == 0
    return t


def _norm_proj_kernel(x_ref, nw_ref, w_ref, ws_ref, o_ref, os_ref, xs_ref):
    @pl.when(pl.program_id(1) == 0)
    def _():
        xs_ref[...] = _rms(x_ref[...], nw_ref[...]).astype(BF16)
        os_ref[...] = jnp.dot(xs_ref[...], ws_ref[...], preferred_element_type=F32)

    o_ref[...] = jnp.dot(xs_ref[...], w_ref[...], preferred_element_type=F32).astype(o_ref.dtype)


def norm_proj(x, nw, w_big, w_small, tm=1024, tn=1792):
    m, d = x.shape
    n = w_big.shape[1]
    tm, tn = _tile(m, tm), _tile(n, tn)
    return pl.pallas_call(
        _norm_proj_kernel,
        grid=(m // tm, n // tn),
        in_specs=[pl.BlockSpec((tm, d), lambda i, j: (i, 0)),
                  pl.BlockSpec((1, d), lambda i, j: (0, 0)),
                  pl.BlockSpec((d, tn), lambda i, j: (0, j)),
                  pl.BlockSpec((d, LANE), lambda i, j: (0, 0))],
        out_specs=[pl.BlockSpec((tm, tn), lambda i, j: (i, j)),
                   pl.BlockSpec((tm, LANE), lambda i, j: (i, 0))],
        out_shape=[jax.ShapeDtypeStruct((m, n), BF16), jax.ShapeDtypeStruct((m, LANE), F32)],
        scratch_shapes=[pltpu.VMEM((tm, d), BF16)],
        compiler_params=_params(("parallel", "arbitrary")),
        name="norm_proj",
    )(x, nw.reshape(1, d), w_big, w_small)


def _chunk_conv_silu(ref, w, ci, rows):
    cur = ref[rows, :].astype(F32)
    prev_rows = pl.ds(pl.multiple_of(jnp.maximum(ci * CHUNK - 16, 0), 16), 16)
    prev = jnp.where(ci > 0, ref[prev_rows, :].astype(F32)[8:16, :], 0.0)
    ext = jnp.concatenate([prev, cur], axis=0)
    y = cur * w[CONV_K - 1:CONV_K, :]
    for back in range(1, CONV_K):
        y = y + pltpu.roll(ext, back, axis=0)[8:8 + CHUNK, :] * w[CONV_K - 1 - back:CONV_K - back, :]
    return _silu(y)


def _dot2(l_bf16, x):
    hi = x.astype(BF16)
    lo = (x - hi.astype(F32)).astype(BF16)
    return (jnp.dot(l_bf16, hi, preferred_element_type=F32)
            + jnp.dot(l_bf16, lo, preferred_element_type=F32))


def _lane_rows(*rows):
    out = jnp.zeros((8, LANE), F32)
    for r, (lane0, vals) in enumerate(rows):
        out = out.at[r, lane0:lane0 + vals.shape[0]].set(vals.astype(F32))
    return out


def _chunk_masks():
    r = lax.broadcasted_iota(jnp.int32, (CHUNK, CHUNK), 0)
    c = lax.broadcasted_iota(jnp.int32, (CHUNK, CHUNK), 1)
    return r, c


def _chunk_rows(ci):
    return pl.ds(pl.multiple_of(ci * CHUNK, CHUNK), CHUNK)


assert DN_HEADS == ML_HEADS
HEADS = range(DN_HEADS)
COLS = [slice(h * HEAD_W, (h + 1) * HEAD_W) for h in HEADS]
UNROLL = 2


def _deltanet_kernel(gp_ref, q_ref, k_ref, v_ref, zg_ref, zs_ref, cw_ref, nw_ref, o_ref,
                     u_s, w_s, at_s, qg_s, kg_s, gl_s):
    n_chunks = q_ref.shape[0] // CHUNK
    width = DN_HEADS * HEAD_W
    r, c = _chunk_masks()
    low_incl = (r >= c).astype(BF16)
    strict_up = (r > c).astype(F32)
    tri_incl = r >= c
    tri_strict = r > c
    cw = cw_ref[...]
    rate_row = jnp.exp(gp_ref[0:1, :])
    bias_row = gp_ref[1:2, :]

    def prep(cp, carry):
        units, qc, kc, kb, kcb, vbeta, g = [], [], [], [], [], [], []
        for t in range(UNROLL):
            ci = UNROLL * cp + t
            rows = _chunk_rows(ci)
            qa = _chunk_conv_silu(q_ref, cw[:, :width], ci, rows)
            ka = _chunk_conv_silu(k_ref, cw[:, width:2 * width], ci, rows)
            va = _chunk_conv_silu(v_ref, cw[:, 2 * width:], ci, rows)
            zs = zs_ref[rows, :]
            sig = _sigmoid(zs)
            g_all = -rate_row * _softplus(zs + bias_row)
            for h in HEADS:
                units.append((ci, rows, h))
                qh, kh = qa[:, COLS[h]], ka[:, COLS[h]]
                qc.append(qh * (lax.rsqrt(jnp.sum(qh * qh, axis=-1, keepdims=True) + EPS) * (HEAD_W ** -0.5)))
                kc.append(kh * lax.rsqrt(jnp.sum(kh * kh, axis=-1, keepdims=True) + EPS))
                beta = sig[:, SL_DN_B + h:SL_DN_B + h + 1]
                g.append(g_all[:, SL_DN_A + h:SL_DN_A + h + 1])
                kb.append(kc[-1] * beta)
                kcb.append(kc[-1].astype(BF16))
                vbeta.append(va[:, COLS[h]] * beta)
        us = range(len(units))
        dmat = [_dot2(low_incl, jnp.broadcast_to(g[u], (CHUNK, CHUNK)) * strict_up) for u in us]
        kk = [_dot_nt(kb[u], kcb[u]) for u in us]
        qk = [_dot_nt(qc[u], kcb[u]) for u in us]
        gc = [dmat[u][:, :1] + g[u][:1, :] for u in us]
        decay = [jnp.where(tri_incl, jnp.exp(dmat[u]), 0.0) for u in us]
        eg = [jnp.exp(gc[u]) for u in us]
        a = [jnp.where(tri_strict, kk[u] * decay[u], 0.0).astype(BF16) for u in us]
        x = [jnp.concatenate([vbeta[u], kb[u] * eg[u]], axis=1) for u in us]
        ax = [_dot(a[u], x[u]) for u in us]
        p = [_dot(a[u], a[u]).astype(BF16) for u in us]
        x = [x[u] - ax[u] for u in us]
        for step in range(5):
            px = [_dot(p[u], x[u]) for u in us]
            if step < 4:
                p = [_dot(p[u], p[u]).astype(BF16) for u in us]
            x = [x[u] + px[u] for u in us]
        for u, (ci, rows, h) in enumerate(units):
            at_s[rows, h * HEAD_W:h * HEAD_W + CHUNK] = (qk[u] * decay[u]).astype(BF16)
            u_s[rows, COLS[h]] = x[u][:, :HEAD_W]
            w_s[rows, COLS[h]] = x[u][:, HEAD_W:].astype(BF16)
            g_last = gc[u][CHUNK - 1:CHUNK, :]
            qg_s[rows, COLS[h]] = (qc[u] * eg[u]).astype(BF16)
            kg_s[rows, COLS[h]] = (kc[u] * jnp.exp(g_last - gc[u])).astype(BF16)
            gl_s[pl.ds(pl.multiple_of(ci * 8, 8), 8), COLS[h]] = jnp.broadcast_to(jnp.exp(g_last), (8, HEAD_W))
        return carry

    assert n_chunks % UNROLL == 0
    lax.fori_loop(0, n_chunks // UNROLL, prep, 0)

    nw = nw_ref[...]

    def scan(ci, states):
        rows = _chunk_rows(ci)
        sb = [states[h].astype(BF16) for h in HEADS]
        ws = [jnp.dot(w_s[rows, COLS[h]], sb[h], preferred_element_type=F32) for h in HEADS]
        qs = [jnp.dot(qg_s[rows, COLS[h]], sb[h], preferred_element_type=F32) for h in HEADS]
        vb = [(u_s[rows, COLS[h]] - ws[h]).astype(BF16) for h in HEADS]
        av = [jnp.dot(at_s[rows, h * HEAD_W:h * HEAD_W + CHUNK], vb[h], preferred_element_type=F32)
              for h in HEADS]
        kv = [_dot_tn(kg_s[rows, COLS[h]], vb[h]) for h in HEADS]
        out = []
        for h in HEADS:
            gl = gl_s[pl.ds(pl.multiple_of(ci * 8, 8), 1), COLS[h]]
            out.append(states[h] * gl + kv[h])
            o_ref[rows, COLS[h]] = (_rms(qs[h] + av[h], nw)
                                    * _silu(zg_ref[rows, COLS[h]].astype(F32))).astype(o_ref.dtype)
        return tuple(out)

    lax.fori_loop(0, n_chunks, scan, tuple(jnp.zeros((HEAD_W, HEAD_W), F32) for _ in range(DN_HEADS)))


def deltanet(zb, zs, conv_dn, a_log, dt_bias, dn_norm, batch, seq):
    m = zb.shape[0]
    width = DN_HEADS * HEAD_W
    blk = lambda cb: pl.BlockSpec((seq, width), lambda b: (b, cb // DN_HEADS))
    return pl.pallas_call(
        _deltanet_kernel,
        grid=(batch,),
        in_specs=[pl.BlockSpec((8, LANE), lambda b: (0, 0)),
                  blk(CB_DN_Q), blk(CB_DN_K), blk(CB_DN_V), blk(CB_DN_Z),
                  pl.BlockSpec((seq, LANE), lambda b: (b, 0)),
                  pl.BlockSpec((CONV_K, 3 * width), lambda b: (0, 0)),
                  pl.BlockSpec((1, HEAD_W), lambda b: (0, 0))],
        out_specs=pl.BlockSpec((seq, width), lambda b: (b, 0)),
        out_shape=jax.ShapeDtypeStruct((m, width), BF16),
        scratch_shapes=[pltpu.VMEM((seq, width), F32), pltpu.VMEM((seq, width), BF16),
                        pltpu.VMEM((seq, width), BF16), pltpu.VMEM((seq, width), BF16),
                        pltpu.VMEM((seq, width), BF16),
                        pltpu.VMEM((seq // CHUNK * 8, width), F32)],
        compiler_params=_params(("parallel",)),
        name="deltanet",
    )(_lane_rows((SL_DN_A, a_log), (SL_DN_A, dt_bias)), zb, zb, zb, zb, zs, conv_dn,
      dn_norm.reshape(1, HEAD_W))


def _mlstm_kernel(gp_ref, q_ref, k_ref, v_ref, og_ref, zs_ref, cw_ref, nw_ref, o_ref):
    n_chunks = q_ref.shape[0] // CHUNK
    width = ML_HEADS * HEAD_W
    r, c = _chunk_masks()
    low_incl = (r >= c).astype(BF16)
    strict_up = (r > c).astype(F32)
    eye = (r == c).astype(F32)
    tri_incl = r >= c
    nw = nw_ref[...]
    cw = cw_ref[...]

    def local(ci):
        rows = _chunk_rows(ci)
        qa = _chunk_conv_silu(q_ref, cw[:, :width], ci, rows)
        ka = _chunk_conv_silu(k_ref, cw[:, width:], ci, rows) * (HEAD_W ** -0.5)
        zs = zs_ref[rows, :]
        ip_all = zs + gp_ref[0:1, :]
        lf_all = _log_sigmoid(zs + gp_ref[1:2, :])
        qc = [qa[:, COLS[h]] for h in HEADS]
        kc = [ka[:, COLS[h]] for h in HEADS]
        ip = [ip_all[:, SL_ML_I + h:SL_ML_I + h + 1] for h in HEADS]
        lf = [lf_all[:, SL_ML_F + h:SL_ML_F + h + 1] for h in HEADS]
        dl = [_dot2(low_incl, jnp.broadcast_to(lf[h], (CHUNK, CHUNK)) * strict_up
                    + jnp.broadcast_to(ip[h], (CHUNK, CHUNK)) * eye) for h in HEADS]
        qk = [_dot_nt(qc[h], kc[h]) for h in HEADS]
        heads = []
        for h in HEADS:
            b = dl[h][:, :1] - ip[h][:1, :] + lf[h][:1, :]
            dlog = jnp.where(tri_incl, dl[h], NEG)
            b_last = b[CHUNK - 1:CHUNK, :]
            a = b_last - b + ip[h]
            heads.append(dict(qc=qc[h], kc=kc[h], qk=qk[h], b=b, dlog=dlog, b_last=b_last, a=a,
                              dmax=jnp.max(dlog, axis=-1, keepdims=True),
                              a_max=jnp.max(a, axis=0, keepdims=True)))
        return rows, heads

    def advance(rows, heads, carry):
        vc = [v_ref[rows, COLS[h]] for h in HEADS]
        s, wk, inter, m_t, m_new, scale = [], [], [], [], [], []
        for h in HEADS:
            t = heads[h]
            m_st = carry[h][2]
            m_t.append(jnp.maximum(t["dmax"], t["b"] + m_st))
            s.append(t["qk"] * jnp.exp(t["dlog"] - m_t[h]))
            inter.append(jnp.exp(t["b"] + m_st - m_t[h]))
            m_new.append(jnp.maximum(t["b_last"] + m_st, t["a_max"]))
            scale.append(jnp.exp(t["b_last"] + m_st - m_new[h]))
            wk.append(t["kc"] * jnp.exp(t["a"] - m_new[h]))
        sv = [_dot(s[h], vc[h]) for h in HEADS]
        kv = [_dot_tn(wk[h], vc[h]) for h in HEADS]
        qcs = [_dot(heads[h]["qc"], carry[h][0]) for h in HEADS]
        out = []
        for h in HEADS:
            c_st, n_st, _ = carry[h]
            qc = heads[h]["qc"]
            num = sv[h] + inter[h] * qcs[h]
            den = (jnp.sum(s[h], axis=-1, keepdims=True)
                   + inter[h] * jnp.sum(qc * n_st, axis=-1, keepdims=True))
            hh = num / jnp.maximum(jnp.abs(den), jnp.exp(-m_t[h]))
            out.append((scale[h] * c_st + kv[h],
                        scale[h] * n_st + jnp.sum(wk[h], axis=0, keepdims=True),
                        m_new[h]))
            o_ref[rows, COLS[h]] = (_sigmoid(og_ref[rows, COLS[h]].astype(F32))
                                    * _rms(hh, nw)).astype(o_ref.dtype)
        return tuple(out)

    init = tuple((jnp.zeros((HEAD_W, HEAD_W), F32), jnp.zeros((1, HEAD_W), F32), jnp.zeros((1, 1), F32))
                 for _ in range(ML_HEADS))
    lax.fori_loop(0, n_chunks, lambda ci, carry: advance(*local(ci), carry), init)


def mlstm(zb, zs, conv_ml, i_bias, f_bias, ml_norm, batch, seq):
    m = zb.shape[0]
    width = ML_HEADS * HEAD_W
    blk = lambda cb: pl.BlockSpec((seq, width), lambda b: (b, cb // ML_HEADS))
    return pl.pallas_call(
        _mlstm_kernel,
        grid=(batch,),
        in_specs=[pl.BlockSpec((8, LANE), lambda b: (0, 0)),
                  blk(CB_ML_Q), blk(CB_ML_K), blk(CB_ML_V), blk(CB_ML_O),
                  pl.BlockSpec((seq, LANE), lambda b: (b, 0)),
                  pl.BlockSpec((CONV_K, 2 * width), lambda b: (0, 0)),
                  pl.BlockSpec((1, HEAD_W), lambda b: (0, 0))],
        out_specs=pl.BlockSpec((seq, width), lambda b: (b, 0)),
        out_shape=jax.ShapeDtypeStruct((m, width), BF16),
        compiler_params=_params(("parallel",)),
        name="mlstm",
    )(_lane_rows((SL_ML_I, i_bias), (SL_ML_F, f_bias)), zb, zb, zb, zb, zs, conv_ml,
      ml_norm.reshape(1, HEAD_W))


LOG2E = math.log2(math.e)
LOG2E_HI = float(np.asarray(LOG2E, dtype=BF16))
LOG2E_LO = LOG2E - LOG2E_HI
SLAB = 128
RAMP = 256


def _with_lanes(x, base, vals):
    lane = lax.broadcasted_iota(jnp.int32, x.shape, 1)
    for i, v in enumerate(vals):
        x = jnp.where(lane == base + i, v, x)
    return x


def _diff_attn_kernel(lq1_ref, lk1_ref, lq2_ref, lk2_ref, q_ref, k_ref, v_ref, nw_ref, o_ref,
                      k0_s, k1_s, *, tq, lambda_init):
    h = pl.program_id(1)
    qi = pl.program_id(2)
    tk = RAMP
    per_pair = tq // (2 * tk)
    assert k_ref.shape[0] <= RAMP * RAMP and tq % (2 * tk) == 0
    slope = jnp.exp2(jnp.full((1, 1), -(h + 1).astype(F32) * (8.0 / DIFF_HEADS), F32))

    @pl.when(qi == 0)
    def _():
        kk = k_ref[...]
        pos = lax.broadcasted_iota(jnp.int32, kk.shape, 0)
        fine = ((pos & (RAMP - 1)).astype(F32) * slope).astype(BF16)
        coarse = ((pos - (pos & (RAMP - 1))).astype(F32) * slope).astype(BF16)
        k0_s[...] = _with_lanes(kk, DIFF_DH, (fine, fine, coarse, coarse))
        k1_s[...] = _with_lanes(kk, 0, (fine, fine, coarse, coarse))

    lam = (jnp.exp(jnp.sum(lq1_ref[...] * lk1_ref[...], axis=-1, keepdims=True))
           - jnp.exp(jnp.sum(lq2_ref[...] * lk2_ref[...], axis=-1, keepdims=True)) + lambda_init)
    lane = lax.broadcasted_iota(jnp.int32, (tq, HEAD_W), 1)
    q = q_ref[...].astype(F32) * (DIFF_DH ** -0.5 * LOG2E)
    l2e = (LOG2E_HI, LOG2E_LO, LOG2E_HI, LOG2E_LO)
    q0 = _with_lanes(jnp.where(lane < DIFF_DH, q, 0.0), DIFF_DH, l2e).astype(BF16)
    q1 = _with_lanes(jnp.where(lane >= DIFF_DH, q, 0.0), 0, l2e).astype(BF16)
    per_map = tq // SLAB
    n_slab = 2 * per_map
    q_slab = [qm[j * SLAB:(j + 1) * SLAB, :] for qm in (q0, q1) for j in range(per_map)]

    def run_pair(kp, diag_at, carry):
        def needed(i, blk):
            return diag_at is None or diag_at + blk * tk <= (i % per_map) * SLAB + SLAB - 1

        scores, values = [], []
        for blk in range(2):
            rows = pl.ds(pl.multiple_of((2 * kp + blk) * tk, tk), tk)
            kb = (k0_s[rows, :], k1_s[rows, :])
            values.append(v_ref[rows, :])
            scores.append([_dot_nt(q_slab[i], kb[i // per_map]) if needed(i, blk) else None
                           for i in range(n_slab)])
        for blk in range(2):
            out = []
            for i in range(n_slab):
                if not needed(i, blk):
                    out.append(carry[i])
                    continue
                m_i, l_i, acc = carry[i]
                s = scores[blk][i]
                if diag_at is not None and diag_at + (blk + 1) * tk - 1 > (i % per_map) * SLAB:
                    ri = lax.broadcasted_iota(jnp.int32, (SLAB, tk), 0) + (i % per_map) * SLAB
                    ci = lax.broadcasted_iota(jnp.int32, (SLAB, tk), 1) + (diag_at + blk * tk)
                    s = jnp.where(ci <= ri, s, NEG)
                m_new = jnp.maximum(m_i, jnp.max(s, axis=-1, keepdims=True))
                p = jnp.exp2(s - m_new)
                alpha = jnp.exp2(m_i - m_new)
                l_new = alpha * l_i + sum(p[:, j * LANE:(j + 1) * LANE] for j in range(tk // LANE))
                acc_new = alpha * acc + jnp.dot(p.astype(BF16), values[blk], preferred_element_type=F32)
                out.append((m_new, l_new, acc_new))
            carry = tuple(out)
        return carry

    init = tuple((jnp.full((SLAB, 1), NEG, F32), jnp.zeros((SLAB, LANE), F32), jnp.zeros((SLAB, HEAD_W), F32))
                 for _ in range(n_slab))
    carry = lax.fori_loop(0, qi * per_pair, lambda kp, cr: run_pair(kp, None, cr), init)
    for t in range(per_pair):
        carry = run_pair(qi * per_pair + t, t * 2 * tk, carry)
    nw = nw_ref[...]
    for j in range(per_map):
        (_, l0, a0), (_, l1, a1) = carry[j], carry[per_map + j]
        l0 = jnp.sum(l0, axis=-1, keepdims=True)
        l1 = jnp.sum(l1, axis=-1, keepdims=True)
        o = a0 / l0 - lam * (a1 / l1)
        o_ref[j * SLAB:(j + 1) * SLAB, :] = (_rms(o, nw) * (1.0 - lambda_init)).astype(o_ref.dtype)


def diff_attention(zb, lq1, lk1, lq2, lk2, diff_norm, lambda_init, batch, seq, tq=512):
    m = zb.shape[0]
    tq = _tile(seq, tq)
    nq = seq // tq
    vec = pl.BlockSpec((1, DIFF_DH), lambda b, h, i: (0, 0))
    return pl.pallas_call(
        functools.partial(_diff_attn_kernel, tq=tq, lambda_init=lambda_init),
        grid=(batch, DIFF_HEADS, nq),
        in_specs=[vec, vec, vec, vec,
                  pl.BlockSpec((tq, HEAD_W), lambda b, h, i: (b * nq + i, CB_AQ + h)),
                  pl.BlockSpec((seq, HEAD_W), lambda b, h, i: (b, CB_AK + h)),
                  pl.BlockSpec((seq, HEAD_W), lambda b, h, i: (b, CB_AV + h)),
                  pl.BlockSpec((1, HEAD_W), lambda b, h, i: (0, 0))],
        out_specs=pl.BlockSpec((tq, HEAD_W), lambda b, h, i: (b * nq + i, h)),
        out_shape=jax.ShapeDtypeStruct((m, DIFF_HEADS * HEAD_W), BF16),
        scratch_shapes=[pltpu.VMEM((seq, HEAD_W), BF16), pltpu.VMEM((seq, HEAD_W), BF16)],
        compiler_params=_params(("parallel", "parallel", "arbitrary")),
        name="diff_attn",
    )(lq1.reshape(1, -1), lk1.reshape(1, -1), lq2.reshape(1, -1), lk2.reshape(1, -1),
      zb, zb, zb, diff_norm.reshape(1, HEAD_W))


def _out_proj_kernel(a_ref, b_ref, c_ref, wa_ref, wb_ref, wc_ref, h_ref, o_ref):
    acc = jnp.dot(a_ref[...], wa_ref[...], preferred_element_type=F32)
    acc += jnp.dot(b_ref[...], wb_ref[...], preferred_element_type=F32)
    acc += jnp.dot(c_ref[...], wc_ref[...], preferred_element_type=F32)
    o_ref[...] = h_ref[...] + acc


def out_proj(o_dn, o_diff, o_ml, wa, wb, wc, h, tm=512):
    m, d = h.shape
    tm = _tile(m, tm)
    row = lambda w: pl.BlockSpec((tm, w), lambda i: (i, 0))
    full = lambda w: pl.BlockSpec((w, d), lambda i: (0, 0))
    return pl.pallas_call(
        _out_proj_kernel,
        grid=(m // tm,),
        in_specs=[row(o_dn.shape[1]), row(o_diff.shape[1]), row(o_ml.shape[1]),
                  full(wa.shape[0]), full(wb.shape[0]), full(wc.shape[0]), row(d)],
        out_specs=row(d),
        out_shape=jax.ShapeDtypeStruct((m, d), F32),
        compiler_params=_params(("parallel",)),
        name="out_proj",
    )(o_dn, o_diff, o_ml, wa, wb, wc, h)


RING = 3


def _ffn_kernel(te_ref, na_ref, x_ref, nw_ref, wg_hbm, wu_hbm, wd_hbm, o_ref, xs_ref, wg_b, wu_b, wd_b, sems,
                *, fuse_norm):
    i, f = pl.program_id(0), pl.program_id(1)
    nf = pl.num_programs(1)
    tf = wg_b.shape[2]
    step = i * nf + f
    total = na_ref[0] * nf

    def copies(s):
        e = te_ref[s // nf]
        cols = pl.ds(pl.multiple_of((s % nf) * tf, tf), tf)
        slot = s % RING
        return (pltpu.make_async_copy(wg_hbm.at[e, :, cols], wg_b.at[slot], sems.at[slot, 0]),
                pltpu.make_async_copy(wu_hbm.at[e, :, cols], wu_b.at[slot], sems.at[slot, 1]),
                pltpu.make_async_copy(wd_hbm.at[e, cols, :], wd_b.at[slot], sems.at[slot, 2]))

    def start(s):
        @pl.when(s < total)
        def _():
            for c in copies(s):
                c.start()

    @pl.when(step == 0)
    def _():
        for s in range(RING - 1):
            start(s)

    @pl.when(f == 0)
    def _():
        x = x_ref[...]
        if fuse_norm:
            xs_ref[...] = _rms(x, nw_ref[...]).astype(BF16)
            o_ref[...] = x
        else:
            xs_ref[...] = x.astype(BF16)
            o_ref[...] = jnp.zeros_like(o_ref)

    @pl.when(i < na_ref[0])
    def _():
        start(step + RING - 1)
        for c in copies(step):
            c.wait()
        slot = step % RING
        xs = xs_ref[...]
        g = jnp.dot(xs, wg_b[slot].astype(BF16), preferred_element_type=F32)
        u = jnp.dot(xs, wu_b[slot].astype(BF16), preferred_element_type=F32)
        o_ref[...] += jnp.dot((_silu(g) * u).astype(BF16), wd_b[slot].astype(BF16), preferred_element_type=F32)


def ffn(x, nw, tile_expert, n_active, wg, wu, wd, tm, tf, fuse_norm, name):
    rows, d = x.shape
    dff = wg.shape[2]
    tf = _tile(dff, tf)
    any_spec = pl.BlockSpec(memory_space=pl.ANY)
    return pl.pallas_call(
        functools.partial(_ffn_kernel, fuse_norm=fuse_norm),
        grid_spec=pltpu.PrefetchScalarGridSpec(
            num_scalar_prefetch=2,
            grid=(rows // tm, dff // tf),
            in_specs=[pl.BlockSpec((tm, d), lambda i, f, te, na: (i, 0), pipeline_mode=pl.Buffered(1)),
                      pl.BlockSpec((1, d), lambda i, f, te, na: (0, 0)),
                      any_spec, any_spec, any_spec],
            out_specs=pl.BlockSpec((tm, d), lambda i, f, te, na: (i, 0)),
            scratch_shapes=[pltpu.VMEM((tm, d), BF16),
                            pltpu.VMEM((RING, d, tf), F32), pltpu.VMEM((RING, d, tf), F32),
                            pltpu.VMEM((RING, tf, d), F32), pltpu.SemaphoreType.DMA((RING, 3))]),
        out_shape=jax.ShapeDtypeStruct((rows, d), F32),
        compiler_params=_params(("arbitrary", "arbitrary")),
        name=name,
    )(tile_expert, n_active, x, nw.reshape(1, d), wg, wu, wd)


def dense_ffn(h, nw, wg, wu, wd, tm=1024, tf=256):
    tm = _tile(h.shape[0], tm)
    n_tiles = h.shape[0] // tm
    return ffn(h, nw, jnp.zeros((n_tiles,), jnp.int32), jnp.full((1,), n_tiles, jnp.int32),
               wg[None], wu[None], wd[None], tm, tf, True, "dense_ffn")


def _router_kernel(h_ref, nw_ref, wr_ref, xs_ref, idx_ref, wts_ref, cnt_ref, carry_ref):
    i = pl.program_id(0)
    tm = h_ref.shape[0]

    @pl.when(i == 0)
    def _():
        carry_ref[...] = jnp.zeros_like(carry_ref)

    xn = _rms(h_ref[...], nw_ref[...])
    xs_ref[...] = xn
    logits = _dot_hi(xn, wr_ref[...])
    lane = lax.broadcasted_iota(jnp.int32, logits.shape, 1)
    logits = jnp.where(lane < N_EXPERTS, logits, NEG)
    m1 = jnp.max(logits, axis=-1, keepdims=True)
    e1 = jnp.min(jnp.where(logits == m1, lane, LANE), axis=-1, keepdims=True)
    rest = jnp.where(lane == e1, NEG, logits)
    m2 = jnp.max(rest, axis=-1, keepdims=True)
    e2 = jnp.min(jnp.where(rest == m2, lane, LANE), axis=-1, keepdims=True)
    ex = jnp.exp(m2 - m1)
    w1 = 1.0 / (1.0 + ex)
    w2 = ex * w1
    oh1 = (lane == e1).astype(F32)
    oh2 = (lane == e2).astype(F32)
    oh = oh1 + oh2
    r = lax.broadcasted_iota(jnp.int32, (tm, tm), 0)
    c = lax.broadcasted_iota(jnp.int32, (tm, tm), 1)
    before = jnp.dot((r > c).astype(BF16), oh.astype(BF16), preferred_element_type=F32) + carry_ref[...]
    rank1 = jnp.sum(before * oh1, axis=-1, keepdims=True)
    rank2 = jnp.sum(before * oh2, axis=-1, keepdims=True)
    carry_ref[...] += jnp.sum(oh, axis=0, keepdims=True)
    cnt_ref[...] = jnp.broadcast_to(carry_ref[...], cnt_ref.shape)
    idx = jnp.where(lane == 0, e1, jnp.where(lane == 1, e2, 0))
    idx = jnp.where(lane == 2, rank1.astype(jnp.int32), jnp.where(lane == 3, rank2.astype(jnp.int32), idx))
    idx_ref[...] = idx
    wts_ref[...] = jnp.where(lane == 0, w1, jnp.where(lane == 1, w2, 0.0))


def moe_router(h, nw, router, tm=512):
    m, d = h.shape
    tm = _tile(m, tm)
    wr = jnp.zeros((d, LANE), F32).at[:, :N_EXPERTS].set(router)
    return pl.pallas_call(
        _router_kernel,
        grid=(m // tm,),
        in_specs=[pl.BlockSpec((tm, d), lambda i: (i, 0)),
                  pl.BlockSpec((1, d), lambda i: (0, 0)),
                  pl.BlockSpec((d, LANE), lambda i: (0, 0))],
        out_specs=[pl.BlockSpec((tm, d), lambda i: (i, 0)),
                   pl.BlockSpec((tm, LANE), lambda i: (i, 0)),
                   pl.BlockSpec((tm, LANE), lambda i: (i, 0)),
                   pl.BlockSpec((8, LANE), lambda i: (0, 0))],
        out_shape=[jax.ShapeDtypeStruct((m, d), F32),
                   jax.ShapeDtypeStruct((m, LANE), jnp.int32),
                   jax.ShapeDtypeStruct((m, LANE), F32),
                   jax.ShapeDtypeStruct((8, LANE), F32)],
        scratch_shapes=[pltpu.VMEM((1, LANE), F32)],
        compiler_params=_params(("arbitrary",)),
        name="moe_router",
    )(h, nw.reshape(1, d), wr)


def _dispatch_kernel(dest_ref, xs_ref, init_ref, out_ref, sem):
    del init_ref
    tm = xs_ref.shape[0]

    def copy(t, k):
        return pltpu.make_async_copy(xs_ref.at[pl.ds(t, 1), :],
                                     out_ref.at[pl.ds(dest_ref[2 * t + k], 1), :], sem)

    def start(t, c):
        copy(t, 0).start()
        copy(t, 1).start()
        return c

    def wait(t, c):
        copy(t, 0).wait()
        copy(t, 1).wait()
        return c

    lax.fori_loop(0, tm, start, 0)
    lax.fori_loop(0, tm, wait, 0)


def moe_dispatch(xs, dest, n_sorted, tm=256):
    m, d = xs.shape
    tm = _tile(m, tm)
    init = jnp.zeros((n_sorted, d), xs.dtype)
    return pl.pallas_call(
        _dispatch_kernel,
        grid=(m // tm,),
        in_specs=[pl.BlockSpec((2 * tm,), lambda i: (i,), memory_space=pltpu.SMEM),
                  pl.BlockSpec((tm, d), lambda i: (i, 0)),
                  pl.BlockSpec(memory_space=pl.ANY)],
        out_specs=pl.BlockSpec(memory_space=pl.ANY),
        out_shape=jax.ShapeDtypeStruct((n_sorted, d), xs.dtype),
        scratch_shapes=[pltpu.SemaphoreType.DMA(())],
        input_output_aliases={2: 0},
        compiler_params=_params(("arbitrary",)),
        name="moe_dispatch",
    )(dest, xs, init)


def _combine_kernel(dest_ref, ys_ref, h_ref, wts_ref, o_ref, y0_ref, y1_ref, sem):
    tm = h_ref.shape[0]

    def copy(t, k):
        dst = (y0_ref, y1_ref)[k]
        return pltpu.make_async_copy(ys_ref.at[pl.ds(dest_ref[2 * t + k], 1), :],
                                     dst.at[pl.ds(t, 1), :], sem)

    def start(t, c):
        copy(t, 0).start()
        copy(t, 1).start()
        return c

    def wait(t, c):
        copy(t, 0).wait()
        copy(t, 1).wait()
        return c

    lax.fori_loop(0, tm, start, 0)
    lax.fori_loop(0, tm, wait, 0)
    wts = wts_ref[...]
    o_ref[...] = h_ref[...] + wts[:, 0:1] * y0_ref[...] + wts[:, 1:2] * y1_ref[...]


def moe_combine(ysorted, dest, h, wts, tm=256):
    m, d = h.shape
    tm = _tile(m, tm)
    return pl.pallas_call(
        _combine_kernel,
        grid=(m // tm,),
        in_specs=[pl.BlockSpec((2 * tm,), lambda i: (i,), memory_space=pltpu.SMEM),
                  pl.BlockSpec(memory_space=pl.ANY),
                  pl.BlockSpec((tm, d), lambda i: (i, 0)),
                  pl.BlockSpec((tm, LANE), lambda i: (i, 0))],
        out_specs=pl.BlockSpec((tm, d), lambda i: (i, 0)),
        out_shape=jax.ShapeDtypeStruct((m, d), F32),
        scratch_shapes=[pltpu.VMEM((tm, d), F32), pltpu.VMEM((tm, d), F32),
                        pltpu.SemaphoreType.DMA(())],
        compiler_params=_params(("arbitrary",)),
        name="moe_combine",
    )(dest, ysorted, h, wts)


def moe_ffn(h, nw, router, wg, wu, wd, tmg=1024):
    m, d = h.shape
    tmg = min(tmg, m)
    xs, idx, wts, cnt = moe_router(h, nw, router)
    counts = cnt[0, :N_EXPERTS].astype(jnp.int32)
    padded = (counts + tmg - 1) // tmg * tmg
    ends = jnp.cumsum(padded)
    offsets = ends - padded
    n_tiles = (2 * m) // tmg + N_EXPERTS
    n_sorted = n_tiles * tmg
    dest = (offsets[idx[:, 0:2]] + idx[:, 2:4]).reshape(-1)
    tile_start = jnp.arange(n_tiles, dtype=jnp.int32) * tmg
    tile_expert = jnp.minimum(jnp.sum(tile_start[:, None] >= ends[None, :], axis=1), N_EXPERTS - 1)
    n_active = (ends[-1] // tmg).reshape(1).astype(jnp.int32)
    xsorted = moe_dispatch(xs, dest, n_sorted)
    ysorted = ffn(xsorted, nw, tile_expert.astype(jnp.int32), n_active, wg, wu, wd, tmg, 256, False, "grouped_ffn")
    return moe_combine(ysorted, dest, h, wts)


def _ple_kernel(h_ref, p_ref, nw_ref, wg_ref, wp_ref, fw_ref, o_ref, *, final):
    h = h_ref[...]
    gate = _sigmoid(jnp.dot(_rms(h, nw_ref[...]).astype(BF16), wg_ref[...], preferred_element_type=F32))
    out = h + jnp.dot(p_ref[...].astype(BF16), wp_ref[...], preferred_element_type=F32) * gate
    o_ref[...] = _rms(out, fw_ref[...]) if final else out


def ple(h, p, nw, wg, wp, fw, final, tm=512):
    m, d = h.shape
    pd = p.shape[1]
    tm = _tile(m, tm)
    return pl.pallas_call(
        functools.partial(_ple_kernel, final=final),
        grid=(m // tm,),
        in_specs=[pl.BlockSpec((tm, d), lambda i: (i, 0)),
                  pl.BlockSpec((tm, pd), lambda i: (i, 0)),
                  pl.BlockSpec((1, d), lambda i: (0, 0)),
                  pl.BlockSpec((d, d), lambda i: (0, 0)),
                  pl.BlockSpec((pd, d), lambda i: (0, 0)),
                  pl.BlockSpec((1, d), lambda i: (0, 0))],
        out_specs=pl.BlockSpec((tm, d), lambda i: (i, 0)),
        out_shape=jax.ShapeDtypeStruct((m, d), F32),
        compiler_params=_params(("parallel",)),
        name="ple",
    )(h, p, nw.reshape(1, d), wg, wp, fw.reshape(1, d))


def _split_w_in(w):
    d = w.shape[0]
    dn_w, diff_w, ml_w = DN_HEADS * HEAD_W, DIFF_HEADS * HEAD_W, ML_HEADS * HEAD_W
    g0 = 3 * dn_w + 2 * ml_w + dn_w
    g1 = g0 + 2 * DN_HEADS
    g2 = g1 + 3 * diff_w + 2 * ml_w
    g3 = g2 + 2 * ML_HEADS
    assert w.shape[1] == g3 and g0 + g2 - g1 == N_BIG
    big = jnp.concatenate([w[:, :g0], w[:, g1:g2]], axis=1).astype(BF16)
    small = jnp.concatenate([w[:, g0:g1], w[:, g2:g3],
                             jnp.zeros((d, LANE - (g1 - g0) - (g3 - g2)), w.dtype)], axis=1).astype(BF16)
    return big, small


def kernel(x, p, attn_norm, w_in, conv_dn, conv_ml, dn_a_log, dn_dt_bias, dn_norm, diff_lq1, diff_lk1, diff_lq2, diff_lk2, diff_norm, ml_i_bias, ml_f_bias, ml_norm, w_out, ffn_norm, dense_w_gate, dense_w_up, dense_w_down, router, moe_w_gate, moe_w_up, moe_w_down, ple_norm, ple_proj, ple_gate, final_norm):
    batch, seq, d = x.shape
    depth = w_in.shape[0]
    m = batch * seq
    h = x.reshape(m, d)
    dn_w, diff_w = DN_HEADS * HEAD_W, DIFF_HEADS * HEAD_W
    for i in range(depth):
        lambda_init = 0.8 - 0.6 * math.exp(-0.3 * i)
        w_big, w_small = _split_w_in(w_in[i])
        zb, zs = norm_proj(h, attn_norm[i], w_big, w_small)
        o_dn = deltanet(zb, zs, conv_dn[i], dn_a_log[i], dn_dt_bias[i], dn_norm[i], batch, seq)
        o_diff = diff_attention(zb, diff_lq1[i], diff_lk1[i], diff_lq2[i], diff_lk2[i], diff_norm[i],
                                lambda_init, batch, seq)
        o_ml = mlstm(zb, zs, conv_ml[i], ml_i_bias[i], ml_f_bias[i], ml_norm[i], batch, seq)
        wo = w_out[i].astype(BF16)
        h = out_proj(o_dn, o_diff, o_ml, wo[:dn_w], wo[dn_w:dn_w + diff_w], wo[dn_w + diff_w:], h)
        j = i // 2
        if i % 2 == 0:
            h = dense_ffn(h, ffn_norm[i], dense_w_gate[j], dense_w_up[j], dense_w_down[j])
        else:
            h = moe_ffn(h, ffn_norm[i], router[j], moe_w_gate[j], moe_w_up[j], moe_w_down[j])
        h = ple(h, p[i].reshape(m, -1), ple_norm[i], ple_gate[i].astype(BF16), ple_proj[i].astype(BF16),
                final_norm, final=(i == depth - 1))
    return h.reshape(batch, seq, d)
```

```python
import functools
import math

import numpy as np
import jax
import jax.numpy as jnp
from jax import lax
from jax.experimental import pallas as pl
from jax.experimental.pallas import tpu as pltpu

F32 = jnp.float32
BF16 = jnp.bfloat16
EPS = 1e-6
LANE = 128
NEG = -1e30

DN_HEADS = 4
DIFF_HEADS = 8
ML_HEADS = 4
HEAD_W = 128
DIFF_DH = 64
CHUNK = 64
CONV_K = 4
N_EXPERTS = 8
VMEM_LIMIT = 56 * 1024 * 1024

CB_DN_Q, CB_DN_K, CB_DN_V = 0, 4, 8
CB_ML_Q, CB_ML_K = 12, 16
CB_DN_Z = 20
CB_AQ, CB_AK, CB_AV = 24, 32, 40
CB_ML_V, CB_ML_O = 48, 52
N_BIG = 56 * LANE
SL_DN_B, SL_DN_A, SL_ML_I, SL_ML_F = 0, 4, 8, 12


def _params(sem):
    return pltpu.CompilerParams(dimension_semantics=sem, vmem_limit_bytes=VMEM_LIMIT)


def _dot(a, b):
    return jnp.dot(a.astype(BF16), b.astype(BF16), preferred_element_type=F32)


def _dot_nt(a, b):
    return lax.dot_general(a.astype(BF16), b.astype(BF16), (((1,), (1,)), ((), ())),
                           preferred_element_type=F32)


def _dot_tn(a, b):
    return lax.dot_general(a.astype(BF16), b.astype(BF16), (((0,), (0,)), ((), ())),
                           preferred_element_type=F32)


def _dot_hi(a, b):
    return jnp.dot(a, b, preferred_element_type=F32, precision=lax.Precision.HIGHEST)


def _rms(x, w):
    return x * lax.rsqrt(jnp.mean(x * x, axis=-1, keepdims=True) + EPS) * w


def _sigmoid(x):
    return 1.0 / (1.0 + jnp.exp(-x))


def _silu(x):
    return x * _sigmoid(x)


def _softplus(x):
    return jnp.maximum(x, 0.0) + jnp.log(1.0 + jnp.exp(-jnp.abs(x)))


def _log_sigmoid(x):
    return -_softplus(-x)


def _tile(m, t):
    t = min(m, t)
    assert m % t == 0
    return t


def _norm_proj_kernel(x_ref, nw_ref, w_ref, ws_ref, o_ref, os_ref, xs_ref):
    @pl.when(pl.program_id(1) == 0)
    def _():
        xs_ref[...] = _rms(x_ref[...], nw_ref[...]).astype(BF16)
        os_ref[...] = jnp.dot(xs_ref[...], ws_ref[...], preferred_element_type=F32)

    o_ref[...] = jnp.dot(xs_ref[...], w_ref[...], preferred_element_type=F32).astype(o_ref.dtype)


def norm_proj(x, nw, w_big, w_small, tm=1024, tn=1792):
    m, d = x.shape
    n = w_big.shape[1]
    tm, tn = _tile(m, tm), _tile(n, tn)
    return pl.pallas_call(
        _norm_proj_kernel,
        grid=(m // tm, n // tn),
        in_specs=[pl.BlockSpec((tm, d), lambda i, j: (i, 0)),
                  pl.BlockSpec((1, d), lambda i, j: (0, 0)),
                  pl.BlockSpec((d, tn), lambda i, j: (0, j)),
                  pl.BlockSpec((d, LANE), lambda i, j: (0, 0))],
        out_specs=[pl.BlockSpec((tm, tn), lambda i, j: (i, j)),
                   pl.BlockSpec((tm, LANE), lambda i, j: (i, 0))],
        out_shape=[jax.ShapeDtypeStruct((m, n), BF16), jax.ShapeDtypeStruct((m, LANE), F32)],
        scratch_shapes=[pltpu.VMEM((tm, d), BF16)],
        compiler_params=_params(("parallel", "arbitrary")),
        name="norm_proj",
    )(x, nw.reshape(1, d), w_big, w_small)


def _chunk_conv_silu(ref, w, ci, rows):
    cur = ref[rows, :].astype(F32)
    prev_rows = pl.ds(pl.multiple_of(jnp.maximum(ci * CHUNK - 16, 0), 16), 16)
    prev = jnp.where(ci > 0, ref[prev_rows, :].astype(F32)[8:16, :], 0.0)
    ext = jnp.concatenate([prev, cur], axis=0)
    y = cur * w[CONV_K - 1:CONV_K, :]
    for back in range(1, CONV_K):
        y = y + pltpu.roll(ext, back, axis=0)[8:8 + CHUNK, :] * w[CONV_K - 1 - back:CONV_K - back, :]
    return _silu(y)


def _dot2(l_bf16, x):
    hi = x.astype(BF16)
    lo = (x - hi.astype(F32)).astype(BF16)
    return (jnp.dot(l_bf16, hi, preferred_element_type=F32)
            + jnp.dot(l_bf16, lo, preferred_element_type=F32))


def _lane_rows(*rows):
    out = jnp.zeros((8, LANE), F32)
    for r, (lane0, vals) in enumerate(rows):
        out = out.at[r, lane0:lane0 + vals.shape[0]].set(vals.astype(F32))
    return out


def _chunk_masks():
    r = lax.broadcasted_iota(jnp.int32, (CHUNK, CHUNK), 0)
    c = lax.broadcasted_iota(jnp.int32, (CHUNK, CHUNK), 1)
    return r, c


def _chunk_rows(ci):
    return pl.ds(pl.multiple_of(ci * CHUNK, CHUNK), CHUNK)


assert DN_HEADS == ML_HEADS
HEADS = range(DN_HEADS)
COLS = [slice(h * HEAD_W, (h + 1) * HEAD_W) for h in HEADS]
UNROLL = 2


def _deltanet_kernel(gp_ref, q_ref, k_ref, v_ref, zg_ref, zs_ref, cw_ref, nw_ref, o_ref,
                     u_s, w_s, at_s, qg_s, kg_s, gl_s):
    n_chunks = q_ref.shape[0] // CHUNK
    width = DN_HEADS * HEAD_W
    r, c = _chunk_masks()
    low_incl = (r >= c).astype(BF16)
    strict_up = (r > c).astype(F32)
    tri_incl = r >= c
    tri_strict = r > c
    cw = cw_ref[...]
    rate_row = jnp.exp(gp_ref[0:1, :])
    bias_row = gp_ref[1:2, :]

    def prep(cp, carry):
        units, qc, kc, kb, kcb, vbeta, g = [], [], [], [], [], [], []
        for t in range(UNROLL):
            ci = UNROLL * cp + t
            rows = _chunk_rows(ci)
            qa = _chunk_conv_silu(q_ref, cw[:, :width], ci, rows)
            ka = _chunk_conv_silu(k_ref, cw[:, width:2 * width], ci, rows)
            va = _chunk_conv_silu(v_ref, cw[:, 2 * width:], ci, rows)
            zs = zs_ref[rows, :]
            sig = _sigmoid(zs)
            g_all = -rate_row * _softplus(zs + bias_row)
            for h in HEADS:
                units.append((ci, rows, h))
                qh, kh = qa[:, COLS[h]], ka[:, COLS[h]]
                qc.append(qh * (lax.rsqrt(jnp.sum(qh * qh, axis=-1, keepdims=True) + EPS) * (HEAD_W ** -0.5)))
                kc.append(kh * lax.rsqrt(jnp.sum(kh * kh, axis=-1, keepdims=True) + EPS))
                beta = sig[:, SL_DN_B + h:SL_DN_B + h + 1]
                g.append(g_all[:, SL_DN_A + h:SL_DN_A + h + 1])
                kb.append(kc[-1] * beta)
                kcb.append(kc[-1].astype(BF16))
                vbeta.append(va[:, COLS[h]] * beta)
        us = range(len(units))
        dmat = [_dot2(low_incl, jnp.broadcast_to(g[u], (CHUNK, CHUNK)) * strict_up) for u in us]
        kk = [_dot_nt(kb[u], kcb[u]) for u in us]
        qk = [_dot_nt(qc[u], kcb[u]) for u in us]
        gc = [dmat[u][:, :1] + g[u][:1, :] for u in us]
        decay = [jnp.where(tri_incl, jnp.exp(dmat[u]), 0.0) for u in us]
        eg = [jnp.exp(gc[u]) for u in us]
        a = [jnp.where(tri_strict, kk[u] * decay[u], 0.0).astype(BF16) for u in us]
        x = [jnp.concatenate([vbeta[u], kb[u] * eg[u]], axis=1) for u in us]
        ax = [_dot(a[u], x[u]) for u in us]
        p = [_dot(a[u], a[u]).astype(BF16) for u in us]
        x = [x[u] - ax[u] for u in us]
        for step in range(5):
            px = [_dot(p[u], x[u]) for u in us]
            if step < 4:
                p = [_dot(p[u], p[u]).astype(BF16) for u in us]
            x = [x[u] + px[u] for u in us]
        for u, (ci, rows, h) in enumerate(units):
            at_s[rows, h * HEAD_W:h * HEAD_W + CHUNK] = (qk[u] * decay[u]).astype(BF16)
            u_s[rows, COLS[h]] = x[u][:, :HEAD_W]
            w_s[rows, COLS[h]] = x[u][:, HEAD_W:].astype(BF16)
            g_last = gc[u][CHUNK - 1:CHUNK, :]
            qg_s[rows, COLS[h]] = (qc[u] * eg[u]).astype(BF16)
            kg_s[rows, COLS[h]] = (kc[u] * jnp.exp(g_last - gc[u])).astype(BF16)
            gl_s[pl.ds(pl.multiple_of(ci * 8, 8), 8), COLS[h]] = jnp.broadcast_to(jnp.exp(g_last), (8, HEAD_W))
        return carry

    assert n_chunks % UNROLL == 0
    lax.fori_loop(0, n_chunks // UNROLL, prep, 0)

    nw = nw_ref[...]

    def scan(ci, states):
        rows = _chunk_rows(ci)
        sb = [states[h].astype(BF16) for h in HEADS]
        ws = [jnp.dot(w_s[rows, COLS[h]], sb[h], preferred_element_type=F32) for h in HEADS]
        qs = [jnp.dot(qg_s[rows, COLS[h]], sb[h], preferred_element_type=F32) for h in HEADS]
        vb = [(u_s[rows, COLS[h]] - ws[h]).astype(BF16) for h in HEADS]
        av = [jnp.dot(at_s[rows, h * HEAD_W:h * HEAD_W + CHUNK], vb[h], preferred_element_type=F32)
              for h in HEADS]
        kv = [_dot_tn(kg_s[rows, COLS[h]], vb[h]) for h in HEADS]
        out = []
        for h in HEADS:
            gl = gl_s[pl.ds(pl.multiple_of(ci * 8, 8), 1), COLS[h]]
            out.append(states[h] * gl + kv[h])
            o_ref[rows, COLS[h]] = (_rms(qs[h] + av[h], nw)
                                    * _silu(zg_ref[rows, COLS[h]].astype(F32))).astype(o_ref.dtype)
        return tuple(out)

    lax.fori_loop(0, n_chunks, scan, tuple(jnp.zeros((HEAD_W, HEAD_W), F32) for _ in range(DN_HEADS)))


def deltanet(zb, zs, conv_dn, a_log, dt_bias, dn_norm, batch, seq):
    m = zb.shape[0]
    width = DN_HEADS * HEAD_W
    blk = lambda cb: pl.BlockSpec((seq, width), lambda b: (b, cb // DN_HEADS))
    return pl.pallas_call(
        _deltanet_kernel,
        grid=(batch,),
        in_specs=[pl.BlockSpec((8, LANE), lambda b: (0, 0)),
                  blk(CB_DN_Q), blk(CB_DN_K), blk(CB_DN_V), blk(CB_DN_Z),
                  pl.BlockSpec((seq, LANE), lambda b: (b, 0)),
                  pl.BlockSpec((CONV_K, 3 * width), lambda b: (0, 0)),
                  pl.BlockSpec((1, HEAD_W), lambda b: (0, 0))],
        out_specs=pl.BlockSpec((seq, width), lambda b: (b, 0)),
        out_shape=jax.ShapeDtypeStruct((m, width), BF16),
        scratch_shapes=[pltpu.VMEM((seq, width), F32), pltpu.VMEM((seq, width), BF16),
                        pltpu.VMEM((seq, width), BF16), pltpu.VMEM((seq, width), BF16),
                        pltpu.VMEM((seq, width), BF16),
                        pltpu.VMEM((seq // CHUNK * 8, width), F32)],
        compiler_params=_params(("parallel",)),
        name="deltanet",
    )(_lane_rows((SL_DN_A, a_log), (SL_DN_A, dt_bias)), zb, zb, zb, zb, zs, conv_dn,
      dn_norm.reshape(1, HEAD_W))


def _mlstm_kernel(gp_ref, q_ref, k_ref, v_ref, og_ref, zs_ref, cw_ref, nw_ref, o_ref):
    n_chunks = q_ref.shape[0] // CHUNK
    width = ML_HEADS * HEAD_W
    r, c = _chunk_masks()
    low_incl = (r >= c).astype(BF16)
    strict_up = (r > c).astype(F32)
    eye = (r == c).astype(F32)
    tri_incl = r >= c
    nw = nw_ref[...]
    cw = cw_ref[...]

    def local(ci):
        rows = _chunk_rows(ci)
        qa = _chunk_conv_silu(q_ref, cw[:, :width], ci, rows)
        ka = _chunk_conv_silu(k_ref, cw[:, width:], ci, rows) * (HEAD_W ** -0.5)
        zs = zs_ref[rows, :]
        ip_all = zs + gp_ref[0:1, :]
        lf_all = _log_sigmoid(zs + gp_ref[1:2, :])
        qc = [qa[:, COLS[h]] for h in HEADS]
        kc = [ka[:, COLS[h]] for h in HEADS]
        ip = [ip_all[:, SL_ML_I + h:SL_ML_I + h + 1] for h in HEADS]
        lf = [lf_all[:, SL_ML_F + h:SL_ML_F + h + 1] for h in HEADS]
        dl = [_dot2(low_incl, jnp.broadcast_to(lf[h], (CHUNK, CHUNK)) * strict_up
                    + jnp.broadcast_to(ip[h], (CHUNK, CHUNK)) * eye) for h in HEADS]
        qk = [_dot_nt(qc[h], kc[h]) for h in HEADS]
        heads = []
        for h in HEADS:
            b = dl[h][:, :1] - ip[h][:1, :] + lf[h][:1, :]
            dlog = jnp.where(tri_incl, dl[h], NEG)
            b_last = b[CHUNK - 1:CHUNK, :]
            a = b_last - b + ip[h]
            heads.append(dict(qc=qc[h], kc=kc[h], qk=qk[h], b=b, dlog=dlog, b_last=b_last, a=a,
                              dmax=jnp.max(dlog, axis=-1, keepdims=True),
                              a_max=jnp.max(a, axis=0, keepdims=True)))
        return rows, heads

    def advance(rows, heads, carry):
        vc = [v_ref[rows, COLS[h]] for h in HEADS]
        s, wk, inter, m_t, m_new, scale = [], [], [], [], [], []
        for h in HEADS:
            t = heads[h]
            m_st = carry[h][2]
            m_t.append(jnp.maximum(t["dmax"], t["b"] + m_st))
            s.append(t["qk"] * jnp.exp(t["dlog"] - m_t[h]))
            inter.append(jnp.exp(t["b"] + m_st - m_t[h]))
            m_new.append(jnp.maximum(t["b_last"] + m_st, t["a_max"]))
            scale.append(jnp.exp(t["b_last"] + m_st - m_new[h]))
            wk.append(t["kc"] * jnp.exp(t["a"] - m_new[h]))
        sv = [_dot(s[h], vc[h]) for h in HEADS]
        kv = [_dot_tn(wk[h], vc[h]) for h in HEADS]
        qcs = [_dot(heads[h]["qc"], carry[h][0]) for h in HEADS]
        out = []
        for h in HEADS:
            c_st, n_st, _ = carry[h]
            qc = heads[h]["qc"]
            num = sv[h] + inter[h] * qcs[h]
            den = (jnp.sum(s[h], axis=-1, keepdims=True)
                   + inter[h] * jnp.sum(qc * n_st, axis=-1, keepdims=True))
            hh = num / jnp.maximum(jnp.abs(den), jnp.exp(-m_t[h]))
            out.append((scale[h] * c_st + kv[h],
                        scale[h] * n_st + jnp.sum(wk[h], axis=0, keepdims=True),
                        m_new[h]))
            o_ref[rows, COLS[h]] = (_sigmoid(og_ref[rows, COLS[h]].astype(F32))
                                    * _rms(hh, nw)).astype(o_ref.dtype)
        return tuple(out)

    init = tuple((jnp.zeros((HEAD_W, HEAD_W), F32), jnp.zeros((1, HEAD_W), F32), jnp.zeros((1, 1), F32))
                 for _ in range(ML_HEADS))
    lax.fori_loop(0, n_chunks, lambda ci, carry: advance(*local(ci), carry), init)


def mlstm(zb, zs, conv_ml, i_bias, f_bias, ml_norm, batch, seq):
    m = zb.shape[0]
    width = ML_HEADS * HEAD_W
    blk = lambda cb: pl.BlockSpec((seq, width), lambda b: (b, cb // ML_HEADS))
    return pl.pallas_call(
        _mlstm_kernel,
        grid=(batch,),
        in_specs=[pl.BlockSpec((8, LANE), lambda b: (0, 0)),
                  blk(CB_ML_Q), blk(CB_ML_K), blk(CB_ML_V), blk(CB_ML_O),
                  pl.BlockSpec((seq, LANE), lambda b: (b, 0)),
                  pl.BlockSpec((CONV_K, 2 * width), lambda b: (0, 0)),
                  pl.BlockSpec((1, HEAD_W), lambda b: (0, 0))],
        out_specs=pl.BlockSpec((seq, width), lambda b: (b, 0)),
        out_shape=jax.ShapeDtypeStruct((m, width), BF16),
        compiler_params=_params(("parallel",)),
        name="mlstm",
    )(_lane_rows((SL_ML_I, i_bias), (SL_ML_F, f_bias)), zb, zb, zb, zb, zs, conv_ml,
      ml_norm.reshape(1, HEAD_W))


LOG2E = math.log2(math.e)
LOG2E_HI = float(np.asarray(LOG2E, dtype=BF16))
LOG2E_LO = LOG2E - LOG2E_HI
SLAB = 128
RAMP = 256


def _with_lanes(x, base, vals):
    lane = lax.broadcasted_iota(jnp.int32, x.shape, 1)
    for i, v in enumerate(vals):
        x = jnp.where(lane == base + i, v, x)
    return x


def _diff_attn_kernel(lq1_ref, lk1_ref, lq2_ref, lk2_ref, q_ref, k_ref, v_ref, nw_ref, o_ref,
                      k0_s, k1_s, *, tq, lambda_init):
    h = pl.program_id(1)
    qi = pl.program_id(2)
    tk = RAMP
    per_pair = tq // (2 * tk)
    assert k_ref.shape[0] <= RAMP * RAMP and tq % (2 * tk) == 0
    slope = jnp.exp2(jnp.full((1, 1), -(h + 1).astype(F32) * (8.0 / DIFF_HEADS), F32))

    @pl.when(qi == 0)
    def _():
        kk = k_ref[...]
        pos = lax.broadcasted_iota(jnp.int32, kk.shape, 0)
        fine = ((pos & (RAMP - 1)).astype(F32) * slope).astype(BF16)
        coarse = ((pos - (pos & (RAMP - 1))).astype(F32) * slope).astype(BF16)
        k0_s[...] = _with_lanes(kk, DIFF_DH, (fine, fine, coarse, coarse))
        k1_s[...] = _with_lanes(kk, 0, (fine, fine, coarse, coarse))

    lam = (jnp.exp(jnp.sum(lq1_ref[...] * lk1_ref[...], axis=-1, keepdims=True))
           - jnp.exp(jnp.sum(lq2_ref[...] * lk2_ref[...], axis=-1, keepdims=True)) + lambda_init)
    lane = lax.broadcasted_iota(jnp.int32, (tq, HEAD_W), 1)
    q = q_ref[...].astype(F32) * (DIFF_DH ** -0.5 * LOG2E)
    l2e = (LOG2E_HI, LOG2E_LO, LOG2E_HI, LOG2E_LO)
    q0 = _with_lanes(jnp.where(lane < DIFF_DH, q, 0.0), DIFF_DH, l2e).astype(BF16)
    q1 = _with_lanes(jnp.where(lane >= DIFF_DH, q, 0.0), 0, l2e).astype(BF16)
    per_map = tq // SLAB
    n_slab = 2 * per_map
    q_slab = [qm[j * SLAB:(j + 1) * SLAB, :] for qm in (q0, q1) for j in range(per_map)]

    def run_pair(kp, diag_at, carry):
        def needed(i, blk):
            return diag_at is None or diag_at + blk * tk <= (i % per_map) * SLAB + SLAB - 1

        scores, values = [], []
        for blk in range(2):
            rows = pl.ds(pl.multiple_of((2 * kp + blk) * tk, tk), tk)
            kb = (k0_s[rows, :], k1_s[rows, :])
            values.append(v_ref[rows, :])
            scores.append([_dot_nt(q_slab[i], kb[i // per_map]) if needed(i, blk) else None
                           for i in range(n_slab)])
        for blk in range(2):
            out = []
            for i in range(n_slab):
                if not needed(i, blk):
                    out.append(carry[i])
                    continue
                m_i, l_i, acc = carry[i]
                s = scores[blk][i]
                if diag_at is not None and diag_at + (blk + 1) * tk - 1 > (i % per_map) * SLAB:
                    ri = lax.broadcasted_iota(jnp.int32, (SLAB, tk), 0) + (i % per_map) * SLAB
                    ci = lax.broadcasted_iota(jnp.int32, (SLAB, tk), 1) + (diag_at + blk * tk)
                    s = jnp.where(ci <= ri, s, NEG)
                m_new = jnp.maximum(m_i, jnp.max(s, axis=-1, keepdims=True))
                p = jnp.exp2(s - m_new)
                alpha = jnp.exp2(m_i - m_new)
                l_new = alpha * l_i + sum(p[:, j * LANE:(j + 1) * LANE] for j in range(tk // LANE))
                acc_new = alpha * acc + jnp.dot(p.astype(BF16), values[blk], preferred_element_type=F32)
                out.append((m_new, l_new, acc_new))
            carry = tuple(out)
        return carry

    init = tuple((jnp.full((SLAB, 1), NEG, F32), jnp.zeros((SLAB, LANE), F32), jnp.zeros((SLAB, HEAD_W), F32))
                 for _ in range(n_slab))
    carry = lax.fori_loop(0, qi * per_pair, lambda kp, cr: run_pair(kp, None, cr), init)
    for t in range(per_pair):
        carry = run_pair(qi * per_pair + t, t * 2 * tk, carry)
    nw = nw_ref[...]
    for j in range(per_map):
        (_, l0, a0), (_, l1, a1) = carry[j], carry[per_map + j]
        l0 = jnp.sum(l0, axis=-1, keepdims=True)
        l1 = jnp.sum(l1, axis=-1, keepdims=True)
        o = a0 / l0 - lam * (a1 / l1)
        o_ref[j * SLAB:(j + 1) * SLAB, :] = (_rms(o, nw) * (1.0 - lambda_init)).astype(o_ref.dtype)


def diff_attention(zb, lq1, lk1, lq2, lk2, diff_norm, lambda_init, batch, seq, tq=512):
    m = zb.shape[0]
    tq = _tile(seq, tq)
    nq = seq // tq
    vec = pl.BlockSpec((1, DIFF_DH), lambda b, h, i: (0, 0))
    return pl.pallas_call(
        functools.partial(_diff_attn_kernel, tq=tq, lambda_init=lambda_init),
        grid=(batch, DIFF_HEADS, nq),
        in_specs=[vec, vec, vec, vec,
                  pl.BlockSpec((tq, HEAD_W), lambda b, h, i: (b * nq + i, CB_AQ + h)),
                  pl.BlockSpec((seq, HEAD_W), lambda b, h, i: (b, CB_AK + h)),
                  pl.BlockSpec((seq, HEAD_W), lambda b, h, i: (b, CB_AV + h)),
                  pl.BlockSpec((1, HEAD_W), lambda b, h, i: (0, 0))],
        out_specs=pl.BlockSpec((tq, HEAD_W), lambda b, h, i: (b * nq + i, h)),
        out_shape=jax.ShapeDtypeStruct((m, DIFF_HEADS * HEAD_W), BF16),
        scratch_shapes=[pltpu.VMEM((seq, HEAD_W), BF16), pltpu.VMEM((seq, HEAD_W), BF16)],
        compiler_params=_params(("parallel", "parallel", "arbitrary")),
        name="diff_attn",
    )(lq1.reshape(1, -1), lk1.reshape(1, -1), lq2.reshape(1, -1), lk2.reshape(1, -1),
      zb, zb, zb, diff_norm.reshape(1, HEAD_W))


def _out_proj_kernel(a_ref, b_ref, c_ref, wa_ref, wb_ref, wc_ref, h_ref, o_ref):
    acc = jnp.dot(a_ref[...], wa_ref[...], preferred_element_type=F32)
    acc += jnp.dot(b_ref[...], wb_ref[...], preferred_element_type=F32)
    acc += jnp.dot(c_ref[...], wc_ref[...], preferred_element_type=F32)
    o_ref[...] = h_ref[...] + acc


def out_proj(o_dn, o_diff, o_ml, wa, wb, wc, h, tm=512):
    m, d = h.shape
    tm = _tile(m, tm)
    row = lambda w: pl.BlockSpec((tm, w), lambda i: (i, 0))
    full = lambda w: pl.BlockSpec((w, d), lambda i: (0, 0))
    return pl.pallas_call(
        _out_proj_kernel,
        grid=(m // tm,),
        in_specs=[row(o_dn.shape[1]), row(o_diff.shape[1]), row(o_ml.shape[1]),
                  full(wa.shape[0]), full(wb.shape[0]), full(wc.shape[0]), row(d)],
        out_specs=row(d),
        out_shape=jax.ShapeDtypeStruct((m, d), F32),
        compiler_params=_params(("parallel",)),
        name="out_proj",
    )(o_dn, o_diff, o_ml, wa, wb, wc, h)


RING = 2


def _ffn_kernel(te_ref, na_ref, x_ref, nw_ref, wg_hbm, wu_hbm, wd_hbm, o_ref, xs_ref, wg_b, wu_b, wd_b, sems,
                *, fuse_norm):
    i, f = pl.program_id(0), pl.program_id(1)
    nf = pl.num_programs(1)
    tf = wg_b.shape[2]
    step = i * nf + f
    total = na_ref[0] * nf

    def copies(s):
        e = te_ref[s // nf]
        cols = pl.ds(pl.multiple_of((s % nf) * tf, tf), tf)
        slot = s % RING
        return (pltpu.make_async_copy(wg_hbm.at[e, :, cols], wg_b.at[slot], sems.at[slot, 0]),
                pltpu.make_async_copy(wu_hbm.at[e, :, cols], wu_b.at[slot], sems.at[slot, 1]),
                pltpu.make_async_copy(wd_hbm.at[e, cols, :], wd_b.at[slot], sems.at[slot, 2]))

    def start(s):
        @pl.when(s < total)
        def _():
            for c in copies(s):
                c.start()

    @pl.when(step == 0)
    def _():
        for s in range(RING - 1):
            start(s)

    @pl.when(f == 0)
    def _():
        x = x_ref[...]
        if fuse_norm:
            xs_ref[...] = _rms(x, nw_ref[...]).astype(BF16)
            o_ref[...] = x
        else:
            xs_ref[...] = x.astype(BF16)
            o_ref[...] = jnp.zeros_like(o_ref)

    @pl.when(i < na_ref[0])
    def _():
        start(step + RING - 1)
        for c in copies(step):
            c.wait()
        slot = step % RING
        xs = xs_ref[...]
        g = jnp.dot(xs, wg_b[slot].astype(BF16), preferred_element_type=F32)
        u = jnp.dot(xs, wu_b[slot].astype(BF16), preferred_element_type=F32)
        o_ref[...] += jnp.dot((_silu(g) * u).astype(BF16), wd_b[slot].astype(BF16), preferred_element_type=F32)


def ffn(x, nw, tile_expert, n_active, wg, wu, wd, tm, tf, fuse_norm, name):
    rows, d = x.shape
    dff = wg.shape[2]
    tf = _tile(dff, tf)
    any_spec = pl.BlockSpec(memory_space=pl.ANY)
    return pl.pallas_call(
        functools.partial(_ffn_kernel, fuse_norm=fuse_norm),
        grid_spec=pltpu.PrefetchScalarGridSpec(
            num_scalar_prefetch=2,
            grid=(rows // tm, dff // tf),
            in_specs=[pl.BlockSpec((tm, d), lambda i, f, te, na: (i, 0), pipeline_mode=pl.Buffered(1)),
                      pl.BlockSpec((1, d), lambda i, f, te, na: (0, 0)),
                      any_spec, any_spec, any_spec],
            out_specs=pl.BlockSpec((tm, d), lambda i, f, te, na: (i, 0), pipeline_mode=pl.Buffered(1)),
            scratch_shapes=[pltpu.VMEM((tm, d), BF16),
                            pltpu.VMEM((RING, d, tf), F32), pltpu.VMEM((RING, d, tf), F32),
                            pltpu.VMEM((RING, tf, d), F32), pltpu.SemaphoreType.DMA((RING, 3))]),
        out_shape=jax.ShapeDtypeStruct((rows, d), F32),
        compiler_params=_params(("arbitrary", "arbitrary")),
        name=name,
    )(tile_expert, n_active, x, nw.reshape(1, d), wg, wu, wd)


def dense_ffn(h, nw, wg, wu, wd, tm=1024, tf=512):
    tm = _tile(h.shape[0], tm)
    n_tiles = h.shape[0] // tm
    return ffn(h, nw, jnp.zeros((n_tiles,), jnp.int32), jnp.full((1,), n_tiles, jnp.int32),
               wg[None], wu[None], wd[None], tm, tf, True, "dense_ffn")


def _router_kernel(h_ref, nw_ref, wr_ref, xs_ref, idx_ref, wts_ref, cnt_ref, carry_ref):
    i = pl.program_id(0)
    tm = h_ref.shape[0]

    @pl.when(i == 0)
    def _():
        carry_ref[...] = jnp.zeros_like(carry_ref)

    xn = _rms(h_ref[...], nw_ref[...])
    xs_ref[...] = xn
    logits = _dot_hi(xn, wr_ref[...])
    lane = lax.broadcasted_iota(jnp.int32, logits.shape, 1)
    logits = jnp.where(lane < N_EXPERTS, logits, NEG)
    m1 = jnp.max(logits, axis=-1, keepdims=True)
    e1 = jnp.min(jnp.where(logits == m1, lane, LANE), axis=-1, keepdims=True)
    rest = jnp.where(lane == e1, NEG, logits)
    m2 = jnp.max(rest, axis=-1, keepdims=True)
    e2 = jnp.min(jnp.where(rest == m2, lane, LANE), axis=-1, keepdims=True)
    ex = jnp.exp(m2 - m1)
    w1 = 1.0 / (1.0 + ex)
    w2 = ex * w1
    oh1 = (lane == e1).astype(F32)
    oh2 = (lane == e2).astype(F32)
    oh = oh1 + oh2
    r = lax.broadcasted_iota(jnp.int32, (tm, tm), 0)
    c = lax.broadcasted_iota(jnp.int32, (tm, tm), 1)
    before = jnp.dot((r > c).astype(BF16), oh.astype(BF16), preferred_element_type=F32) + carry_ref[...]
    rank1 = jnp.sum(before * oh1, axis=-1, keepdims=True)
    rank2 = jnp.sum(before * oh2, axis=-1, keepdims=True)
    carry_ref[...] += jnp.sum(oh, axis=0, keepdims=True)
    cnt_ref[...] = jnp.broadcast_to(carry_ref[...], cnt_ref.shape)
    idx = jnp.where(lane == 0, e1, jnp.where(lane == 1, e2, 0))
    idx = jnp.where(lane == 2, rank1.astype(jnp.int32), jnp.where(lane == 3, rank2.astype(jnp.int32), idx))
    idx_ref[...] = idx
    wts_ref[...] = jnp.where(lane == 0, w1, jnp.where(lane == 1, w2, 0.0))


def moe_router(h, nw, router, tm=512):
    m, d = h.shape
    tm = _tile(m, tm)
    wr = jnp.zeros((d, LANE), F32).at[:, :N_EXPERTS].set(router)
    return pl.pallas_call(
        _router_kernel,
        grid=(m // tm,),
        in_specs=[pl.BlockSpec((tm, d), lambda i: (i, 0)),
                  pl.BlockSpec((1, d), lambda i: (0, 0)),
                  pl.BlockSpec((d, LANE), lambda i: (0, 0))],
        out_specs=[pl.BlockSpec((tm, d), lambda i: (i, 0)),
                   pl.BlockSpec((tm, LANE), lambda i: (i, 0)),
                   pl.BlockSpec((tm, LANE), lambda i: (i, 0)),
                   pl.BlockSpec((8, LANE), lambda i: (0, 0))],
        out_shape=[jax.ShapeDtypeStruct((m, d), F32),
                   jax.ShapeDtypeStruct((m, LANE), jnp.int32),
                   jax.ShapeDtypeStruct((m, LANE), F32),
                   jax.ShapeDtypeStruct((8, LANE), F32)],
        scratch_shapes=[pltpu.VMEM((1, LANE), F32)],
        compiler_params=_params(("arbitrary",)),
        name="moe_router",
    )(h, nw.reshape(1, d), wr)


ROW_UNROLL = 8


def _dispatch_kernel(dest_ref, xs_ref, init_ref, out_ref, sems):
    del init_ref
    tm = xs_ref.shape[0]

    def start(t, c):
        for k in range(2):
            pltpu.make_async_copy(xs_ref.at[pl.ds(t, 1), :], out_ref.at[pl.ds(dest_ref[2 * t + k], 1), :],
                                  sems.at[k]).start(priority=k)
        return c

    lax.fori_loop(0, tm, start, 0, unroll=ROW_UNROLL)
    for k in range(2):
        pltpu.make_async_copy(xs_ref, out_ref.at[pl.ds(0, tm), :], sems.at[k]).wait()


def moe_dispatch(xs, dest, n_sorted, tm=256):
    m, d = xs.shape
    tm = _tile(m, tm)
    init = jnp.zeros((n_sorted, d), xs.dtype)
    return pl.pallas_call(
        _dispatch_kernel,
        grid=(m // tm,),
        in_specs=[pl.BlockSpec((2 * tm,), lambda i: (i,), memory_space=pltpu.SMEM),
                  pl.BlockSpec((tm, d), lambda i: (i, 0)),
                  pl.BlockSpec(memory_space=pl.ANY)],
        out_specs=pl.BlockSpec(memory_space=pl.ANY),
        out_shape=jax.ShapeDtypeStruct((n_sorted, d), xs.dtype),
        scratch_shapes=[pltpu.SemaphoreType.DMA((2,))],
        input_output_aliases={2: 0},
        compiler_params=_params(("arbitrary",)),
        name="moe_dispatch",
    )(dest, xs, init)


def _combine_kernel(dest_ref, ys_ref, h_ref, wts_ref, o_ref, y_s, sems):
    i = pl.program_id(0)
    tm = h_ref.shape[0]

    def gather(tile, slot):
        def start(t, c):
            for k in range(2):
                src = dest_ref[2 * (tile * tm + t) + k]
                pltpu.make_async_copy(ys_ref.at[pl.ds(src, 1), :], y_s.at[slot, k, pl.ds(t, 1), :],
                                      sems.at[slot, k]).start(priority=k)
            return c

        lax.fori_loop(0, tm, start, 0, unroll=ROW_UNROLL)

    @pl.when(i == 0)
    def _():
        gather(0, 0)

    @pl.when(i + 1 < pl.num_programs(0))
    def _():
        gather(i + 1, (i + 1) % 2)

    slot = i % 2
    for k in range(2):
        pltpu.make_async_copy(ys_ref.at[pl.ds(0, tm), :], y_s.at[slot, k], sems.at[slot, k]).wait()
    wts = wts_ref[...]
    o_ref[...] = h_ref[...] + wts[:, 0:1] * y_s[slot, 0] + wts[:, 1:2] * y_s[slot, 1]


def moe_combine(ysorted, dest, h, wts, tm=256):
    m, d = h.shape
    tm = _tile(m, tm)
    return pl.pallas_call(
        _combine_kernel,
        grid_spec=pltpu.PrefetchScalarGridSpec(
            num_scalar_prefetch=1,
            grid=(m // tm,),
            in_specs=[pl.BlockSpec(memory_space=pl.ANY),
                      pl.BlockSpec((tm, d), lambda i, dest: (i, 0)),
                      pl.BlockSpec((tm, LANE), lambda i, dest: (i, 0))],
            out_specs=pl.BlockSpec((tm, d), lambda i, dest: (i, 0)),
            scratch_shapes=[pltpu.VMEM((2, 2, tm, d), F32), pltpu.SemaphoreType.DMA((2, 2))]),
        out_shape=jax.ShapeDtypeStruct((m, d), F32),
        compiler_params=_params(("arbitrary",)),
        name="moe_combine",
    )(dest, ysorted, h, wts)


def moe_ffn(h, nw, router, wg, wu, wd, tmg=1024):
    m, d = h.shape
    tmg = min(tmg, m)
    xs, idx, wts, cnt = moe_router(h, nw, router)
    counts = cnt[0, :N_EXPERTS].astype(jnp.int32)
    padded = (counts + tmg - 1) // tmg * tmg
    ends = jnp.cumsum(padded)
    offsets = ends - padded
    n_tiles = (2 * m) // tmg + N_EXPERTS
    n_sorted = n_tiles * tmg
    dest = (offsets[idx[:, 0:2]] + idx[:, 2:4]).reshape(-1)
    tile_start = jnp.arange(n_tiles, dtype=jnp.int32) * tmg
    tile_expert = jnp.minimum(jnp.sum(tile_start[:, None] >= ends[None, :], axis=1), N_EXPERTS - 1)
    n_active = (ends[-1] // tmg).reshape(1).astype(jnp.int32)
    xsorted = moe_dispatch(xs, dest, n_sorted)
    ysorted = ffn(xsorted, nw, tile_expert.astype(jnp.int32), n_active, wg, wu, wd, tmg, 512, False, "grouped_ffn")
    return moe_combine(ysorted, dest, h, wts)


def _ple_kernel(h_ref, p_ref, nw_ref, wg_ref, wp_ref, fw_ref, o_ref, *, final):
    h = h_ref[...]
    gate = _sigmoid(jnp.dot(_rms(h, nw_ref[...]).astype(BF16), wg_ref[...], preferred_element_type=F32))
    out = h + jnp.dot(p_ref[...].astype(BF16), wp_ref[...], preferred_element_type=F32) * gate
    o_ref[...] = _rms(out, fw_ref[...]) if final else out


def ple(h, p, nw, wg, wp, fw, final, tm=512):
    m, d = h.shape
    pd = p.shape[1]
    tm = _tile(m, tm)
    return pl.pallas_call(
        functools.partial(_ple_kernel, final=final),
        grid=(m // tm,),
        in_specs=[pl.BlockSpec((tm, d), lambda i: (i, 0)),
                  pl.BlockSpec((tm, pd), lambda i: (i, 0)),
                  pl.BlockSpec((1, d), lambda i: (0, 0)),
                  pl.BlockSpec((d, d), lambda i: (0, 0)),
                  pl.BlockSpec((pd, d), lambda i: (0, 0)),
                  pl.BlockSpec((1, d), lambda i: (0, 0))],
        out_specs=pl.BlockSpec((tm, d), lambda i: (i, 0)),
        out_shape=jax.ShapeDtypeStruct((m, d), F32),
        compiler_params=_params(("parallel",)),
        name="ple",
    )(h, p, nw.reshape(1, d), wg, wp, fw.reshape(1, d))


def _split_w_in(w):
    d = w.shape[0]
    dn_w, diff_w, ml_w = DN_HEADS * HEAD_W, DIFF_HEADS * HEAD_W, ML_HEADS * HEAD_W
    g0 = 3 * dn_w + 2 * ml_w + dn_w
    g1 = g0 + 2 * DN_HEADS
    g2 = g1 + 3 * diff_w + 2 * ml_w
    g3 = g2 + 2 * ML_HEADS
    assert w.shape[1] == g3 and g0 + g2 - g1 == N_BIG
    big = jnp.concatenate([w[:, :g0], w[:, g1:g2]], axis=1).astype(BF16)
    small = jnp.concatenate([w[:, g0:g1], w[:, g2:g3],
                             jnp.zeros((d, LANE - (g1 - g0) - (g3 - g2)), w.dtype)], axis=1).astype(BF16)
    return big, small


def kernel(x, p, attn_norm, w_in, conv_dn, conv_ml, dn_a_log, dn_dt_bias, dn_norm, diff_lq1, diff_lk1, diff_lq2, diff_lk2, diff_norm, ml_i_bias, ml_f_bias, ml_norm, w_out, ffn_norm, dense_w_gate, dense_w_up, dense_w_down, router, moe_w_gate, moe_w_up, moe_w_down, ple_norm, ple_proj, ple_gate, final_norm):
    batch, seq, d = x.shape
    depth = w_in.shape[0]
    m = batch * seq
    h = x.reshape(m, d)
    dn_w, diff_w = DN_HEADS * HEAD_W, DIFF_HEADS * HEAD_W
    for i in range(depth):
        lambda_init = 0.8 - 0.6 * math.exp(-0.3 * i)
        w_big, w_small = _split_w_in(w_in[i])
        zb, zs = norm_proj(h, attn_norm[i], w_big, w_small)
        o_dn = deltanet(zb, zs, conv_dn[i], dn_a_log[i], dn_dt_bias[i], dn_norm[i], batch, seq)
        o_diff = diff_attention(zb, diff_lq1[i], diff_lk1[i], diff_lq2[i], diff_lk2[i], diff_norm[i],
                                lambda_init, batch, seq)
        o_ml = mlstm(zb, zs, conv_ml[i], ml_i_bias[i], ml_f_bias[i], ml_norm[i], batch, seq)
        wo = w_out[i].astype(BF16)
        h = out_proj(o_dn, o_diff, o_ml, wo[:dn_w], wo[dn_w:dn_w + diff_w], wo[dn_w + diff_w:], h)
        j = i // 2
        if i % 2 == 0:
            h = dense_ffn(h, ffn_norm[i], dense_w_gate[j], dense_w_up[j], dense_w_down[j])
        else:
            h = moe_ffn(h, ffn_norm[i], router[j], moe_w_gate[j], moe_w_up[j], moe_w_down[j])
        h = ple(h, p[i].reshape(m, -1), ple_norm[i], ple_gate[i].astype(BF16), ple_proj[i].astype(BF16),
                final_norm, final=(i == depth - 1))
    return h.reshape(batch, seq, d)
```

```python
import functools
import math

import numpy as np
import jax
import jax.numpy as jnp
from jax import lax
from jax.experimental import pallas as pl
from jax.experimental.pallas import tpu as pltpu

F32 = jnp.float32
BF16 = jnp.bfloat16
EPS = 1e-6
LANE = 128
NEG = -1e30

DN_HEADS = 4
DIFF_HEADS = 8
ML_HEADS = 4
HEAD_W = 128
DIFF_DH = 64
CHUNK = 64
CONV_K = 4
N_EXPERTS = 8
VMEM_LIMIT = 56 * 1024 * 1024

CB_DN_Q, CB_DN_K, CB_DN_V = 0, 4, 8
CB_ML_Q, CB_ML_K = 12, 16
CB_DN_Z = 20
CB_AQ, CB_AK, CB_AV = 24, 32, 40
CB_ML_V, CB_ML_O = 48, 52
N_BIG = 56 * LANE
SL_DN_B, SL_DN_A, SL_ML_I, SL_ML_F = 0, 4, 8, 12


def _params(sem):
    return pltpu.CompilerParams(dimension_semantics=sem, vmem_limit_bytes=VMEM_LIMIT)


def _dot(a, b):
    return jnp.dot(a.astype(BF16), b.astype(BF16), preferred_element_type=F32)


def _dot_nt(a, b):
    return lax.dot_general(a.astype(BF16), b.astype(BF16), (((1,), (1,)), ((), ())),
                           preferred_element_type=F32)


def _dot_tn(a, b):
    return lax.dot_general(a.astype(BF16), b.astype(BF16), (((0,), (0,)), ((), ())),
                           preferred_element_type=F32)


def _dot_hi(a, b):
    return jnp.dot(a, b, preferred_element_type=F32, precision=lax.Precision.HIGHEST)


def _rms(x, w):
    return x * lax.rsqrt(jnp.mean(x * x, axis=-1, keepdims=True) + EPS) * w


def _sigmoid(x):
    return 1.0 / (1.0 + jnp.exp(-x))


def _silu(x):
    return x * _sigmoid(x)


def _softplus(x):
    return jnp.maximum(x, 0.0) + jnp.log(1.0 + jnp.exp(-jnp.abs(x)))


def _log_sigmoid(x):
    return -_softplus(-x)


def _tile(m, t):
    t = min(m, t)
    assert m % t == 0
    return t


def _norm_proj_kernel(x_ref, nw_ref, w_ref, ws_ref, o_ref, os_ref, xs_ref):
    @pl.when(pl.program_id(1) == 0)
    def _():
        xs_ref[...] = _rms(x_ref[...], nw_ref[...]).astype(BF16)
        os_ref[...] = jnp.dot(xs_ref[...], ws_ref[...], preferred_element_type=F32)

    o_ref[...] = jnp.dot(xs_ref[...], w_ref[...], preferred_element_type=F32).astype(o_ref.dtype)


def norm_proj(x, nw, w_big, w_small, tm=1024, tn=1792):
    m, d = x.shape
    n = w_big.shape[1]
    tm, tn = _tile(m, tm), _tile(n, tn)
    return pl.pallas_call(
        _norm_proj_kernel,
        grid=(m // tm, n // tn),
        in_specs=[pl.BlockSpec((tm, d), lambda i, j: (i, 0)),
                  pl.BlockSpec((1, d), lambda i, j: (0, 0)),
                  pl.BlockSpec((d, tn), lambda i, j: (0, j)),
                  pl.BlockSpec((d, LANE), lambda i, j: (0, 0))],
        out_specs=[pl.BlockSpec((tm, tn), lambda i, j: (i, j)),
                   pl.BlockSpec((tm, LANE), lambda i, j: (i, 0))],
        out_shape=[jax.ShapeDtypeStruct((m, n), BF16), jax.ShapeDtypeStruct((m, LANE), F32)],
        scratch_shapes=[pltpu.VMEM((tm, d), BF16)],
        compiler_params=_params(("parallel", "arbitrary")),
        name="norm_proj",
    )(x, nw.reshape(1, d), w_big, w_small)


def _chunk_conv_silu(ref, w, ci, rows):
    cur = ref[rows, :].astype(F32)
    prev_rows = pl.ds(pl.multiple_of(jnp.maximum(ci * CHUNK - 16, 0), 16), 16)
    prev = jnp.where(ci > 0, ref[prev_rows, :].astype(F32)[8:16, :], 0.0)
    ext = jnp.concatenate([prev, cur], axis=0)
    y = cur * w[CONV_K - 1:CONV_K, :]
    for back in range(1, CONV_K):
        y = y + pltpu.roll(ext, back, axis=0)[8:8 + CHUNK, :] * w[CONV_K - 1 - back:CONV_K - back, :]
    return _silu(y)


def _dot2(l_bf16, x):
    hi = x.astype(BF16)
    lo = (x - hi.astype(F32)).astype(BF16)
    return (jnp.dot(l_bf16, hi, preferred_element_type=F32)
            + jnp.dot(l_bf16, lo, preferred_element_type=F32))


def _lane_rows(*rows):
    out = jnp.zeros((8, LANE), F32)
    for r, (lane0, vals) in enumerate(rows):
        out = out.at[r, lane0:lane0 + vals.shape[0]].set(vals.astype(F32))
    return out


def _chunk_masks():
    r = lax.broadcasted_iota(jnp.int32, (CHUNK, CHUNK), 0)
    c = lax.broadcasted_iota(jnp.int32, (CHUNK, CHUNK), 1)
    return r, c


def _chunk_rows(ci):
    return pl.ds(pl.multiple_of(ci * CHUNK, CHUNK), CHUNK)


assert DN_HEADS == ML_HEADS
HEADS = range(DN_HEADS)
COLS = [slice(h * HEAD_W, (h + 1) * HEAD_W) for h in HEADS]
UNROLL = 4


def _deltanet_kernel(gp_ref, q_ref, k_ref, v_ref, zg_ref, zs_ref, cw_ref, nw_ref, o_ref,
                     u_s, w_s, at_s, qg_s, kg_s, gl_s):
    n_chunks = q_ref.shape[0] // CHUNK
    width = DN_HEADS * HEAD_W
    r, c = _chunk_masks()
    low_incl = (r >= c).astype(BF16)
    strict_up = (r > c).astype(F32)
    tri_incl = r >= c
    tri_strict = r > c
    cw = cw_ref[...]
    rate_row = jnp.exp(gp_ref[0:1, :])
    bias_row = gp_ref[1:2, :]

    def prep(cp, carry):
        units, qc, kc, kb, kcb, vbeta, g = [], [], [], [], [], [], []
        for t in range(UNROLL):
            ci = UNROLL * cp + t
            rows = _chunk_rows(ci)
            qa = _chunk_conv_silu(q_ref, cw[:, :width], ci, rows)
            ka = _chunk_conv_silu(k_ref, cw[:, width:2 * width], ci, rows)
            va = _chunk_conv_silu(v_ref, cw[:, 2 * width:], ci, rows)
            zs = zs_ref[rows, :]
            sig = _sigmoid(zs)
            g_all = -rate_row * _softplus(zs + bias_row)
            for h in HEADS:
                units.append((ci, rows, h))
                qh, kh = qa[:, COLS[h]], ka[:, COLS[h]]
                qc.append(qh * (lax.rsqrt(jnp.sum(qh * qh, axis=-1, keepdims=True) + EPS) * (HEAD_W ** -0.5)))
                kc.append(kh * lax.rsqrt(jnp.sum(kh * kh, axis=-1, keepdims=True) + EPS))
                beta = sig[:, SL_DN_B + h:SL_DN_B + h + 1]
                g.append(g_all[:, SL_DN_A + h:SL_DN_A + h + 1])
                kb.append(kc[-1] * beta)
                kcb.append(kc[-1].astype(BF16))
                vbeta.append(va[:, COLS[h]] * beta)
        us = range(len(units))
        dmat = [_dot2(low_incl, jnp.broadcast_to(g[u], (CHUNK, CHUNK)) * strict_up) for u in us]
        kk = [_dot_nt(kb[u], kcb[u]) for u in us]
        qk = [_dot_nt(qc[u], kcb[u]) for u in us]
        gc = [dmat[u][:, :1] + g[u][:1, :] for u in us]
        decay = [jnp.where(tri_incl, jnp.exp(dmat[u]), 0.0) for u in us]
        eg = [jnp.exp(gc[u]) for u in us]
        a = [jnp.where(tri_strict, kk[u] * decay[u], 0.0).astype(BF16) for u in us]
        x = [jnp.concatenate([vbeta[u], kb[u] * eg[u]], axis=1) for u in us]
        ax = [_dot(a[u], x[u]) for u in us]
        p = [_dot(a[u], a[u]).astype(BF16) for u in us]
        x = [x[u] - ax[u] for u in us]
        for step in range(5):
            px = [_dot(p[u], x[u]) for u in us]
            if step < 4:
                p = [_dot(p[u], p[u]).astype(BF16) for u in us]
            x = [x[u] + px[u] for u in us]
        for u, (ci, rows, h) in enumerate(units):
            at_s[rows, h * HEAD_W:h * HEAD_W + CHUNK] = (qk[u] * decay[u]).astype(BF16)
            u_s[rows, COLS[h]] = x[u][:, :HEAD_W]
            w_s[rows, COLS[h]] = x[u][:, HEAD_W:].astype(BF16)
            g_last = gc[u][CHUNK - 1:CHUNK, :]
            qg_s[rows, COLS[h]] = (qc[u] * eg[u]).astype(BF16)
            kg_s[rows, COLS[h]] = (kc[u] * jnp.exp(g_last - gc[u])).astype(BF16)
            gl_s[pl.ds(pl.multiple_of(ci * 8, 8), 8), COLS[h]] = jnp.broadcast_to(jnp.exp(g_last), (8, HEAD_W))
        return carry

    assert n_chunks % UNROLL == 0
    lax.fori_loop(0, n_chunks // UNROLL, prep, 0)

    nw = nw_ref[...]

    def scan(ci, states):
        rows = _chunk_rows(ci)
        sb = [states[h].astype(BF16) for h in HEADS]
        ws = [jnp.dot(w_s[rows, COLS[h]], sb[h], preferred_element_type=F32) for h in HEADS]
        qs = [jnp.dot(qg_s[rows, COLS[h]], sb[h], preferred_element_type=F32) for h in HEADS]
        vb = [(u_s[rows, COLS[h]] - ws[h]).astype(BF16) for h in HEADS]
        av = [jnp.dot(at_s[rows, h * HEAD_W:h * HEAD_W + CHUNK], vb[h], preferred_element_type=F32)
              for h in HEADS]
        kv = [_dot_tn(kg_s[rows, COLS[h]], vb[h]) for h in HEADS]
        out = []
        for h in HEADS:
            gl = gl_s[pl.ds(pl.multiple_of(ci * 8, 8), 1), COLS[h]]
            out.append(states[h] * gl + kv[h])
            o_ref[rows, COLS[h]] = (_rms(qs[h] + av[h], nw)
                                    * _silu(zg_ref[rows, COLS[h]].astype(F32))).astype(o_ref.dtype)
        return tuple(out)

    lax.fori_loop(0, n_chunks, scan, tuple(jnp.zeros((HEAD_W, HEAD_W), F32) for _ in range(DN_HEADS)))


def deltanet(zb, zs, conv_dn, a_log, dt_bias, dn_norm, batch, seq):
    m = zb.shape[0]
    width = DN_HEADS * HEAD_W
    blk = lambda cb: pl.BlockSpec((seq, width), lambda b: (b, cb // DN_HEADS))
    return pl.pallas_call(
        _deltanet_kernel,
        grid=(batch,),
        in_specs=[pl.BlockSpec((8, LANE), lambda b: (0, 0)),
                  blk(CB_DN_Q), blk(CB_DN_K), blk(CB_DN_V), blk(CB_DN_Z),
                  pl.BlockSpec((seq, LANE), lambda b: (b, 0)),
                  pl.BlockSpec((CONV_K, 3 * width), lambda b: (0, 0)),
                  pl.BlockSpec((1, HEAD_W), lambda b: (0, 0))],
        out_specs=pl.BlockSpec((seq, width), lambda b: (b, 0)),
        out_shape=jax.ShapeDtypeStruct((m, width), BF16),
        scratch_shapes=[pltpu.VMEM((seq, width), F32), pltpu.VMEM((seq, width), BF16),
                        pltpu.VMEM((seq, width), BF16), pltpu.VMEM((seq, width), BF16),
                        pltpu.VMEM((seq, width), BF16),
                        pltpu.VMEM((seq // CHUNK * 8, width), F32)],
        compiler_params=_params(("parallel",)),
        name="deltanet",
    )(_lane_rows((SL_DN_A, a_log), (SL_DN_A, dt_bias)), zb, zb, zb, zb, zs, conv_dn,
      dn_norm.reshape(1, HEAD_W))


def _mlstm_kernel(gp_ref, q_ref, k_ref, v_ref, og_ref, zs_ref, cw_ref, nw_ref, o_ref):
    n_chunks = q_ref.shape[0] // CHUNK
    width = ML_HEADS * HEAD_W
    r, c = _chunk_masks()
    low_incl = (r >= c).astype(BF16)
    strict_up = (r > c).astype(F32)
    eye = (r == c).astype(F32)
    tri_incl = r >= c
    nw = nw_ref[...]
    cw = cw_ref[...]

    def local(ci):
        rows = _chunk_rows(ci)
        qa = _chunk_conv_silu(q_ref, cw[:, :width], ci, rows)
        ka = _chunk_conv_silu(k_ref, cw[:, width:], ci, rows) * (HEAD_W ** -0.5)
        zs = zs_ref[rows, :]
        ip_all = zs + gp_ref[0:1, :]
        lf_all = _log_sigmoid(zs + gp_ref[1:2, :])
        qc = [qa[:, COLS[h]] for h in HEADS]
        kc = [ka[:, COLS[h]] for h in HEADS]
        ip = [ip_all[:, SL_ML_I + h:SL_ML_I + h + 1] for h in HEADS]
        lf = [lf_all[:, SL_ML_F + h:SL_ML_F + h + 1] for h in HEADS]
        dl = [_dot2(low_incl, jnp.broadcast_to(lf[h], (CHUNK, CHUNK)) * strict_up
                    + jnp.broadcast_to(ip[h], (CHUNK, CHUNK)) * eye) for h in HEADS]
        qk = [_dot_nt(qc[h], kc[h]) for h in HEADS]
        heads = []
        for h in HEADS:
            b = dl[h][:, :1] - ip[h][:1, :] + lf[h][:1, :]
            dlog = jnp.where(tri_incl, dl[h], NEG)
            b_last = b[CHUNK - 1:CHUNK, :]
            a = b_last - b + ip[h]
            heads.append(dict(qc=qc[h], kc=kc[h], qk=qk[h], b=b, dlog=dlog, b_last=b_last, a=a,
                              dmax=jnp.max(dlog, axis=-1, keepdims=True),
                              a_max=jnp.max(a, axis=0, keepdims=True)))
        return rows, heads

    def advance(rows, heads, carry):
        vc = [v_ref[rows, COLS[h]] for h in HEADS]
        s, wk, inter, m_t, m_new, scale = [], [], [], [], [], []
        for h in HEADS:
            t = heads[h]
            m_st = carry[h][2]
            m_t.append(jnp.maximum(t["dmax"], t["b"] + m_st))
            s.append(t["qk"] * jnp.exp(t["dlog"] - m_t[h]))
            inter.append(jnp.exp(t["b"] + m_st - m_t[h]))
            m_new.append(jnp.maximum(t["b_last"] + m_st, t["a_max"]))
            scale.append(jnp.exp(t["b_last"] + m_st - m_new[h]))
            wk.append(t["kc"] * jnp.exp(t["a"] - m_new[h]))
        sv = [_dot(s[h], vc[h]) for h in HEADS]
        kv = [_dot_tn(wk[h], vc[h]) for h in HEADS]
        qcs = [_dot(heads[h]["qc"], carry[h][0]) for h in HEADS]
        out = []
        for h in HEADS:
            c_st, n_st, _ = carry[h]
            qc = heads[h]["qc"]
            num = sv[h] + inter[h] * qcs[h]
            den = (jnp.sum(s[h], axis=-1, keepdims=True)
                   + inter[h] * jnp.sum(qc * n_st, axis=-1, keepdims=True))
            hh = num / jnp.maximum(jnp.abs(den), jnp.exp(-m_t[h]))
            out.append((scale[h] * c_st + kv[h],
                        scale[h] * n_st + jnp.sum(wk[h], axis=0, keepdims=True),
                        m_new[h]))
            o_ref[rows, COLS[h]] = (_sigmoid(og_ref[rows, COLS[h]].astype(F32))
                                    * _rms(hh, nw)).astype(o_ref.dtype)
        return tuple(out)

    init = tuple((jnp.zeros((HEAD_W, HEAD_W), F32), jnp.zeros((1, HEAD_W), F32), jnp.zeros((1, 1), F32))
                 for _ in range(ML_HEADS))
    lax.fori_loop(0, n_chunks, lambda ci, carry: advance(*local(ci), carry), init)


def mlstm(zb, zs, conv_ml, i_bias, f_bias, ml_norm, batch, seq):
    m = zb.shape[0]
    width = ML_HEADS * HEAD_W
    blk = lambda cb: pl.BlockSpec((seq, width), lambda b: (b, cb // ML_HEADS))
    return pl.pallas_call(
        _mlstm_kernel,
        grid=(batch,),
        in_specs=[pl.BlockSpec((8, LANE), lambda b: (0, 0)),
                  blk(CB_ML_Q), blk(CB_ML_K), blk(CB_ML_V), blk(CB_ML_O),
                  pl.BlockSpec((seq, LANE), lambda b: (b, 0)),
                  pl.BlockSpec((CONV_K, 2 * width), lambda b: (0, 0)),
                  pl.BlockSpec((1, HEAD_W), lambda b: (0, 0))],
        out_specs=pl.BlockSpec((seq, width), lambda b: (b, 0)),
        out_shape=jax.ShapeDtypeStruct((m, width), BF16),
        compiler_params=_params(("parallel",)),
        name="mlstm",
    )(_lane_rows((SL_ML_I, i_bias), (SL_ML_F, f_bias)), zb, zb, zb, zb, zs, conv_ml,
      ml_norm.reshape(1, HEAD_W))


LOG2E = math.log2(math.e)
LOG2E_HI = float(np.asarray(LOG2E, dtype=BF16))
LOG2E_LO = LOG2E - LOG2E_HI
SLAB = 128
RAMP = 256


def _with_lanes(x, base, vals):
    lane = lax.broadcasted_iota(jnp.int32, x.shape, 1)
    for i, v in enumerate(vals):
        x = jnp.where(lane == base + i, v, x)
    return x


def _diff_attn_kernel(lq1_ref, lk1_ref, lq2_ref, lk2_ref, q_ref, k_ref, v_ref, nw_ref, o_ref,
                      k0_s, k1_s, *, tq, lambda_init):
    h = pl.program_id(1)
    qi = pl.program_id(2)
    tk = RAMP
    per_pair = tq // (2 * tk)
    assert k_ref.shape[0] <= RAMP * RAMP and tq % (2 * tk) == 0
    slope = jnp.exp2(jnp.full((1, 1), -(h + 1).astype(F32) * (8.0 / DIFF_HEADS), F32))

    @pl.when(qi == 0)
    def _():
        kk = k_ref[...]
        pos = lax.broadcasted_iota(jnp.int32, kk.shape, 0)
        fine = ((pos & (RAMP - 1)).astype(F32) * slope).astype(BF16)
        coarse = ((pos - (pos & (RAMP - 1))).astype(F32) * slope).astype(BF16)
        k0_s[...] = _with_lanes(kk, DIFF_DH, (fine, fine, coarse, coarse))
        k1_s[...] = _with_lanes(kk, 0, (fine, fine, coarse, coarse))

    lam = (jnp.exp(jnp.sum(lq1_ref[...] * lk1_ref[...], axis=-1, keepdims=True))
           - jnp.exp(jnp.sum(lq2_ref[...] * lk2_ref[...], axis=-1, keepdims=True)) + lambda_init)
    lane = lax.broadcasted_iota(jnp.int32, (tq, HEAD_W), 1)
    q = q_ref[...].astype(F32) * (DIFF_DH ** -0.5 * LOG2E)
    l2e = (LOG2E_HI, LOG2E_LO, LOG2E_HI, LOG2E_LO)
    q0 = _with_lanes(jnp.where(lane < DIFF_DH, q, 0.0), DIFF_DH, l2e).astype(BF16)
    q1 = _with_lanes(jnp.where(lane >= DIFF_DH, q, 0.0), 0, l2e).astype(BF16)
    per_map = tq // SLAB
    n_slab = 2 * per_map
    q_slab = [qm[j * SLAB:(j + 1) * SLAB, :] for qm in (q0, q1) for j in range(per_map)]

    def run_pair(kp, diag_at, carry):
        def needed(i, blk):
            return diag_at is None or diag_at + blk * tk <= (i % per_map) * SLAB + SLAB - 1

        scores, values = [], []
        for blk in range(2):
            rows = pl.ds(pl.multiple_of((2 * kp + blk) * tk, tk), tk)
            kb = (k0_s[rows, :], k1_s[rows, :])
            values.append(v_ref[rows, :])
            scores.append([_dot_nt(q_slab[i], kb[i // per_map]) if needed(i, blk) else None
                           for i in range(n_slab)])
        for blk in range(2):
            out = []
            for i in range(n_slab):
                if not needed(i, blk):
                    out.append(carry[i])
                    continue
                m_i, l_i, acc = carry[i]
                s = scores[blk][i]
                if diag_at is not None and diag_at + (blk + 1) * tk - 1 > (i % per_map) * SLAB:
                    ri = lax.broadcasted_iota(jnp.int32, (SLAB, tk), 0) + (i % per_map) * SLAB
                    ci = lax.broadcasted_iota(jnp.int32, (SLAB, tk), 1) + (diag_at + blk * tk)
                    s = jnp.where(ci <= ri, s, NEG)
                m_new = jnp.maximum(m_i, jnp.max(s, axis=-1, keepdims=True))
                p = jnp.exp2(s - m_new)
                alpha = jnp.exp2(m_i - m_new)
                l_new = alpha * l_i + sum(p[:, j * LANE:(j + 1) * LANE] for j in range(tk // LANE))
                acc_new = alpha * acc + jnp.dot(p.astype(BF16), values[blk], preferred_element_type=F32)
                out.append((m_new, l_new, acc_new))
            carry = tuple(out)
        return carry

    init = tuple((jnp.full((SLAB, 1), NEG, F32), jnp.zeros((SLAB, LANE), F32), jnp.zeros((SLAB, HEAD_W), F32))
                 for _ in range(n_slab))
    carry = lax.fori_loop(0, qi * per_pair, lambda kp, cr: run_pair(kp, None, cr), init)
    for t in range(per_pair):
        carry = run_pair(qi * per_pair + t, t * 2 * tk, carry)
    nw = nw_ref[...]
    for j in range(per_map):
        (_, l0, a0), (_, l1, a1) = carry[j], carry[per_map + j]
        l0 = jnp.sum(l0, axis=-1, keepdims=True)
        l1 = jnp.sum(l1, axis=-1, keepdims=True)
        o = a0 / l0 - lam * (a1 / l1)
        o_ref[j * SLAB:(j + 1) * SLAB, :] = (_rms(o, nw) * (1.0 - lambda_init)).astype(o_ref.dtype)


def diff_attention(zb, lq1, lk1, lq2, lk2, diff_norm, lambda_init, batch, seq, tq=512):
    m = zb.shape[0]
    tq = _tile(seq, tq)
    nq = seq // tq
    vec = pl.BlockSpec((1, DIFF_DH), lambda b, h, i: (0, 0))
    return pl.pallas_call(
        functools.partial(_diff_attn_kernel, tq=tq, lambda_init=lambda_init),
        grid=(batch, DIFF_HEADS, nq),
        in_specs=[vec, vec, vec, vec,
                  pl.BlockSpec((tq, HEAD_W), lambda b, h, i: (b * nq + i, CB_AQ + h)),
                  pl.BlockSpec((seq, HEAD_W), lambda b, h, i: (b, CB_AK + h)),
                  pl.BlockSpec((seq, HEAD_W), lambda b, h, i: (b, CB_AV + h)),
                  pl.BlockSpec((1, HEAD_W), lambda b, h, i: (0, 0))],
        out_specs=pl.BlockSpec((tq, HEAD_W), lambda b, h, i: (b * nq + i, h)),
        out_shape=jax.ShapeDtypeStruct((m, DIFF_HEADS * HEAD_W), BF16),
        scratch_shapes=[pltpu.VMEM((seq, HEAD_W), BF16), pltpu.VMEM((seq, HEAD_W), BF16)],
        compiler_params=_params(("parallel", "parallel", "arbitrary")),
        name="diff_attn",
    )(lq1.reshape(1, -1), lk1.reshape(1, -1), lq2.reshape(1, -1), lk2.reshape(1, -1),
      zb, zb, zb, diff_norm.reshape(1, HEAD_W))


def _out_proj_kernel(a_ref, b_ref, c_ref, w_ref, h_ref, o_ref):
    r0, r1 = a_ref.shape[1], a_ref.shape[1] + b_ref.shape[1]
    acc = jnp.dot(a_ref[...], w_ref[:r0, :], preferred_element_type=F32)
    acc += jnp.dot(b_ref[...], w_ref[r0:r1, :], preferred_element_type=F32)
    acc += jnp.dot(c_ref[...], w_ref[r1:, :], preferred_element_type=F32)
    o_ref[...] = h_ref[...] + acc


def out_proj(o_dn, o_diff, o_ml, w, h, tm=512):
    m, d = h.shape
    tm = _tile(m, tm)
    row = lambda width: pl.BlockSpec((tm, width), lambda i: (i, 0))
    return pl.pallas_call(
        _out_proj_kernel,
        grid=(m // tm,),
        in_specs=[row(o_dn.shape[1]), row(o_diff.shape[1]), row(o_ml.shape[1]),
                  pl.BlockSpec(w.shape, lambda i: (0, 0)), row(d)],
        out_specs=row(d),
        out_shape=jax.ShapeDtypeStruct((m, d), F32),
        compiler_params=_params(("parallel",)),
        name="out_proj",
    )(o_dn, o_diff, o_ml, w, h)


RING = 2


def _ffn_kernel(te_ref, na_ref, x_hbm, nw_ref, wg_hbm, wu_hbm, wd_hbm, o_ref, x_b, xs_ref, wg_b, wu_b, wd_b,
                sems, x_sem, *, fuse_norm):
    i, f = pl.program_id(0), pl.program_id(1)
    nf = pl.num_programs(1)
    tm = x_b.shape[0]
    tf = wg_b.shape[2]
    step = i * nf + f
    total = na_ref[0] * nf

    def copies(s):
        e = te_ref[s // nf]
        cols = pl.ds(pl.multiple_of((s % nf) * tf, tf), tf)
        slot = s % RING
        return (pltpu.make_async_copy(wg_hbm.at[e, :, cols], wg_b.at[slot], sems.at[slot, 0]),
                pltpu.make_async_copy(wu_hbm.at[e, :, cols], wu_b.at[slot], sems.at[slot, 1]),
                pltpu.make_async_copy(wd_hbm.at[e, cols, :], wd_b.at[slot], sems.at[slot, 2]))

    def start(s):
        @pl.when(s < total)
        def _():
            for c in copies(s):
                c.start()

    def x_copy(tile):
        return pltpu.make_async_copy(x_hbm.at[pl.ds(pl.multiple_of(tile * tm, tm), tm), :], x_b, x_sem)

    @pl.when(step == 0)
    def _():
        x_copy(0).start()
        for s in range(RING - 1):
            start(s)

    @pl.when((f == 1) & (i + 1 < pl.num_programs(0)))
    def _():
        x_copy(i + 1).start()

    @pl.when(f == 0)
    def _():
        x_copy(i).wait()
        x = x_b[...]
        if fuse_norm:
            xs_ref[...] = _rms(x, nw_ref[...]).astype(BF16)
            o_ref[...] = x
        else:
            xs_ref[...] = x.astype(BF16)
            o_ref[...] = jnp.zeros_like(o_ref)

    @pl.when(i < na_ref[0])
    def _():
        start(step + RING - 1)
        for c in copies(step):
            c.wait()
        slot = step % RING
        xs = xs_ref[...]
        g = jnp.dot(xs, wg_b[slot].astype(BF16), preferred_element_type=F32)
        u = jnp.dot(xs, wu_b[slot].astype(BF16), preferred_element_type=F32)
        o_ref[...] += jnp.dot((_silu(g) * u).astype(BF16), wd_b[slot].astype(BF16), preferred_element_type=F32)


def ffn(x, nw, tile_expert, n_active, wg, wu, wd, tm, tf, fuse_norm, name):
    rows, d = x.shape
    dff = wg.shape[2]
    tf = _tile(dff, tf)
    any_spec = pl.BlockSpec(memory_space=pl.ANY)
    return pl.pallas_call(
        functools.partial(_ffn_kernel, fuse_norm=fuse_norm),
        grid_spec=pltpu.PrefetchScalarGridSpec(
            num_scalar_prefetch=2,
            grid=(rows // tm, dff // tf),
            in_specs=[any_spec,
                      pl.BlockSpec((1, d), lambda i, f, te, na: (0, 0)),
                      any_spec, any_spec, any_spec],
            out_specs=pl.BlockSpec((tm, d), lambda i, f, te, na: (i, 0), pipeline_mode=pl.Buffered(1)),
            scratch_shapes=[pltpu.VMEM((tm, d), F32), pltpu.VMEM((tm, d), BF16),
                            pltpu.VMEM((RING, d, tf), F32), pltpu.VMEM((RING, d, tf), F32),
                            pltpu.VMEM((RING, tf, d), F32), pltpu.SemaphoreType.DMA((RING, 3)),
                            pltpu.SemaphoreType.DMA(())]),
        out_shape=jax.ShapeDtypeStruct((rows, d), F32),
        compiler_params=_params(("arbitrary", "arbitrary")),
        name=name,
    )(tile_expert, n_active, x, nw.reshape(1, d), wg, wu, wd)


def dense_ffn(h, nw, wg, wu, wd, tm=1024, tf=512):
    tm = _tile(h.shape[0], tm)
    n_tiles = h.shape[0] // tm
    return ffn(h, nw, jnp.zeros((n_tiles,), jnp.int32), jnp.full((1,), n_tiles, jnp.int32),
               wg[None], wu[None], wd[None], tm, tf, True, "dense_ffn")


def _router_kernel(h_ref, nw_ref, wr_ref, xs_ref, idx_ref, wts_ref, cnt_ref, carry_ref):
    i = pl.program_id(0)
    tm = h_ref.shape[0]

    @pl.when(i == 0)
    def _():
        carry_ref[...] = jnp.zeros_like(carry_ref)

    xn = _rms(h_ref[...], nw_ref[...])
    xs_ref[...] = xn
    logits = _dot_hi(xn, wr_ref[...])
    lane = lax.broadcasted_iota(jnp.int32, logits.shape, 1)
    logits = jnp.where(lane < N_EXPERTS, logits, NEG)
    m1 = jnp.max(logits, axis=-1, keepdims=True)
    e1 = jnp.min(jnp.where(logits == m1, lane, LANE), axis=-1, keepdims=True)
    rest = jnp.where(lane == e1, NEG, logits)
    m2 = jnp.max(rest, axis=-1, keepdims=True)
    e2 = jnp.min(jnp.where(rest == m2, lane, LANE), axis=-1, keepdims=True)
    ex = jnp.exp(m2 - m1)
    w1 = 1.0 / (1.0 + ex)
    w2 = ex * w1
    oh1 = (lane == e1).astype(F32)
    oh2 = (lane == e2).astype(F32)
    oh = oh1 + oh2
    r = lax.broadcasted_iota(jnp.int32, (tm, tm), 0)
    c = lax.broadcasted_iota(jnp.int32, (tm, tm), 1)
    before = jnp.dot((r > c).astype(BF16), oh.astype(BF16), preferred_element_type=F32) + carry_ref[...]
    rank1 = jnp.sum(before * oh1, axis=-1, keepdims=True)
    rank2 = jnp.sum(before * oh2, axis=-1, keepdims=True)
    carry_ref[...] += jnp.sum(oh, axis=0, keepdims=True)
    cnt_ref[...] = jnp.broadcast_to(carry_ref[...], cnt_ref.shape)
    idx = jnp.where(lane == 0, e1, jnp.where(lane == 1, e2, 0))
    idx = jnp.where(lane == 2, rank1.astype(jnp.int32), jnp.where(lane == 3, rank2.astype(jnp.int32), idx))
    idx_ref[...] = idx
    wts_ref[...] = jnp.where(lane == 0, w1, jnp.where(lane == 1, w2, 0.0))


def moe_router(h, nw, router, tm=512):
    m, d = h.shape
    tm = _tile(m, tm)
    wr = jnp.zeros((d, LANE), F32).at[:, :N_EXPERTS].set(router)
    return pl.pallas_call(
        _router_kernel,
        grid=(m // tm,),
        in_specs=[pl.BlockSpec((tm, d), lambda i: (i, 0)),
                  pl.BlockSpec((1, d), lambda i: (0, 0)),
                  pl.BlockSpec((d, LANE), lambda i: (0, 0))],
        out_specs=[pl.BlockSpec((tm, d), lambda i: (i, 0)),
                   pl.BlockSpec((tm, LANE), lambda i: (i, 0)),
                   pl.BlockSpec((tm, LANE), lambda i: (i, 0)),
                   pl.BlockSpec((8, LANE), lambda i: (0, 0))],
        out_shape=[jax.ShapeDtypeStruct((m, d), F32),
                   jax.ShapeDtypeStruct((m, LANE), jnp.int32),
                   jax.ShapeDtypeStruct((m, LANE), F32),
                   jax.ShapeDtypeStruct((8, LANE), F32)],
        scratch_shapes=[pltpu.VMEM((1, LANE), F32)],
        compiler_params=_params(("arbitrary",)),
        name="moe_router",
    )(h, nw.reshape(1, d), wr)


ROW_UNROLL = 8


def _dispatch_kernel(dest_ref, xs_ref, init_ref, out_ref, sems):
    del init_ref
    tm = xs_ref.shape[0]

    def start(t, c):
        for k in range(2):
            pltpu.make_async_copy(xs_ref.at[pl.ds(t, 1), :], out_ref.at[pl.ds(dest_ref[2 * t + k], 1), :],
                                  sems.at[k]).start(priority=k)
        return c

    lax.fori_loop(0, tm, start, 0, unroll=ROW_UNROLL)
    for k in range(2):
        pltpu.make_async_copy(xs_ref, out_ref.at[pl.ds(0, tm), :], sems.at[k]).wait()


def moe_dispatch(xs, dest, n_sorted, tm=256):
    m, d = xs.shape
    tm = _tile(m, tm)
    init = jnp.zeros((n_sorted, d), xs.dtype)
    return pl.pallas_call(
        _dispatch_kernel,
        grid=(m // tm,),
        in_specs=[pl.BlockSpec((2 * tm,), lambda i: (i,), memory_space=pltpu.SMEM),
                  pl.BlockSpec((tm, d), lambda i: (i, 0)),
                  pl.BlockSpec(memory_space=pl.ANY)],
        out_specs=pl.BlockSpec(memory_space=pl.ANY),
        out_shape=jax.ShapeDtypeStruct((n_sorted, d), xs.dtype),
        scratch_shapes=[pltpu.SemaphoreType.DMA((2,))],
        input_output_aliases={2: 0},
        compiler_params=_params(("arbitrary",)),
        name="moe_dispatch",
    )(dest, xs, init)


def _combine_kernel(dest_ref, ys_ref, h_ref, wts_ref, o_ref, y_s, sems):
    i = pl.program_id(0)
    tm = h_ref.shape[0]

    def gather(tile, slot):
        def start(t, c):
            for k in range(2):
                src = dest_ref[2 * (tile * tm + t) + k]
                pltpu.make_async_copy(ys_ref.at[pl.ds(src, 1), :], y_s.at[slot, k, pl.ds(t, 1), :],
                                      sems.at[slot, k]).start(priority=k)
            return c

        lax.fori_loop(0, tm, start, 0, unroll=ROW_UNROLL)

    @pl.when(i == 0)
    def _():
        gather(0, 0)

    @pl.when(i + 1 < pl.num_programs(0))
    def _():
        gather(i + 1, (i + 1) % 2)

    slot = i % 2
    for k in range(2):
        pltpu.make_async_copy(ys_ref.at[pl.ds(0, tm), :], y_s.at[slot, k], sems.at[slot, k]).wait()
    wts = wts_ref[...]
    o_ref[...] = h_ref[...] + wts[:, 0:1] * y_s[slot, 0] + wts[:, 1:2] * y_s[slot, 1]


def moe_combine(ysorted, dest, h, wts, tm=256):
    m, d = h.shape
    tm = _tile(m, tm)
    return pl.pallas_call(
        _combine_kernel,
        grid_spec=pltpu.PrefetchScalarGridSpec(
            num_scalar_prefetch=1,
            grid=(m // tm,),
            in_specs=[pl.BlockSpec(memory_space=pl.ANY),
                      pl.BlockSpec((tm, d), lambda i, dest: (i, 0)),
                      pl.BlockSpec((tm, LANE), lambda i, dest: (i, 0))],
            out_specs=pl.BlockSpec((tm, d), lambda i, dest: (i, 0)),
            scratch_shapes=[pltpu.VMEM((2, 2, tm, d), F32), pltpu.SemaphoreType.DMA((2, 2))]),
        out_shape=jax.ShapeDtypeStruct((m, d), F32),
        compiler_params=_params(("arbitrary",)),
        name="moe_combine",
    )(dest, ysorted, h, wts)


def moe_ffn(h, nw, router, wg, wu, wd, tmg=1024):
    m, d = h.shape
    tmg = min(tmg, m)
    xs, idx, wts, cnt = moe_router(h, nw, router)
    counts = cnt[0, :N_EXPERTS].astype(jnp.int32)
    padded = (counts + tmg - 1) // tmg * tmg
    ends = jnp.cumsum(padded)
    offsets = ends - padded
    n_tiles = (2 * m) // tmg + N_EXPERTS
    n_sorted = n_tiles * tmg
    dest = (offsets[idx[:, 0:2]] + idx[:, 2:4]).reshape(-1)
    tile_start = jnp.arange(n_tiles, dtype=jnp.int32) * tmg
    tile_expert = jnp.minimum(jnp.sum(tile_start[:, None] >= ends[None, :], axis=1), N_EXPERTS - 1)
    n_active = (ends[-1] // tmg).reshape(1).astype(jnp.int32)
    xsorted = moe_dispatch(xs, dest, n_sorted)
    ysorted = ffn(xsorted, nw, tile_expert.astype(jnp.int32), n_active, wg, wu, wd, tmg, 512, False, "grouped_ffn")
    return moe_combine(ysorted, dest, h, wts)


def _ple_kernel(h_ref, p_ref, nw_ref, wg_ref, wp_ref, fw_ref, o_ref, *, final):
    h = h_ref[...]
    gate = _sigmoid(jnp.dot(_rms(h, nw_ref[...]).astype(BF16), wg_ref[...], preferred_element_type=F32))
    out = h + jnp.dot(p_ref[...].astype(BF16), wp_ref[...], preferred_element_type=F32) * gate
    o_ref[...] = _rms(out, fw_ref[...]) if final else out


def ple(h, p, nw, wg, wp, fw, final, tm=512):
    m, d = h.shape
    pd = p.shape[1]
    tm = _tile(m, tm)
    return pl.pallas_call(
        functools.partial(_ple_kernel, final=final),
        grid=(m // tm,),
        in_specs=[pl.BlockSpec((tm, d), lambda i: (i, 0)),
                  pl.BlockSpec((tm, pd), lambda i: (i, 0)),
                  pl.BlockSpec((1, d), lambda i: (0, 0)),
                  pl.BlockSpec((d, d), lambda i: (0, 0)),
                  pl.BlockSpec((pd, d), lambda i: (0, 0)),
                  pl.BlockSpec((1, d), lambda i: (0, 0))],
        out_specs=pl.BlockSpec((tm, d), lambda i: (i, 0)),
        out_shape=jax.ShapeDtypeStruct((m, d), F32),
        compiler_params=_params(("parallel",)),
        name="ple",
    )(h, p, nw.reshape(1, d), wg, wp, fw.reshape(1, d))


def _split_w_in(w):
    d = w.shape[0]
    dn_w, diff_w, ml_w = DN_HEADS * HEAD_W, DIFF_HEADS * HEAD_W, ML_HEADS * HEAD_W
    g0 = 3 * dn_w + 2 * ml_w + dn_w
    g1 = g0 + 2 * DN_HEADS
    g2 = g1 + 3 * diff_w + 2 * ml_w
    g3 = g2 + 2 * ML_HEADS
    assert w.shape[1] == g3 and g0 + g2 - g1 == N_BIG
    big = jnp.concatenate([w[:, :g0], w[:, g1:g2]], axis=1).astype(BF16)
    small = jnp.concatenate([w[:, g0:g1], w[:, g2:g3],
                             jnp.zeros((d, LANE - (g1 - g0) - (g3 - g2)), w.dtype)], axis=1).astype(BF16)
    return big, small


def kernel(x, p, attn_norm, w_in, conv_dn, conv_ml, dn_a_log, dn_dt_bias, dn_norm, diff_lq1, diff_lk1, diff_lq2, diff_lk2, diff_norm, ml_i_bias, ml_f_bias, ml_norm, w_out, ffn_norm, dense_w_gate, dense_w_up, dense_w_down, router, moe_w_gate, moe_w_up, moe_w_down, ple_norm, ple_proj, ple_gate, final_norm):
    batch, seq, d = x.shape
    depth = w_in.shape[0]
    m = batch * seq
    h = x.reshape(m, d)
    for i in range(depth):
        lambda_init = 0.8 - 0.6 * math.exp(-0.3 * i)
        w_big, w_small = _split_w_in(w_in[i])
        zb, zs = norm_proj(h, attn_norm[i], w_big, w_small)
        o_dn = deltanet(zb, zs, conv_dn[i], dn_a_log[i], dn_dt_bias[i], dn_norm[i], batch, seq)
        o_diff = diff_attention(zb, diff_lq1[i], diff_lk1[i], diff_lq2[i], diff_lk2[i], diff_norm[i],
                                lambda_init, batch, seq)
        o_ml = mlstm(zb, zs, conv_ml[i], ml_i_bias[i], ml_f_bias[i], ml_norm[i], batch, seq)
        h = out_proj(o_dn, o_diff, o_ml, w_out[i].astype(BF16), h)
        j = i // 2
        if i % 2 == 0:
            h = dense_ffn(h, ffn_norm[i], dense_w_gate[j], dense_w_up[j], dense_w_down[j])
        else:
            h = moe_ffn(h, ffn_norm[i], router[j], moe_w_gate[j], moe_w_up[j], moe_w_down[j])
        h = ple(h, p[i].reshape(m, -1), ple_norm[i], ple_gate[i].astype(BF16), ple_proj[i].astype(BF16),
                final_norm, final=(i == depth - 1))
    return h.reshape(batch, seq, d)
```

```python
import functools
import math

import numpy as np
import jax
import jax.numpy as jnp
from jax import lax
from jax.experimental import pallas as pl
from jax.experimental.pallas import tpu as pltpu

F32 = jnp.float32
BF16 = jnp.bfloat16
EPS = 1e-6
LANE = 128
NEG = -1e30

DN_HEADS = 4
DIFF_HEADS = 8
ML_HEADS = 4
HEAD_W = 128
DIFF_DH = 64
CHUNK = 64
CONV_K = 4
N_EXPERTS = 8
VMEM_LIMIT = 56 * 1024 * 1024

CB_DN_Q, CB_DN_K, CB_DN_V = 0, 4, 8
CB_ML_Q, CB_ML_K = 12, 16
CB_DN_Z = 20
CB_AQ, CB_AK, CB_AV = 24, 32, 40
CB_ML_V, CB_ML_O = 48, 52
N_BIG = 56 * LANE
SL_DN_B, SL_DN_A, SL_ML_I, SL_ML_F = 0, 4, 8, 12


def _params(sem):
    return pltpu.CompilerParams(dimension_semantics=sem, vmem_limit_bytes=VMEM_LIMIT)


def _dot(a, b):
    return jnp.dot(a.astype(BF16), b.astype(BF16), preferred_element_type=F32)


def _dot_nt(a, b):
    return lax.dot_general(a.astype(BF16), b.astype(BF16), (((1,), (1,)), ((), ())),
                           preferred_element_type=F32)


def _dot_tn(a, b):
    return lax.dot_general(a.astype(BF16), b.astype(BF16), (((0,), (0,)), ((), ())),
                           preferred_element_type=F32)


def _dot_hi(a, b):
    return jnp.dot(a, b, preferred_element_type=F32, precision=lax.Precision.HIGHEST)


def _rms(x, w):
    return x * lax.rsqrt(jnp.mean(x * x, axis=-1, keepdims=True) + EPS) * w


def _sigmoid(x):
    return 1.0 / (1.0 + jnp.exp(-x))


def _silu(x):
    return x * _sigmoid(x)


def _softplus(x):
    return jnp.maximum(x, 0.0) + jnp.log(1.0 + jnp.exp(-jnp.abs(x)))


def _log_sigmoid(x):
    return -_softplus(-x)


def _tile(m, t):
    t = min(m, t)
    assert m % t == 0
    return t


def _norm_proj_kernel(x_ref, nw_ref, w_ref, ws_ref, o_ref, os_ref, xs_ref):
    @pl.when(pl.program_id(1) == 0)
    def _():
        xs_ref[...] = _rms(x_ref[...], nw_ref[...]).astype(BF16)
        os_ref[...] = jnp.dot(xs_ref[...], ws_ref[...], preferred_element_type=F32)

    o_ref[...] = jnp.dot(xs_ref[...], w_ref[...], preferred_element_type=F32).astype(o_ref.dtype)


def norm_proj(x, nw, w_big, w_small, tm=1024, tn=1792):
    m, d = x.shape
    n = w_big.shape[1]
    tm, tn = _tile(m, tm), _tile(n, tn)
    return pl.pallas_call(
        _norm_proj_kernel,
        grid=(m // tm, n // tn),
        in_specs=[pl.BlockSpec((tm, d), lambda i, j: (i, 0)),
                  pl.BlockSpec((1, d), lambda i, j: (0, 0)),
                  pl.BlockSpec((d, tn), lambda i, j: (0, j)),
                  pl.BlockSpec((d, LANE), lambda i, j: (0, 0))],
        out_specs=[pl.BlockSpec((tm, tn), lambda i, j: (i, j)),
                   pl.BlockSpec((tm, LANE), lambda i, j: (i, 0))],
        out_shape=[jax.ShapeDtypeStruct((m, n), BF16), jax.ShapeDtypeStruct((m, LANE), F32)],
        scratch_shapes=[pltpu.VMEM((tm, d), BF16)],
        compiler_params=_params(("parallel", "arbitrary")),
        name="norm_proj",
    )(x, nw.reshape(1, d), w_big, w_small)


def _chunk_conv_silu(ref, w, ci, rows):
    cur = ref[rows, :].astype(F32)
    prev_rows = pl.ds(pl.multiple_of(jnp.maximum(ci * CHUNK - 16, 0), 16), 16)
    prev = jnp.where(ci > 0, ref[prev_rows, :].astype(F32)[8:16, :], 0.0)
    ext = jnp.concatenate([prev, cur], axis=0)
    y = cur * w[CONV_K - 1:CONV_K, :]
    for back in range(1, CONV_K):
        y = y + pltpu.roll(ext, back, axis=0)[8:8 + CHUNK, :] * w[CONV_K - 1 - back:CONV_K - back, :]
    return _silu(y)


def _dot2(l_bf16, x):
    hi = x.astype(BF16)
    lo = (x - hi.astype(F32)).astype(BF16)
    return (jnp.dot(l_bf16, hi, preferred_element_type=F32)
            + jnp.dot(l_bf16, lo, preferred_element_type=F32))


def _lane_rows(*rows):
    out = jnp.zeros((8, LANE), F32)
    for r, (lane0, vals) in enumerate(rows):
        out = out.at[r, lane0:lane0 + vals.shape[0]].set(vals.astype(F32))
    return out


def _chunk_masks():
    r = lax.broadcasted_iota(jnp.int32, (CHUNK, CHUNK), 0)
    c = lax.broadcasted_iota(jnp.int32, (CHUNK, CHUNK), 1)
    return r, c


def _chunk_rows(ci):
    return pl.ds(pl.multiple_of(ci * CHUNK, CHUNK), CHUNK)


assert DN_HEADS == ML_HEADS
HEADS = range(DN_HEADS)
COLS = [slice(h * HEAD_W, (h + 1) * HEAD_W) for h in HEADS]
UNROLL = 4


def _deltanet_kernel(gp_ref, q_ref, k_ref, v_ref, zg_ref, zs_ref, cw_ref, nw_ref, o_ref,
                     u_s, w_s, at_s, qg_s, kg_s, gl_s):
    n_chunks = q_ref.shape[0] // CHUNK
    width = DN_HEADS * HEAD_W
    r, c = _chunk_masks()
    low_incl = (r >= c).astype(BF16)
    strict_up = (r > c).astype(F32)
    tri_incl = r >= c
    tri_strict = r > c
    cw = cw_ref[...]
    rate_row = jnp.exp(gp_ref[0:1, :])
    bias_row = gp_ref[1:2, :]

    def prep(cp, carry):
        units, qc, kc, kb, kcb, vbeta, g = [], [], [], [], [], [], []
        for t in range(UNROLL):
            ci = UNROLL * cp + t
            rows = _chunk_rows(ci)
            qa = _chunk_conv_silu(q_ref, cw[:, :width], ci, rows)
            ka = _chunk_conv_silu(k_ref, cw[:, width:2 * width], ci, rows)
            va = _chunk_conv_silu(v_ref, cw[:, 2 * width:], ci, rows)
            zs = zs_ref[rows, :]
            sig = _sigmoid(zs)
            g_all = -rate_row * _softplus(zs + bias_row)
            for h in HEADS:
                units.append((ci, rows, h))
                qh, kh = qa[:, COLS[h]], ka[:, COLS[h]]
                qc.append(qh * (lax.rsqrt(jnp.sum(qh * qh, axis=-1, keepdims=True) + EPS) * (HEAD_W ** -0.5)))
                kc.append(kh * lax.rsqrt(jnp.sum(kh * kh, axis=-1, keepdims=True) + EPS))
                beta = sig[:, SL_DN_B + h:SL_DN_B + h + 1]
                g.append(g_all[:, SL_DN_A + h:SL_DN_A + h + 1])
                kb.append(kc[-1] * beta)
                kcb.append(kc[-1].astype(BF16))
                vbeta.append(va[:, COLS[h]] * beta)
        us = range(len(units))
        dmat = [_dot2(low_incl, jnp.broadcast_to(g[u], (CHUNK, CHUNK)) * strict_up) for u in us]
        kk = [_dot_nt(kb[u], kcb[u]) for u in us]
        qk = [_dot_nt(qc[u], kcb[u]) for u in us]
        gc = [dmat[u][:, :1] + g[u][:1, :] for u in us]
        decay = [jnp.where(tri_incl, jnp.exp(dmat[u]), 0.0) for u in us]
        eg = [jnp.exp(gc[u]) for u in us]
        a = [jnp.where(tri_strict, kk[u] * decay[u], 0.0).astype(BF16) for u in us]
        x = [jnp.concatenate([vbeta[u], kb[u] * eg[u]], axis=1) for u in us]
        ax = [_dot(a[u], x[u]) for u in us]
        p = [_dot(a[u], a[u]).astype(BF16) for u in us]
        x = [x[u] - ax[u] for u in us]
        for step in range(5):
            px = [_dot(p[u], x[u]) for u in us]
            if step < 4:
                p = [_dot(p[u], p[u]).astype(BF16) for u in us]
            x = [x[u] + px[u] for u in us]
        for u, (ci, rows, h) in enumerate(units):
            at_s[rows, h * HEAD_W:h * HEAD_W + CHUNK] = (qk[u] * decay[u]).astype(BF16)
            u_s[rows, COLS[h]] = x[u][:, :HEAD_W]
            w_s[rows, COLS[h]] = x[u][:, HEAD_W:].astype(BF16)
            g_last = gc[u][CHUNK - 1:CHUNK, :]
            qg_s[rows, COLS[h]] = (qc[u] * eg[u]).astype(BF16)
            kg_s[rows, COLS[h]] = (kc[u] * jnp.exp(g_last - gc[u])).astype(BF16)
            gl_s[pl.ds(pl.multiple_of(ci * 8, 8), 8), COLS[h]] = jnp.broadcast_to(jnp.exp(g_last), (8, HEAD_W))
        return carry

    assert n_chunks % UNROLL == 0
    lax.fori_loop(0, n_chunks // UNROLL, prep, 0)

    nw = nw_ref[...]

    def scan(ci, states):
        rows = _chunk_rows(ci)
        sb = [states[h].astype(BF16) for h in HEADS]
        ws = [jnp.dot(w_s[rows, COLS[h]], sb[h], preferred_element_type=F32) for h in HEADS]
        qs = [jnp.dot(qg_s[rows, COLS[h]], sb[h], preferred_element_type=F32) for h in HEADS]
        vb = [(u_s[rows, COLS[h]] - ws[h]).astype(BF16) for h in HEADS]
        av = [jnp.dot(at_s[rows, h * HEAD_W:h * HEAD_W + CHUNK], vb[h], preferred_element_type=F32)
              for h in HEADS]
        kv = [_dot_tn(kg_s[rows, COLS[h]], vb[h]) for h in HEADS]
        out = []
        for h in HEADS:
            gl = gl_s[pl.ds(pl.multiple_of(ci * 8, 8), 1), COLS[h]]
            out.append(states[h] * gl + kv[h])
            o_ref[rows, COLS[h]] = (_rms(qs[h] + av[h], nw)
                                    * _silu(zg_ref[rows, COLS[h]].astype(F32))).astype(o_ref.dtype)
        return tuple(out)

    lax.fori_loop(0, n_chunks, scan, tuple(jnp.zeros((HEAD_W, HEAD_W), F32) for _ in range(DN_HEADS)))


def deltanet(zb, zs, conv_dn, a_log, dt_bias, dn_norm, batch, seq):
    m = zb.shape[0]
    width = DN_HEADS * HEAD_W
    blk = lambda cb: pl.BlockSpec((seq, width), lambda b: (b, cb // DN_HEADS))
    return pl.pallas_call(
        _deltanet_kernel,
        grid=(batch,),
        in_specs=[pl.BlockSpec((8, LANE), lambda b: (0, 0)),
                  blk(CB_DN_Q), blk(CB_DN_K), blk(CB_DN_V), blk(CB_DN_Z),
                  pl.BlockSpec((seq, LANE), lambda b: (b, 0)),
                  pl.BlockSpec((CONV_K, 3 * width), lambda b: (0, 0)),
                  pl.BlockSpec((1, HEAD_W), lambda b: (0, 0))],
        out_specs=pl.BlockSpec((seq, width), lambda b: (b, 0)),
        out_shape=jax.ShapeDtypeStruct((m, width), BF16),
        scratch_shapes=[pltpu.VMEM((seq, width), F32), pltpu.VMEM((seq, width), BF16),
                        pltpu.VMEM((seq, width), BF16), pltpu.VMEM((seq, width), BF16),
                        pltpu.VMEM((seq, width), BF16),
                        pltpu.VMEM((seq // CHUNK * 8, width), F32)],
        compiler_params=_params(("parallel",)),
        name="deltanet",
    )(_lane_rows((SL_DN_A, a_log), (SL_DN_A, dt_bias)), zb, zb, zb, zb, zs, conv_dn,
      dn_norm.reshape(1, HEAD_W))


def _mlstm_kernel(gp_ref, q_ref, k_ref, v_ref, og_ref, zs_ref, cw_ref, nw_ref, o_ref):
    n_chunks = q_ref.shape[0] // CHUNK
    width = ML_HEADS * HEAD_W
    r, c = _chunk_masks()
    low_incl = (r >= c).astype(BF16)
    strict_up = (r > c).astype(F32)
    eye = (r == c).astype(F32)
    tri_incl = r >= c
    nw = nw_ref[...]
    cw = cw_ref[...]

    def local(ci):
        rows = _chunk_rows(ci)
        qa = _chunk_conv_silu(q_ref, cw[:, :width], ci, rows)
        ka = _chunk_conv_silu(k_ref, cw[:, width:], ci, rows) * (HEAD_W ** -0.5)
        zs = zs_ref[rows, :]
        ip_all = zs + gp_ref[0:1, :]
        lf_all = _log_sigmoid(zs + gp_ref[1:2, :])
        qc = [qa[:, COLS[h]] for h in HEADS]
        kc = [ka[:, COLS[h]] for h in HEADS]
        ip = [ip_all[:, SL_ML_I + h:SL_ML_I + h + 1] for h in HEADS]
        lf = [lf_all[:, SL_ML_F + h:SL_ML_F + h + 1] for h in HEADS]
        dl = [_dot2(low_incl, jnp.broadcast_to(lf[h], (CHUNK, CHUNK)) * strict_up
                    + jnp.broadcast_to(ip[h], (CHUNK, CHUNK)) * eye) for h in HEADS]
        qk = [_dot_nt(qc[h], kc[h]) for h in HEADS]
        heads = []
        for h in HEADS:
            b = dl[h][:, :1] - ip[h][:1, :] + lf[h][:1, :]
            dlog = jnp.where(tri_incl, dl[h], NEG)
            b_last = b[CHUNK - 1:CHUNK, :]
            a = b_last - b + ip[h]
            heads.append(dict(qc=qc[h], kc=kc[h], qk=qk[h], b=b, dlog=dlog, b_last=b_last, a=a,
                              dmax=jnp.max(dlog, axis=-1, keepdims=True),
                              a_max=jnp.max(a, axis=0, keepdims=True)))
        return rows, heads

    def advance(rows, heads, carry):
        vc = [v_ref[rows, COLS[h]] for h in HEADS]
        s, wk, inter, m_t, m_new, scale = [], [], [], [], [], []
        for h in HEADS:
            t = heads[h]
            m_st = carry[h][2]
            m_t.append(jnp.maximum(t["dmax"], t["b"] + m_st))
            s.append(t["qk"] * jnp.exp(t["dlog"] - m_t[h]))
            inter.append(jnp.exp(t["b"] + m_st - m_t[h]))
            m_new.append(jnp.maximum(t["b_last"] + m_st, t["a_max"]))
            scale.append(jnp.exp(t["b_last"] + m_st - m_new[h]))
            wk.append(t["kc"] * jnp.exp(t["a"] - m_new[h]))
        sv = [_dot(s[h], vc[h]) for h in HEADS]
        kv = [_dot_tn(wk[h], vc[h]) for h in HEADS]
        qcs = [_dot(heads[h]["qc"], carry[h][0]) for h in HEADS]
        out = []
        for h in HEADS:
            c_st, n_st, _ = carry[h]
            qc = heads[h]["qc"]
            num = sv[h] + inter[h] * qcs[h]
            den = (jnp.sum(s[h], axis=-1, keepdims=True)
                   + inter[h] * jnp.sum(qc * n_st, axis=-1, keepdims=True))
            hh = num / jnp.maximum(jnp.abs(den), jnp.exp(-m_t[h]))
            out.append((scale[h] * c_st + kv[h],
                        scale[h] * n_st + jnp.sum(wk[h], axis=0, keepdims=True),
                        m_new[h]))
            o_ref[rows, COLS[h]] = (_sigmoid(og_ref[rows, COLS[h]].astype(F32))
                                    * _rms(hh, nw)).astype(o_ref.dtype)
        return tuple(out)

    init = tuple((jnp.zeros((HEAD_W, HEAD_W), F32), jnp.zeros((1, HEAD_W), F32), jnp.zeros((1, 1), F32))
                 for _ in range(ML_HEADS))
    lax.fori_loop(0, n_chunks, lambda ci, carry: advance(*local(ci), carry), init)


def mlstm(zb, zs, conv_ml, i_bias, f_bias, ml_norm, batch, seq):
    m = zb.shape[0]
    width = ML_HEADS * HEAD_W
    blk = lambda cb: pl.BlockSpec((seq, width), lambda b: (b, cb // ML_HEADS))
    return pl.pallas_call(
        _mlstm_kernel,
        grid=(batch,),
        in_specs=[pl.BlockSpec((8, LANE), lambda b: (0, 0)),
                  blk(CB_ML_Q), blk(CB_ML_K), blk(CB_ML_V), blk(CB_ML_O),
                  pl.BlockSpec((seq, LANE), lambda b: (b, 0)),
                  pl.BlockSpec((CONV_K, 2 * width), lambda b: (0, 0)),
                  pl.BlockSpec((1, HEAD_W), lambda b: (0, 0))],
        out_specs=pl.BlockSpec((seq, width), lambda b: (b, 0)),
        out_shape=jax.ShapeDtypeStruct((m, width), BF16),
        compiler_params=_params(("parallel",)),
        name="mlstm",
    )(_lane_rows((SL_ML_I, i_bias), (SL_ML_F, f_bias)), zb, zb, zb, zb, zs, conv_ml,
      ml_norm.reshape(1, HEAD_W))


LOG2E = math.log2(math.e)
LOG2E_HI = float(np.asarray(LOG2E, dtype=BF16))
LOG2E_LO = LOG2E - LOG2E_HI
SLAB = 128
RAMP = 256


def _with_lanes(x, base, vals):
    lane = lax.broadcasted_iota(jnp.int32, x.shape, 1)
    for i, v in enumerate(vals):
        x = jnp.where(lane == base + i, v, x)
    return x


def _diff_attn_kernel(lq1_ref, lk1_ref, lq2_ref, lk2_ref, q_ref, k_ref, v_ref, nw_ref, o_ref,
                      k0_s, k1_s, *, tq, lambda_init):
    h = pl.program_id(1)
    qi = pl.program_id(2)
    tk = RAMP
    per_pair = tq // (2 * tk)
    assert k_ref.shape[0] <= RAMP * RAMP and tq % (2 * tk) == 0
    slope = jnp.exp2(jnp.full((1, 1), -(h + 1).astype(F32) * (8.0 / DIFF_HEADS), F32))

    @pl.when(qi == 0)
    def _():
        kk = k_ref[...]
        pos = lax.broadcasted_iota(jnp.int32, kk.shape, 0)
        fine = ((pos & (RAMP - 1)).astype(F32) * slope).astype(BF16)
        coarse = ((pos - (pos & (RAMP - 1))).astype(F32) * slope).astype(BF16)
        k0_s[...] = _with_lanes(kk, DIFF_DH, (fine, fine, coarse, coarse))
        k1_s[...] = _with_lanes(kk, 0, (fine, fine, coarse, coarse))

    lam = (jnp.exp(jnp.sum(lq1_ref[...] * lk1_ref[...], axis=-1, keepdims=True))
           - jnp.exp(jnp.sum(lq2_ref[...] * lk2_ref[...], axis=-1, keepdims=True)) + lambda_init)
    lane = lax.broadcasted_iota(jnp.int32, (tq, HEAD_W), 1)
    q = q_ref[...].astype(F32) * (DIFF_DH ** -0.5 * LOG2E)
    l2e = (LOG2E_HI, LOG2E_LO, LOG2E_HI, LOG2E_LO)
    q0 = _with_lanes(jnp.where(lane < DIFF_DH, q, 0.0), DIFF_DH, l2e).astype(BF16)
    q1 = _with_lanes(jnp.where(lane >= DIFF_DH, q, 0.0), 0, l2e).astype(BF16)
    per_map = tq // SLAB
    n_slab = 2 * per_map
    q_slab = [qm[j * SLAB:(j + 1) * SLAB, :] for qm in (q0, q1) for j in range(per_map)]

    def run_pair(kp, diag_at, carry):
        def needed(i, blk):
            return diag_at is None or diag_at + blk * tk <= (i % per_map) * SLAB + SLAB - 1

        scores, values = [], []
        for blk in range(2):
            rows = pl.ds(pl.multiple_of((2 * kp + blk) * tk, tk), tk)
            kb = (k0_s[rows, :], k1_s[rows, :])
            values.append(v_ref[rows, :])
            scores.append([_dot_nt(q_slab[i], kb[i // per_map]) if needed(i, blk) else None
                           for i in range(n_slab)])
        for blk in range(2):
            out = []
            for i in range(n_slab):
                if not needed(i, blk):
                    out.append(carry[i])
                    continue
                m_i, l_i, acc = carry[i]
                s = scores[blk][i]
                if diag_at is not None and diag_at + (blk + 1) * tk - 1 > (i % per_map) * SLAB:
                    ri = lax.broadcasted_iota(jnp.int32, (SLAB, tk), 0) + (i % per_map) * SLAB
                    ci = lax.broadcasted_iota(jnp.int32, (SLAB, tk), 1) + (diag_at + blk * tk)
                    s = jnp.where(ci <= ri, s, NEG)
                m_new = jnp.maximum(m_i, jnp.max(s, axis=-1, keepdims=True))
                p = jnp.exp2(s - m_new)
                alpha = jnp.exp2(m_i - m_new)
                l_new = alpha * l_i + sum(p[:, j * LANE:(j + 1) * LANE] for j in range(tk // LANE))
                acc_new = alpha * acc + jnp.dot(p.astype(BF16), values[blk], preferred_element_type=F32)
                out.append((m_new, l_new, acc_new))
            carry = tuple(out)
        return carry

    init = tuple((jnp.full((SLAB, 1), NEG, F32), jnp.zeros((SLAB, LANE), F32), jnp.zeros((SLAB, HEAD_W), F32))
                 for _ in range(n_slab))
    def run_wide(kp, carry):
        rows = pl.ds(pl.multiple_of(2 * kp * tk, 2 * tk), 2 * tk)
        kb = (k0_s[rows, :], k1_s[rows, :])
        vb = v_ref[rows, :]
        scores = [_dot_nt(q_slab[i], kb[i // per_map]) for i in range(n_slab)]
        out = []
        for i in range(n_slab):
            m_i, l_i, acc = carry[i]
            s = scores[i]
            m_new = jnp.maximum(m_i, jnp.max(s, axis=-1, keepdims=True))
            p = jnp.exp2(s - m_new)
            alpha = jnp.exp2(m_i - m_new)
            l_new = alpha * l_i + sum(p[:, j * LANE:(j + 1) * LANE] for j in range(2 * tk // LANE))
            acc_new = alpha * acc + jnp.dot(p.astype(BF16), vb, preferred_element_type=F32)
            out.append((m_new, l_new, acc_new))
        return tuple(out)

    carry = lax.fori_loop(0, qi * per_pair, run_wide, init)
    for t in range(per_pair):
        carry = run_pair(qi * per_pair + t, t * 2 * tk, carry)
    nw = nw_ref[...]
    for j in range(per_map):
        (_, l0, a0), (_, l1, a1) = carry[j], carry[per_map + j]
        l0 = jnp.sum(l0, axis=-1, keepdims=True)
        l1 = jnp.sum(l1, axis=-1, keepdims=True)
        o = a0 / l0 - lam * (a1 / l1)
        o_ref[j * SLAB:(j + 1) * SLAB, :] = (_rms(o, nw) * (1.0 - lambda_init)).astype(o_ref.dtype)


def diff_attention(zb, lq1, lk1, lq2, lk2, diff_norm, lambda_init, batch, seq, tq=512):
    m = zb.shape[0]
    tq = _tile(seq, tq)
    nq = seq // tq
    vec = pl.BlockSpec((1, DIFF_DH), lambda b, h, i: (0, 0))
    return pl.pallas_call(
        functools.partial(_diff_attn_kernel, tq=tq, lambda_init=lambda_init),
        grid=(batch, DIFF_HEADS, nq),
        in_specs=[vec, vec, vec, vec,
                  pl.BlockSpec((tq, HEAD_W), lambda b, h, i: (b * nq + i, CB_AQ + h)),
                  pl.BlockSpec((seq, HEAD_W), lambda b, h, i: (b, CB_AK + h)),
                  pl.BlockSpec((seq, HEAD_W), lambda b, h, i: (b, CB_AV + h)),
                  pl.BlockSpec((1, HEAD_W), lambda b, h, i: (0, 0))],
        out_specs=pl.BlockSpec((tq, HEAD_W), lambda b, h, i: (b * nq + i, h)),
        out_shape=jax.ShapeDtypeStruct((m, DIFF_HEADS * HEAD_W), BF16),
        scratch_shapes=[pltpu.VMEM((seq, HEAD_W), BF16), pltpu.VMEM((seq, HEAD_W), BF16)],
        compiler_params=_params(("parallel", "parallel", "arbitrary")),
        name="diff_attn",
    )(lq1.reshape(1, -1), lk1.reshape(1, -1), lq2.reshape(1, -1), lk2.reshape(1, -1),
      zb, zb, zb, diff_norm.reshape(1, HEAD_W))


def _out_proj_kernel(a_ref, b_ref, c_ref, w_ref, h_ref, o_ref):
    r0, r1 = a_ref.shape[1], a_ref.shape[1] + b_ref.shape[1]
    acc = jnp.dot(a_ref[...], w_ref[:r0, :], preferred_element_type=F32)
    acc += jnp.dot(b_ref[...], w_ref[r0:r1, :], preferred_element_type=F32)
    acc += jnp.dot(c_ref[...], w_ref[r1:, :], preferred_element_type=F32)
    o_ref[...] = h_ref[...] + acc


def out_proj(o_dn, o_diff, o_ml, w, h, tm=512):
    m, d = h.shape
    tm = _tile(m, tm)
    row = lambda width: pl.BlockSpec((tm, width), lambda i: (i, 0))
    return pl.pallas_call(
        _out_proj_kernel,
        grid=(m // tm,),
        in_specs=[row(o_dn.shape[1]), row(o_diff.shape[1]), row(o_ml.shape[1]),
                  pl.BlockSpec(w.shape, lambda i: (0, 0)), row(d)],
        out_specs=row(d),
        out_shape=jax.ShapeDtypeStruct((m, d), F32),
        compiler_params=_params(("parallel",)),
        name="out_proj",
    )(o_dn, o_diff, o_ml, w, h)


RING = 2


def _ffn_kernel(te_ref, na_ref, x_hbm, nw_ref, wg_hbm, wu_hbm, wd_hbm, o_ref, x_b, xs_ref, wg_b, wu_b, wd_b,
                sems, x_sem, *, fuse_norm):
    i, f = pl.program_id(0), pl.program_id(1)
    nf = pl.num_programs(1)
    tm = x_b.shape[0]
    tf = wg_b.shape[2]
    step = i * nf + f
    total = na_ref[0] * nf

    def copies(s):
        e = te_ref[s // nf]
        cols = pl.ds(pl.multiple_of((s % nf) * tf, tf), tf)
        slot = s % RING
        return (pltpu.make_async_copy(wg_hbm.at[e, :, cols], wg_b.at[slot], sems.at[slot, 0]),
                pltpu.make_async_copy(wu_hbm.at[e, :, cols], wu_b.at[slot], sems.at[slot, 1]),
                pltpu.make_async_copy(wd_hbm.at[e, cols, :], wd_b.at[slot], sems.at[slot, 2]))

    def start(s):
        @pl.when(s < total)
        def _():
            for c in copies(s):
                c.start()

    def x_copy(tile):
        return pltpu.make_async_copy(x_hbm.at[pl.ds(pl.multiple_of(tile * tm, tm), tm), :], x_b, x_sem)

    @pl.when(step == 0)
    def _():
        x_copy(0).start()
        for s in range(RING - 1):
            start(s)

    @pl.when((f == 1) & (i + 1 < pl.num_programs(0)))
    def _():
        x_copy(i + 1).start()

    @pl.when(f == 0)
    def _():
        x_copy(i).wait()
        x = x_b[...]
        if fuse_norm:
            xs_ref[...] = _rms(x, nw_ref[...]).astype(BF16)
            o_ref[...] = x
        else:
            xs_ref[...] = x.astype(BF16)
            o_ref[...] = jnp.zeros_like(o_ref)

    @pl.when(i < na_ref[0])
    def _():
        start(step + RING - 1)
        for c in copies(step):
            c.wait()
        slot = step % RING
        xs = xs_ref[...]
        g = jnp.dot(xs, wg_b[slot].astype(BF16), preferred_element_type=F32)
        u = jnp.dot(xs, wu_b[slot].astype(BF16), preferred_element_type=F32)
        o_ref[...] += jnp.dot((_silu(g) * u).astype(BF16), wd_b[slot].astype(BF16), preferred_element_type=F32)


def ffn(x, nw, tile_expert, n_active, wg, wu, wd, tm, tf, fuse_norm, name):
    rows, d = x.shape
    dff = wg.shape[2]
    tf = _tile(dff, tf)
    any_spec = pl.BlockSpec(memory_space=pl.ANY)
    return pl.pallas_call(
        functools.partial(_ffn_kernel, fuse_norm=fuse_norm),
        grid_spec=pltpu.PrefetchScalarGridSpec(
            num_scalar_prefetch=2,
            grid=(rows // tm, dff // tf),
            in_specs=[any_spec,
                      pl.BlockSpec((1, d), lambda i, f, te, na: (0, 0)),
                      any_spec, any_spec, any_spec],
            out_specs=pl.BlockSpec((tm, d), lambda i, f, te, na: (i, 0), pipeline_mode=pl.Buffered(1)),
            scratch_shapes=[pltpu.VMEM((tm, d), F32), pltpu.VMEM((tm, d), BF16),
                            pltpu.VMEM((RING, d, tf), F32), pltpu.VMEM((RING, d, tf), F32),
                            pltpu.VMEM((RING, tf, d), F32), pltpu.SemaphoreType.DMA((RING, 3)),
                            pltpu.SemaphoreType.DMA(())]),
        out_shape=jax.ShapeDtypeStruct((rows, d), F32),
        compiler_params=_params(("arbitrary", "arbitrary")),
        name=name,
    )(tile_expert, n_active, x, nw.reshape(1, d), wg, wu, wd)


def dense_ffn(h, nw, wg, wu, wd, tm=1024, tf=512):
    tm = _tile(h.shape[0], tm)
    n_tiles = h.shape[0] // tm
    return ffn(h, nw, jnp.zeros((n_tiles,), jnp.int32), jnp.full((1,), n_tiles, jnp.int32),
               wg[None], wu[None], wd[None], tm, tf, True, "dense_ffn")


def _router_kernel(h_ref, nw_ref, wr_ref, xs_ref, idx_ref, wts_ref, cnt_ref, carry_ref):
    i = pl.program_id(0)
    tm = h_ref.shape[0]

    @pl.when(i == 0)
    def _():
        carry_ref[...] = jnp.zeros_like(carry_ref)

    xn = _rms(h_ref[...], nw_ref[...])
    xs_ref[...] = xn
    logits = _dot_hi(xn, wr_ref[...])
    lane = lax.broadcasted_iota(jnp.int32, logits.shape, 1)
    logits = jnp.where(lane < N_EXPERTS, logits, NEG)
    m1 = jnp.max(logits, axis=-1, keepdims=True)
    e1 = jnp.min(jnp.where(logits == m1, lane, LANE), axis=-1, keepdims=True)
    rest = jnp.where(lane == e1, NEG, logits)
    m2 = jnp.max(rest, axis=-1, keepdims=True)
    e2 = jnp.min(jnp.where(rest == m2, lane, LANE), axis=-1, keepdims=True)
    ex = jnp.exp(m2 - m1)
    w1 = 1.0 / (1.0 + ex)
    w2 = ex * w1
    oh1 = (lane == e1).astype(F32)
    oh2 = (lane == e2).astype(F32)
    oh = oh1 + oh2
    r = lax.broadcasted_iota(jnp.int32, (tm, tm), 0)
    c = lax.broadcasted_iota(jnp.int32, (tm, tm), 1)
    before = jnp.dot((r > c).astype(BF16), oh.astype(BF16), preferred_element_type=F32) + carry_ref[...]
    rank1 = jnp.sum(before * oh1, axis=-1, keepdims=True)
    rank2 = jnp.sum(before * oh2, axis=-1, keepdims=True)
    carry_ref[...] += jnp.sum(oh, axis=0, keepdims=True)
    cnt_ref[...] = jnp.broadcast_to(carry_ref[...], cnt_ref.shape)
    idx = jnp.where(lane == 0, e1, jnp.where(lane == 1, e2, 0))
    idx = jnp.where(lane == 2, rank1.astype(jnp.int32), jnp.where(lane == 3, rank2.astype(jnp.int32), idx))
    idx_ref[...] = idx
    wts_ref[...] = jnp.where(lane == 0, w1, jnp.where(lane == 1, w2, 0.0))


def moe_router(h, nw, router, tm=512):
    m, d = h.shape
    tm = _tile(m, tm)
    wr = jnp.zeros((d, LANE), F32).at[:, :N_EXPERTS].set(router)
    return pl.pallas_call(
        _router_kernel,
        grid=(m // tm,),
        in_specs=[pl.BlockSpec((tm, d), lambda i: (i, 0)),
                  pl.BlockSpec((1, d), lambda i: (0, 0)),
                  pl.BlockSpec((d, LANE), lambda i: (0, 0))],
        out_specs=[pl.BlockSpec((tm, d), lambda i: (i, 0)),
                   pl.BlockSpec((tm, LANE), lambda i: (i, 0)),
                   pl.BlockSpec((tm, LANE), lambda i: (i, 0)),
                   pl.BlockSpec((8, LANE), lambda i: (0, 0))],
        out_shape=[jax.ShapeDtypeStruct((m, d), F32),
                   jax.ShapeDtypeStruct((m, LANE), jnp.int32),
                   jax.ShapeDtypeStruct((m, LANE), F32),
                   jax.ShapeDtypeStruct((8, LANE), F32)],
        scratch_shapes=[pltpu.VMEM((1, LANE), F32)],
        compiler_params=_params(("arbitrary",)),
        name="moe_router",
    )(h, nw.reshape(1, d), wr)


ROW_UNROLL = 8


def _dispatch_kernel(dest_ref, xs_ref, init_ref, out_ref, sems):
    del init_ref
    tm = xs_ref.shape[0]

    def start(t, c):
        for k in range(2):
            pltpu.make_async_copy(xs_ref.at[pl.ds(t, 1), :], out_ref.at[pl.ds(dest_ref[2 * t + k], 1), :],
                                  sems.at[k]).start(priority=k)
        return c

    lax.fori_loop(0, tm, start, 0, unroll=ROW_UNROLL)
    for k in range(2):
        pltpu.make_async_copy(xs_ref, out_ref.at[pl.ds(0, tm), :], sems.at[k]).wait()


def moe_dispatch(xs, dest, n_sorted, tm=256):
    m, d = xs.shape
    tm = _tile(m, tm)
    init = jnp.zeros((n_sorted, d), xs.dtype)
    return pl.pallas_call(
        _dispatch_kernel,
        grid=(m // tm,),
        in_specs=[pl.BlockSpec((2 * tm,), lambda i: (i,), memory_space=pltpu.SMEM),
                  pl.BlockSpec((tm, d), lambda i: (i, 0)),
                  pl.BlockSpec(memory_space=pl.ANY)],
        out_specs=pl.BlockSpec(memory_space=pl.ANY),
        out_shape=jax.ShapeDtypeStruct((n_sorted, d), xs.dtype),
        scratch_shapes=[pltpu.SemaphoreType.DMA((2,))],
        input_output_aliases={2: 0},
        compiler_params=_params(("arbitrary",)),
        name="moe_dispatch",
    )(dest, xs, init)


def _combine_kernel(dest_ref, ys_ref, h_ref, wts_ref, o_ref, y_s, sems):
    i = pl.program_id(0)
    tm = h_ref.shape[0]

    def gather(tile, slot):
        def start(t, c):
            for k in range(2):
                src = dest_ref[2 * (tile * tm + t) + k]
                pltpu.make_async_copy(ys_ref.at[pl.ds(src, 1), :], y_s.at[slot, k, pl.ds(t, 1), :],
                                      sems.at[slot, k]).start(priority=k)
            return c

        lax.fori_loop(0, tm, start, 0, unroll=ROW_UNROLL)

    @pl.when(i == 0)
    def _():
        gather(0, 0)

    @pl.when(i + 1 < pl.num_programs(0))
    def _():
        gather(i + 1, (i + 1) % 2)

    slot = i % 2
    for k in range(2):
        pltpu.make_async_copy(ys_ref.at[pl.ds(0, tm), :], y_s.at[slot, k], sems.at[slot, k]).wait()
    wts = wts_ref[...]
    o_ref[...] = h_ref[...] + wts[:, 0:1] * y_s[slot, 0] + wts[:, 1:2] * y_s[slot, 1]


def moe_combine(ysorted, dest, h, wts, tm=256):
    m, d = h.shape
    tm = _tile(m, tm)
    return pl.pallas_call(
        _combine_kernel,
        grid_spec=pltpu.PrefetchScalarGridSpec(
            num_scalar_prefetch=1,
            grid=(m // tm,),
            in_specs=[pl.BlockSpec(memory_space=pl.ANY),
                      pl.BlockSpec((tm, d), lambda i, dest: (i, 0)),
                      pl.BlockSpec((tm, LANE), lambda i, dest: (i, 0))],
            out_specs=pl.BlockSpec((tm, d), lambda i, dest: (i, 0)),
            scratch_shapes=[pltpu.VMEM((2, 2, tm, d), F32), pltpu.SemaphoreType.DMA((2, 2))]),
        out_shape=jax.ShapeDtypeStruct((m, d), F32),
        compiler_params=_params(("arbitrary",)),
        name="moe_combine",
    )(dest, ysorted, h, wts)


def moe_ffn(h, nw, router, wg, wu, wd, tmg=1024):
    m, d = h.shape
    tmg = min(tmg, m)
    xs, idx, wts, cnt = moe_router(h, nw, router)
    counts = cnt[0, :N_EXPERTS].astype(jnp.int32)
    padded = (counts + tmg - 1) // tmg * tmg
    ends = jnp.cumsum(padded)
    offsets = ends - padded
    n_tiles = (2 * m) // tmg + N_EXPERTS
    n_sorted = n_tiles * tmg
    dest = (offsets[idx[:, 0:2]] + idx[:, 2:4]).reshape(-1)
    tile_start = jnp.arange(n_tiles, dtype=jnp.int32) * tmg
    tile_expert = jnp.minimum(jnp.sum(tile_start[:, None] >= ends[None, :], axis=1), N_EXPERTS - 1)
    n_active = (ends[-1] // tmg).reshape(1).astype(jnp.int32)
    xsorted = moe_dispatch(xs, dest, n_sorted)
    ysorted = ffn(xsorted, nw, tile_expert.astype(jnp.int32), n_active, wg, wu, wd, tmg, 512, False, "grouped_ffn")
    return moe_combine(ysorted, dest, h, wts)


def _ple_kernel(h_ref, p_ref, nw_ref, wg_ref, wp_ref, fw_ref, o_ref, *, final):
    h = h_ref[...]
    gate = _sigmoid(jnp.dot(_rms(h, nw_ref[...]).astype(BF16), wg_ref[...], preferred_element_type=F32))
    out = h + jnp.dot(p_ref[...].astype(BF16), wp_ref[...], preferred_element_type=F32) * gate
    o_ref[...] = _rms(out, fw_ref[...]) if final else out


def ple(h, p, nw, wg, wp, fw, final, tm=512):
    m, d = h.shape
    pd = p.shape[1]
    tm = _tile(m, tm)
    return pl.pallas_call(
        functools.partial(_ple_kernel, final=final),
        grid=(m // tm,),
        in_specs=[pl.BlockSpec((tm, d), lambda i: (i, 0)),
                  pl.BlockSpec((tm, pd), lambda i: (i, 0)),
                  pl.BlockSpec((1, d), lambda i: (0, 0)),
                  pl.BlockSpec((d, d), lambda i: (0, 0)),
                  pl.BlockSpec((pd, d), lambda i: (0, 0)),
                  pl.BlockSpec((1, d), lambda i: (0, 0))],
        out_specs=pl.BlockSpec((tm, d), lambda i: (i, 0)),
        out_shape=jax.ShapeDtypeStruct((m, d), F32),
        compiler_params=_params(("parallel",)),
        name="ple",
    )(h, p, nw.reshape(1, d), wg, wp, fw.reshape(1, d))


def _split_w_in(w):
    d = w.shape[0]
    dn_w, diff_w, ml_w = DN_HEADS * HEAD_W, DIFF_HEADS * HEAD_W, ML_HEADS * HEAD_W
    g0 = 3 * dn_w + 2 * ml_w + dn_w
    g1 = g0 + 2 * DN_HEADS
    g2 = g1 + 3 * diff_w + 2 * ml_w
    g3 = g2 + 2 * ML_HEADS
    assert w.shape[1] == g3 and g0 + g2 - g1 == N_BIG
    big = jnp.concatenate([w[:, :g0], w[:, g1:g2]], axis=1).astype(BF16)
    small = jnp.concatenate([w[:, g0:g1], w[:, g2:g3],
                             jnp.zeros((d, LANE - (g1 - g0) - (g3 - g2)), w.dtype)], axis=1).astype(BF16)
    return big, small


def kernel(x, p, attn_norm, w_in, conv_dn, conv_ml, dn_a_log, dn_dt_bias, dn_norm, diff_lq1, diff_lk1, diff_lq2, diff_lk2, diff_norm, ml_i_bias, ml_f_bias, ml_norm, w_out, ffn_norm, dense_w_gate, dense_w_up, dense_w_down, router, moe_w_gate, moe_w_up, moe_w_down, ple_norm, ple_proj, ple_gate, final_norm):
    batch, seq, d = x.shape
    depth = w_in.shape[0]
    m = batch * seq
    h = x.reshape(m, d)
    for i in range(depth):
        lambda_init = 0.8 - 0.6 * math.exp(-0.3 * i)
        w_big, w_small = _split_w_in(w_in[i])
        zb, zs = norm_proj(h, attn_norm[i], w_big, w_small)
        o_dn = deltanet(zb, zs, conv_dn[i], dn_a_log[i], dn_dt_bias[i], dn_norm[i], batch, seq)
        o_diff = diff_attention(zb, diff_lq1[i], diff_lk1[i], diff_lq2[i], diff_lk2[i], diff_norm[i],
                                lambda_init, batch, seq)
        o_ml = mlstm(zb, zs, conv_ml[i], ml_i_bias[i], ml_f_bias[i], ml_norm[i], batch, seq)
        h = out_proj(o_dn, o_diff, o_ml, w_out[i].astype(BF16), h)
        j = i // 2
        if i % 2 == 0:
            h = dense_ffn(h, ffn_norm[i], dense_w_gate[j], dense_w_up[j], dense_w_down[j])
        else:
            h = moe_ffn(h, ffn_norm[i], router[j], moe_w_gate[j], moe_w_up[j], moe_w_down[j])
        h = ple(h, p[i].reshape(m, -1), ple_norm[i], ple_gate[i].astype(BF16), ple_proj[i].astype(BF16),
                final_norm, final=(i == depth - 1))
    return h.reshape(batch, seq, d)
```

```python
import functools
import math

import numpy as np
import jax
import jax.numpy as jnp
from jax import lax
from jax.experimental import pallas as pl
from jax.experimental.pallas import tpu as pltpu

F32 = jnp.float32
BF16 = jnp.bfloat16
EPS = 1e-6
LANE = 128
NEG = -1e30

DN_HEADS = 4
DIFF_HEADS = 8
ML_HEADS = 4
HEAD_W = 128
DIFF_DH = 64
CHUNK = 64
CONV_K = 4
N_EXPERTS = 8
VMEM_LIMIT = 56 * 1024 * 1024

CB_DN_Q, CB_DN_K, CB_DN_V = 0, 4, 8
CB_ML_Q, CB_ML_K = 12, 16
CB_DN_Z = 20
CB_AQ, CB_AK, CB_AV = 24, 32, 40
CB_ML_V, CB_ML_O = 48, 52
N_BIG = 56 * LANE
SL_DN_B, SL_DN_A, SL_ML_I, SL_ML_F = 0, 4, 8, 12


def _params(sem):
    return pltpu.CompilerParams(dimension_semantics=sem, vmem_limit_bytes=VMEM_LIMIT)


def _dot(a, b):
    return jnp.dot(a.astype(BF16), b.astype(BF16), preferred_element_type=F32)


def _dot_nt(a, b):
    return lax.dot_general(a.astype(BF16), b.astype(BF16), (((1,), (1,)), ((), ())),
                           preferred_element_type=F32)


def _dot_tn(a, b):
    return lax.dot_general(a.astype(BF16), b.astype(BF16), (((0,), (0,)), ((), ())),
                           preferred_element_type=F32)


def _dot_hi(a, b):
    return jnp.dot(a, b, preferred_element_type=F32, precision=lax.Precision.HIGHEST)


def _rms(x, w):
    return x * lax.rsqrt(jnp.mean(x * x, axis=-1, keepdims=True) + EPS) * w


def _sigmoid(x):
    return 1.0 / (1.0 + jnp.exp(-x))


def _silu(x):
    return x * _sigmoid(x)


def _softplus(x):
    return jnp.maximum(x, 0.0) + jnp.log(1.0 + jnp.exp(-jnp.abs(x)))


def _log_sigmoid(x):
    return -_softplus(-x)


def _tile(m, t):
    t = min(m, t)
    assert m % t == 0
    return t


def _norm_proj_kernel(x_ref, nw_ref, w_ref, ws_ref, o_ref, os_ref, xs_ref):
    @pl.when(pl.program_id(1) == 0)
    def _():
        xs_ref[...] = _rms(x_ref[...], nw_ref[...]).astype(BF16)
        os_ref[...] = jnp.dot(xs_ref[...], ws_ref[...], preferred_element_type=F32)

    o_ref[...] = jnp.dot(xs_ref[...], w_ref[...], preferred_element_type=F32).astype(o_ref.dtype)


def norm_proj(x, nw, w_big, w_small, tm=1024, tn=1792):
    m, d = x.shape
    n = w_big.shape[1]
    tm, tn = _tile(m, tm), _tile(n, tn)
    return pl.pallas_call(
        _norm_proj_kernel,
        grid=(m // tm, n // tn),
        in_specs=[pl.BlockSpec((tm, d), lambda i, j: (i, 0)),
                  pl.BlockSpec((1, d), lambda i, j: (0, 0)),
                  pl.BlockSpec((d, tn), lambda i, j: (0, j)),
                  pl.BlockSpec((d, LANE), lambda i, j: (0, 0))],
        out_specs=[pl.BlockSpec((tm, tn), lambda i, j: (i, j)),
                   pl.BlockSpec((tm, LANE), lambda i, j: (i, 0))],
        out_shape=[jax.ShapeDtypeStruct((m, n), BF16), jax.ShapeDtypeStruct((m, LANE), F32)],
        scratch_shapes=[pltpu.VMEM((tm, d), BF16)],
        compiler_params=_params(("parallel", "arbitrary")),
        name="norm_proj",
    )(x, nw.reshape(1, d), w_big, w_small)


def _chunk_conv_silu(ref, w, ci, rows):
    cur = ref[rows, :].astype(F32)
    prev_rows = pl.ds(pl.multiple_of(jnp.maximum(ci * CHUNK - 16, 0), 16), 16)
    prev = jnp.where(ci > 0, ref[prev_rows, :].astype(F32)[8:16, :], 0.0)
    ext = jnp.concatenate([prev, cur], axis=0)
    y = cur * w[CONV_K - 1:CONV_K, :]
    for back in range(1, CONV_K):
        y = y + pltpu.roll(ext, back, axis=0)[8:8 + CHUNK, :] * w[CONV_K - 1 - back:CONV_K - back, :]
    return _silu(y)


def _dot2(l_bf16, x):
    hi = x.astype(BF16)
    lo = (x - hi.astype(F32)).astype(BF16)
    return (jnp.dot(l_bf16, hi, preferred_element_type=F32)
            + jnp.dot(l_bf16, lo, preferred_element_type=F32))


def _lane_rows(*rows):
    out = jnp.zeros((8, LANE), F32)
    for r, (lane0, vals) in enumerate(rows):
        out = out.at[r, lane0:lane0 + vals.shape[0]].set(vals.astype(F32))
    return out


def _chunk_masks():
    r = lax.broadcasted_iota(jnp.int32, (CHUNK, CHUNK), 0)
    c = lax.broadcasted_iota(jnp.int32, (CHUNK, CHUNK), 1)
    return r, c


def _chunk_rows(ci):
    return pl.ds(pl.multiple_of(ci * CHUNK, CHUNK), CHUNK)


assert DN_HEADS == ML_HEADS
HEADS = range(DN_HEADS)
COLS = [slice(h * HEAD_W, (h + 1) * HEAD_W) for h in HEADS]
UNROLL = 4


def _deltanet_kernel(gp_ref, q_ref, k_ref, v_ref, zg_ref, zs_ref, cw_ref, nw_ref, o_ref,
                     u_s, w_s, at_s, qg_s, kg_s, gl_s):
    n_chunks = q_ref.shape[0] // CHUNK
    width = DN_HEADS * HEAD_W
    r, c = _chunk_masks()
    low_incl = (r >= c).astype(BF16)
    strict_up = (r > c).astype(F32)
    tri_incl = r >= c
    tri_strict = r > c
    cw = cw_ref[...]
    rate_row = jnp.exp(gp_ref[0:1, :])
    bias_row = gp_ref[1:2, :]

    def prep(cp, carry):
        units, qc, kc, kb, kcb, vbeta, g = [], [], [], [], [], [], []
        for t in range(UNROLL):
            ci = UNROLL * cp + t
            rows = _chunk_rows(ci)
            qa = _chunk_conv_silu(q_ref, cw[:, :width], ci, rows)
            ka = _chunk_conv_silu(k_ref, cw[:, width:2 * width], ci, rows)
            va = _chunk_conv_silu(v_ref, cw[:, 2 * width:], ci, rows)
            zs = zs_ref[rows, :]
            sig = _sigmoid(zs)
            g_all = -rate_row * _softplus(zs + bias_row)
            for h in HEADS:
                units.append((ci, rows, h))
                qh, kh = qa[:, COLS[h]], ka[:, COLS[h]]
                qc.append(qh * (lax.rsqrt(jnp.sum(qh * qh, axis=-1, keepdims=True) + EPS) * (HEAD_W ** -0.5)))
                kc.append(kh * lax.rsqrt(jnp.sum(kh * kh, axis=-1, keepdims=True) + EPS))
                beta = sig[:, SL_DN_B + h:SL_DN_B + h + 1]
                g.append(g_all[:, SL_DN_A + h:SL_DN_A + h + 1])
                kb.append(kc[-1] * beta)
                kcb.append(kc[-1].astype(BF16))
                vbeta.append(va[:, COLS[h]] * beta)
        us = range(len(units))
        dmat = [_dot2(low_incl, jnp.broadcast_to(g[u], (CHUNK, CHUNK)) * strict_up) for u in us]
        kk = [_dot_nt(kb[u], kcb[u]) for u in us]
        qk = [_dot_nt(qc[u], kcb[u]) for u in us]
        gc = [dmat[u][:, :1] + g[u][:1, :] for u in us]
        decay = [jnp.where(tri_incl, jnp.exp(dmat[u]), 0.0) for u in us]
        eg = [jnp.exp(gc[u]) for u in us]
        a = [jnp.where(tri_strict, kk[u] * decay[u], 0.0).astype(BF16) for u in us]
        x = [jnp.concatenate([vbeta[u], kb[u] * eg[u]], axis=1) for u in us]
        ax = [_dot(a[u], x[u]) for u in us]
        p = [_dot(a[u], a[u]).astype(BF16) for u in us]
        x = [x[u] - ax[u] for u in us]
        for step in range(5):
            px = [_dot(p[u], x[u]) for u in us]
            if step < 4:
                p = [_dot(p[u], p[u]).astype(BF16) for u in us]
            x = [x[u] + px[u] for u in us]
        for u, (ci, rows, h) in enumerate(units):
            at_s[rows, h * HEAD_W:h * HEAD_W + CHUNK] = (qk[u] * decay[u]).astype(BF16)
            u_s[rows, COLS[h]] = x[u][:, :HEAD_W]
            w_s[rows, COLS[h]] = x[u][:, HEAD_W:].astype(BF16)
            g_last = gc[u][CHUNK - 1:CHUNK, :]
            qg_s[rows, COLS[h]] = (qc[u] * eg[u]).astype(BF16)
            kg_s[rows, COLS[h]] = (kc[u] * jnp.exp(g_last - gc[u])).astype(BF16)
            gl_s[pl.ds(pl.multiple_of(ci * 8, 8), 8), COLS[h]] = jnp.broadcast_to(jnp.exp(g_last), (8, HEAD_W))
        return carry

    assert n_chunks % UNROLL == 0
    lax.fori_loop(0, n_chunks // UNROLL, prep, 0)

    nw = nw_ref[...]

    def scan(ci, states):
        rows = _chunk_rows(ci)
        sb = [states[h].astype(BF16) for h in HEADS]
        ws = [jnp.dot(w_s[rows, COLS[h]], sb[h], preferred_element_type=F32) for h in HEADS]
        qs = [jnp.dot(qg_s[rows, COLS[h]], sb[h], preferred_element_type=F32) for h in HEADS]
        vb = [(u_s[rows, COLS[h]] - ws[h]).astype(BF16) for h in HEADS]
        av = [jnp.dot(at_s[rows, h * HEAD_W:h * HEAD_W + CHUNK], vb[h], preferred_element_type=F32)
              for h in HEADS]
        kv = [_dot_tn(kg_s[rows, COLS[h]], vb[h]) for h in HEADS]
        out = []
        for h in HEADS:
            gl = gl_s[pl.ds(pl.multiple_of(ci * 8, 8), 1), COLS[h]]
            out.append(states[h] * gl + kv[h])
            o_ref[rows, COLS[h]] = (_rms(qs[h] + av[h], nw)
                                    * _silu(zg_ref[rows, COLS[h]].astype(F32))).astype(o_ref.dtype)
        return tuple(out)

    lax.fori_loop(0, n_chunks, scan, tuple(jnp.zeros((HEAD_W, HEAD_W), F32) for _ in range(DN_HEADS)))


def deltanet(zb, zs, conv_dn, a_log, dt_bias, dn_norm, batch, seq):
    m = zb.shape[0]
    width = DN_HEADS * HEAD_W
    blk = lambda cb: pl.BlockSpec((seq, width), lambda b: (b, cb // DN_HEADS))
    return pl.pallas_call(
        _deltanet_kernel,
        grid=(batch,),
        in_specs=[pl.BlockSpec((8, LANE), lambda b: (0, 0)),
                  blk(CB_DN_Q), blk(CB_DN_K), blk(CB_DN_V), blk(CB_DN_Z),
                  pl.BlockSpec((seq, LANE), lambda b: (b, 0)),
                  pl.BlockSpec((CONV_K, 3 * width), lambda b: (0, 0)),
                  pl.BlockSpec((1, HEAD_W), lambda b: (0, 0))],
        out_specs=pl.BlockSpec((seq, width), lambda b: (b, 0)),
        out_shape=jax.ShapeDtypeStruct((m, width), BF16),
        scratch_shapes=[pltpu.VMEM((seq, width), F32), pltpu.VMEM((seq, width), BF16),
                        pltpu.VMEM((seq, width), BF16), pltpu.VMEM((seq, width), BF16),
                        pltpu.VMEM((seq, width), BF16),
                        pltpu.VMEM((seq // CHUNK * 8, width), F32)],
        compiler_params=_params(("parallel",)),
        name="deltanet",
    )(_lane_rows((SL_DN_A, a_log), (SL_DN_A, dt_bias)), zb, zb, zb, zb, zs, conv_dn,
      dn_norm.reshape(1, HEAD_W))


def _mlstm_kernel(gp_ref, q_ref, k_ref, v_ref, og_ref, zs_ref, cw_ref, nw_ref, o_ref):
    n_chunks = q_ref.shape[0] // CHUNK
    width = ML_HEADS * HEAD_W
    r, c = _chunk_masks()
    low_incl = (r >= c).astype(BF16)
    strict_up = (r > c).astype(F32)
    eye = (r == c).astype(F32)
    tri_incl = r >= c
    nw = nw_ref[...]
    cw = cw_ref[...]

    def local(ci):
        rows = _chunk_rows(ci)
        qa = _chunk_conv_silu(q_ref, cw[:, :width], ci, rows)
        ka = _chunk_conv_silu(k_ref, cw[:, width:], ci, rows) * (HEAD_W ** -0.5)
        zs = zs_ref[rows, :]
        ip_all = zs + gp_ref[0:1, :]
        lf_all = _log_sigmoid(zs + gp_ref[1:2, :])
        qc = [qa[:, COLS[h]] for h in HEADS]
        kc = [ka[:, COLS[h]] for h in HEADS]
        ip = [jnp.broadcast_to(ip_all[:, SL_ML_I + h:SL_ML_I + h + 1], (CHUNK, HEAD_W)) for h in HEADS]
        lf = [jnp.broadcast_to(lf_all[:, SL_ML_F + h:SL_ML_F + h + 1], (CHUNK, HEAD_W)) for h in HEADS]
        dl = [_dot2(low_incl, lf[h][:, :CHUNK] * strict_up + ip[h][:, :CHUNK] * eye) for h in HEADS]
        bc = [_dot2(low_incl, lf[h]) for h in HEADS]
        qk = [_dot_nt(qc[h], kc[h]) for h in HEADS]
        heads = []
        for h in HEADS:
            b = bc[h]
            dlog = jnp.where(tri_incl, dl[h], NEG)
            b_last = b[CHUNK - 1:CHUNK, :]
            a = b_last - b + ip[h]
            heads.append(dict(qc=qc[h], kc=kc[h], qk=qk[h], b=b, dlog=dlog, b_last=b_last, a=a,
                              dmax=jnp.max(dlog, axis=-1, keepdims=True),
                              a_max=jnp.max(a, axis=0, keepdims=True)))
        return rows, heads

    def advance(rows, heads, carry):
        vc = [v_ref[rows, COLS[h]] for h in HEADS]
        s, wk, inter, m_t, m_new, scale = [], [], [], [], [], []
        for h in HEADS:
            t = heads[h]
            m_st = carry[h][2]
            m_t.append(jnp.maximum(t["dmax"], t["b"] + m_st))
            s.append(t["qk"] * jnp.exp(t["dlog"] - m_t[h][:, :CHUNK]))
            inter.append(jnp.exp(t["b"] + m_st - m_t[h]))
            m_new.append(jnp.maximum(t["b_last"] + m_st, t["a_max"]))
            scale.append(jnp.exp(t["b_last"] + m_st - m_new[h]))
            wk.append(t["kc"] * jnp.exp(t["a"] - m_new[h]))
        sv = [_dot(s[h], vc[h]) for h in HEADS]
        kv = [_dot_tn(wk[h], vc[h]) for h in HEADS]
        qcs = [_dot(heads[h]["qc"], carry[h][0]) for h in HEADS]
        out = []
        for h in HEADS:
            c_st, n_st, _ = carry[h]
            qc = heads[h]["qc"]
            num = sv[h] + inter[h] * qcs[h]
            den = (jnp.sum(s[h], axis=-1, keepdims=True)
                   + inter[h] * jnp.sum(qc * n_st, axis=-1, keepdims=True))
            hh = num / jnp.maximum(jnp.abs(den), jnp.exp(-m_t[h]))
            out.append((scale[h] * c_st + kv[h],
                        scale[h] * n_st + jnp.sum(wk[h], axis=0, keepdims=True),
                        m_new[h]))
            o_ref[rows, COLS[h]] = (_sigmoid(og_ref[rows, COLS[h]].astype(F32))
                                    * _rms(hh, nw)).astype(o_ref.dtype)
        return tuple(out)

    init = tuple((jnp.zeros((HEAD_W, HEAD_W), F32), jnp.zeros((1, HEAD_W), F32), jnp.zeros((1, HEAD_W), F32))
                 for _ in range(ML_HEADS))
    lax.fori_loop(0, n_chunks, lambda ci, carry: advance(*local(ci), carry), init)


def mlstm(zb, zs, conv_ml, i_bias, f_bias, ml_norm, batch, seq):
    m = zb.shape[0]
    width = ML_HEADS * HEAD_W
    blk = lambda cb: pl.BlockSpec((seq, width), lambda b: (b, cb // ML_HEADS))
    return pl.pallas_call(
        _mlstm_kernel,
        grid=(batch,),
        in_specs=[pl.BlockSpec((8, LANE), lambda b: (0, 0)),
                  blk(CB_ML_Q), blk(CB_ML_K), blk(CB_ML_V), blk(CB_ML_O),
                  pl.BlockSpec((seq, LANE), lambda b: (b, 0)),
                  pl.BlockSpec((CONV_K, 2 * width), lambda b: (0, 0)),
                  pl.BlockSpec((1, HEAD_W), lambda b: (0, 0))],
        out_specs=pl.BlockSpec((seq, width), lambda b: (b, 0)),
        out_shape=jax.ShapeDtypeStruct((m, width), BF16),
        compiler_params=_params(("parallel",)),
        name="mlstm",
    )(_lane_rows((SL_ML_I, i_bias), (SL_ML_F, f_bias)), zb, zb, zb, zb, zs, conv_ml,
      ml_norm.reshape(1, HEAD_W))


LOG2E = math.log2(math.e)
LOG2E_HI = float(np.asarray(LOG2E, dtype=BF16))
LOG2E_LO = LOG2E - LOG2E_HI
SLAB = 128
RAMP = 256


def _with_lanes(x, base, vals):
    lane = lax.broadcasted_iota(jnp.int32, x.shape, 1)
    for i, v in enumerate(vals):
        x = jnp.where(lane == base + i, v, x)
    return x


def _diff_attn_kernel(lq1_ref, lk1_ref, lq2_ref, lk2_ref, q_ref, k_ref, v_ref, nw_ref, o_ref,
                      k0_s, k1_s, *, tq, lambda_init):
    h = pl.program_id(1)
    qi = pl.program_id(2)
    tk = RAMP
    per_pair = tq // (2 * tk)
    assert k_ref.shape[0] <= RAMP * RAMP and tq % (2 * tk) == 0
    slope = jnp.exp2(jnp.full((1, 1), -(h + 1).astype(F32) * (8.0 / DIFF_HEADS), F32))

    @pl.when(qi == 0)
    def _():
        kk = k_ref[...]
        pos = lax.broadcasted_iota(jnp.int32, kk.shape, 0)
        fine = ((pos & (RAMP - 1)).astype(F32) * slope).astype(BF16)
        coarse = ((pos - (pos & (RAMP - 1))).astype(F32) * slope).astype(BF16)
        k0_s[...] = _with_lanes(kk, DIFF_DH, (fine, fine, coarse, coarse))
        k1_s[...] = _with_lanes(kk, 0, (fine, fine, coarse, coarse))

    lam = (jnp.exp(jnp.sum(lq1_ref[...] * lk1_ref[...], axis=-1, keepdims=True))
           - jnp.exp(jnp.sum(lq2_ref[...] * lk2_ref[...], axis=-1, keepdims=True)) + lambda_init)
    lane = lax.broadcasted_iota(jnp.int32, (tq, HEAD_W), 1)
    q = q_ref[...].astype(F32) * (DIFF_DH ** -0.5 * LOG2E)
    l2e = (LOG2E_HI, LOG2E_LO, LOG2E_HI, LOG2E_LO)
    q0 = _with_lanes(jnp.where(lane < DIFF_DH, q, 0.0), DIFF_DH, l2e).astype(BF16)
    q1 = _with_lanes(jnp.where(lane >= DIFF_DH, q, 0.0), 0, l2e).astype(BF16)
    per_map = tq // SLAB
    n_slab = 2 * per_map
    q_slab = [qm[j * SLAB:(j + 1) * SLAB, :] for qm in (q0, q1) for j in range(per_map)]

    def run_pair(kp, diag_at, carry):
        def needed(i, blk):
            return diag_at is None or diag_at + blk * tk <= (i % per_map) * SLAB + SLAB - 1

        scores, values = [], []
        for blk in range(2):
            rows = pl.ds(pl.multiple_of((2 * kp + blk) * tk, tk), tk)
            kb = (k0_s[rows, :], k1_s[rows, :])
            values.append(v_ref[rows, :])
            scores.append([_dot_nt(q_slab[i], kb[i // per_map]) if needed(i, blk) else None
                           for i in range(n_slab)])
        for blk in range(2):
            out = []
            for i in range(n_slab):
                if not needed(i, blk):
                    out.append(carry[i])
                    continue
                m_i, l_i, acc = carry[i]
                s = scores[blk][i]
                if diag_at is not None and diag_at + (blk + 1) * tk - 1 > (i % per_map) * SLAB:
                    ri = lax.broadcasted_iota(jnp.int32, (SLAB, tk), 0) + (i % per_map) * SLAB
                    ci = lax.broadcasted_iota(jnp.int32, (SLAB, tk), 1) + (diag_at + blk * tk)
                    s = jnp.where(ci <= ri, s, NEG)
                m_new = jnp.maximum(m_i, jnp.max(s, axis=-1, keepdims=True))
                p = jnp.exp2(s - m_new)
                alpha = jnp.exp2(m_i - m_new)
                l_new = alpha * l_i + sum(p[:, j * LANE:(j + 1) * LANE] for j in range(tk // LANE))
                acc_new = alpha * acc + jnp.dot(p.astype(BF16), values[blk], preferred_element_type=F32)
                out.append((m_new, l_new, acc_new))
            carry = tuple(out)
        return carry

    init = tuple((jnp.full((SLAB, 1), NEG, F32), jnp.zeros((SLAB, LANE), F32), jnp.zeros((SLAB, HEAD_W), F32))
                 for _ in range(n_slab))
    def run_wide(kp, carry):
        rows = pl.ds(pl.multiple_of(2 * kp * tk, 2 * tk), 2 * tk)
        kb = (k0_s[rows, :], k1_s[rows, :])
        vb = v_ref[rows, :]
        scores = [_dot_nt(q_slab[i], kb[i // per_map]) for i in range(n_slab)]
        out = []
        for i in range(n_slab):
            m_i, l_i, acc = carry[i]
            s = scores[i]
            m_new = jnp.maximum(m_i, jnp.max(s, axis=-1, keepdims=True))
            p = jnp.exp2(s - m_new)
            alpha = jnp.exp2(m_i - m_new)
            l_new = alpha * l_i + sum(p[:, j * LANE:(j + 1) * LANE] for j in range(2 * tk // LANE))
            acc_new = alpha * acc + jnp.dot(p.astype(BF16), vb, preferred_element_type=F32)
            out.append((m_new, l_new, acc_new))
        return tuple(out)

    carry = lax.fori_loop(0, qi * per_pair, run_wide, init)
    for t in range(per_pair):
        carry = run_pair(qi * per_pair + t, t * 2 * tk, carry)
    nw = nw_ref[...]
    for j in range(per_map):
        (_, l0, a0), (_, l1, a1) = carry[j], carry[per_map + j]
        l0 = jnp.sum(l0, axis=-1, keepdims=True)
        l1 = jnp.sum(l1, axis=-1, keepdims=True)
        o = a0 / l0 - lam * (a1 / l1)
        o_ref[j * SLAB:(j + 1) * SLAB, :] = (_rms(o, nw) * (1.0 - lambda_init)).astype(o_ref.dtype)


def diff_attention(zb, lq1, lk1, lq2, lk2, diff_norm, lambda_init, batch, seq, tq=512):
    m = zb.shape[0]
    tq = _tile(seq, tq)
    nq = seq // tq
    vec = pl.BlockSpec((1, DIFF_DH), lambda b, h, i: (0, 0))
    return pl.pallas_call(
        functools.partial(_diff_attn_kernel, tq=tq, lambda_init=lambda_init),
        grid=(batch, DIFF_HEADS, nq),
        in_specs=[vec, vec, vec, vec,
                  pl.BlockSpec((tq, HEAD_W), lambda b, h, i: (b * nq + i, CB_AQ + h)),
                  pl.BlockSpec((seq, HEAD_W), lambda b, h, i: (b, CB_AK + h)),
                  pl.BlockSpec((seq, HEAD_W), lambda b, h, i: (b, CB_AV + h)),
                  pl.BlockSpec((1, HEAD_W), lambda b, h, i: (0, 0))],
        out_specs=pl.BlockSpec((tq, HEAD_W), lambda b, h, i: (b * nq + i, h)),
        out_shape=jax.ShapeDtypeStruct((m, DIFF_HEADS * HEAD_W), BF16),
        scratch_shapes=[pltpu.VMEM((seq, HEAD_W), BF16), pltpu.VMEM((seq, HEAD_W), BF16)],
        compiler_params=_params(("parallel", "parallel", "arbitrary")),
        name="diff_attn",
    )(lq1.reshape(1, -1), lk1.reshape(1, -1), lq2.reshape(1, -1), lk2.reshape(1, -1),
      zb, zb, zb, diff_norm.reshape(1, HEAD_W))


def _out_proj_kernel(a_ref, b_ref, c_ref, w_ref, h_ref, o_ref):
    r0, r1 = a_ref.shape[1], a_ref.shape[1] + b_ref.shape[1]
    acc = jnp.dot(a_ref[...], w_ref[:r0, :], preferred_element_type=F32)
    acc += jnp.dot(b_ref[...], w_ref[r0:r1, :], preferred_element_type=F32)
    acc += jnp.dot(c_ref[...], w_ref[r1:, :], preferred_element_type=F32)
    o_ref[...] = h_ref[...] + acc


def out_proj(o_dn, o_diff, o_ml, w, h, tm=512):
    m, d = h.shape
    tm = _tile(m, tm)
    row = lambda width: pl.BlockSpec((tm, width), lambda i: (i, 0))
    return pl.pallas_call(
        _out_proj_kernel,
        grid=(m // tm,),
        in_specs=[row(o_dn.shape[1]), row(o_diff.shape[1]), row(o_ml.shape[1]),
                  pl.BlockSpec(w.shape, lambda i: (0, 0)), row(d)],
        out_specs=row(d),
        out_shape=jax.ShapeDtypeStruct((m, d), F32),
        compiler_params=_params(("parallel",)),
        name="out_proj",
    )(o_dn, o_diff, o_ml, w, h)


RING = 2


def _ffn_kernel(te_ref, na_ref, x_hbm, nw_ref, wg_hbm, wu_hbm, wd_hbm, o_ref, x_b, xs_ref, wg_b, wu_b, wd_b,
                sems, x_sem, *, fuse_norm):
    i, f = pl.program_id(0), pl.program_id(1)
    nf = pl.num_programs(1)
    tm = x_b.shape[0]
    tf = wg_b.shape[2]
    step = i * nf + f
    total = na_ref[0] * nf

    def copies(s):
        e = te_ref[s // nf]
        cols = pl.ds(pl.multiple_of((s % nf) * tf, tf), tf)
        slot = s % RING
        return (pltpu.make_async_copy(wg_hbm.at[e, :, cols], wg_b.at[slot], sems.at[slot, 0]),
                pltpu.make_async_copy(wu_hbm.at[e, :, cols], wu_b.at[slot], sems.at[slot, 1]),
                pltpu.make_async_copy(wd_hbm.at[e, cols, :], wd_b.at[slot], sems.at[slot, 2]))

    def start(s):
        @pl.when(s < total)
        def _():
            for c in copies(s):
                c.start()

    def x_copy(tile):
        return pltpu.make_async_copy(x_hbm.at[pl.ds(pl.multiple_of(tile * tm, tm), tm), :], x_b, x_sem)

    @pl.when(step == 0)
    def _():
        x_copy(0).start()
        for s in range(RING - 1):
            start(s)

    @pl.when((f == 1) & (i + 1 < pl.num_programs(0)))
    def _():
        x_copy(i + 1).start()

    @pl.when(f == 0)
    def _():
        x_copy(i).wait()
        x = x_b[...]
        if fuse_norm:
            xs_ref[...] = _rms(x, nw_ref[...]).astype(BF16)
            o_ref[...] = x
        else:
            xs_ref[...] = x.astype(BF16)
            o_ref[...] = jnp.zeros_like(o_ref)

    @pl.when(i < na_ref[0])
    def _():
        start(step + RING - 1)
        for c in copies(step):
            c.wait()
        slot = step % RING
        xs = xs_ref[...]
        g = jnp.dot(xs, wg_b[slot].astype(BF16), preferred_element_type=F32)
        u = jnp.dot(xs, wu_b[slot].astype(BF16), preferred_element_type=F32)
        o_ref[...] += jnp.dot((_silu(g) * u).astype(BF16), wd_b[slot].astype(BF16), preferred_element_type=F32)


def ffn(x, nw, tile_expert, n_active, wg, wu, wd, tm, tf, fuse_norm, name):
    rows, d = x.shape
    dff = wg.shape[2]
    tf = _tile(dff, tf)
    any_spec = pl.BlockSpec(memory_space=pl.ANY)
    return pl.pallas_call(
        functools.partial(_ffn_kernel, fuse_norm=fuse_norm),
        grid_spec=pltpu.PrefetchScalarGridSpec(
            num_scalar_prefetch=2,
            grid=(rows // tm, dff // tf),
            in_specs=[any_spec,
                      pl.BlockSpec((1, d), lambda i, f, te, na: (0, 0)),
                      any_spec, any_spec, any_spec],
            out_specs=pl.BlockSpec((tm, d), lambda i, f, te, na: (i, 0), pipeline_mode=pl.Buffered(1)),
            scratch_shapes=[pltpu.VMEM((tm, d), F32), pltpu.VMEM((tm, d), BF16),
                            pltpu.VMEM((RING, d, tf), F32), pltpu.VMEM((RING, d, tf), F32),
                            pltpu.VMEM((RING, tf, d), F32), pltpu.SemaphoreType.DMA((RING, 3)),
                            pltpu.SemaphoreType.DMA(())]),
        out_shape=jax.ShapeDtypeStruct((rows, d), F32),
        compiler_params=_params(("arbitrary", "arbitrary")),
        name=name,
    )(tile_expert, n_active, x, nw.reshape(1, d), wg, wu, wd)


def dense_ffn(h, nw, wg, wu, wd, tm=1024, tf=512):
    tm = _tile(h.shape[0], tm)
    n_tiles = h.shape[0] // tm
    return ffn(h, nw, jnp.zeros((n_tiles,), jnp.int32), jnp.full((1,), n_tiles, jnp.int32),
               wg[None], wu[None], wd[None], tm, tf, True, "dense_ffn")


def _router_kernel(h_ref, nw_ref, wr_ref, xs_ref, idx_ref, wts_ref, cnt_ref, carry_ref):
    i = pl.program_id(0)
    tm = h_ref.shape[0]

    @pl.when(i == 0)
    def _():
        carry_ref[...] = jnp.zeros_like(carry_ref)

    xn = _rms(h_ref[...], nw_ref[...])
    xs_ref[...] = xn
    logits = _dot_hi(xn, wr_ref[...])
    lane = lax.broadcasted_iota(jnp.int32, logits.shape, 1)
    logits = jnp.where(lane < N_EXPERTS, logits, NEG)
    m1 = jnp.max(logits, axis=-1, keepdims=True)
    e1 = jnp.min(jnp.where(logits == m1, lane, LANE), axis=-1, keepdims=True)
    rest = jnp.where(lane == e1, NEG, logits)
    m2 = jnp.max(rest, axis=-1, keepdims=True)
    e2 = jnp.min(jnp.where(rest == m2, lane, LANE), axis=-1, keepdims=True)
    ex = jnp.exp(m2 - m1)
    w1 = 1.0 / (1.0 + ex)
    w2 = ex * w1
    oh1 = (lane == e1).astype(F32)
    oh2 = (lane == e2).astype(F32)
    oh = oh1 + oh2
    r = lax.broadcasted_iota(jnp.int32, (tm, tm), 0)
    c = lax.broadcasted_iota(jnp.int32, (tm, tm), 1)
    before = jnp.dot((r > c).astype(BF16), oh.astype(BF16), preferred_element_type=F32) + carry_ref[...]
    rank1 = jnp.sum(before * oh1, axis=-1, keepdims=True)
    rank2 = jnp.sum(before * oh2, axis=-1, keepdims=True)
    carry_ref[...] += jnp.sum(oh, axis=0, keepdims=True)
    cnt_ref[...] = jnp.broadcast_to(carry_ref[...], cnt_ref.shape)
    idx = jnp.where(lane == 0, e1, jnp.where(lane == 1, e2, 0))
    idx = jnp.where(lane == 2, rank1.astype(jnp.int32), jnp.where(lane == 3, rank2.astype(jnp.int32), idx))
    idx_ref[...] = idx
    wts_ref[...] = jnp.where(lane == 0, w1, jnp.where(lane == 1, w2, 0.0))


def moe_router(h, nw, router, tm=512):
    m, d = h.shape
    tm = _tile(m, tm)
    wr = jnp.zeros((d, LANE), F32).at[:, :N_EXPERTS].set(router)
    return pl.pallas_call(
        _router_kernel,
        grid=(m // tm,),
        in_specs=[pl.BlockSpec((tm, d), lambda i: (i, 0)),
                  pl.BlockSpec((1, d), lambda i: (0, 0)),
                  pl.BlockSpec((d, LANE), lambda i: (0, 0))],
        out_specs=[pl.BlockSpec((tm, d), lambda i: (i, 0)),
                   pl.BlockSpec((tm, LANE), lambda i: (i, 0)),
                   pl.BlockSpec((tm, LANE), lambda i: (i, 0)),
                   pl.BlockSpec((8, LANE), lambda i: (0, 0))],
        out_shape=[jax.ShapeDtypeStruct((m, d), F32),
                   jax.ShapeDtypeStruct((m, LANE), jnp.int32),
                   jax.ShapeDtypeStruct((m, LANE), F32),
                   jax.ShapeDtypeStruct((8, LANE), F32)],
        scratch_shapes=[pltpu.VMEM((1, LANE), F32)],
        compiler_params=_params(("arbitrary",)),
        name="moe_router",
    )(h, nw.reshape(1, d), wr)


ROW_UNROLL = 8


def _dispatch_kernel(dest_ref, xs_ref, init_ref, out_ref, sems):
    del init_ref
    tm = xs_ref.shape[0]

    def start(t, c):
        for k in range(2):
            pltpu.make_async_copy(xs_ref.at[pl.ds(t, 1), :], out_ref.at[pl.ds(dest_ref[2 * t + k], 1), :],
                                  sems.at[k]).start(priority=k)
        return c

    lax.fori_loop(0, tm, start, 0, unroll=ROW_UNROLL)
    for k in range(2):
        pltpu.make_async_copy(xs_ref, out_ref.at[pl.ds(0, tm), :], sems.at[k]).wait()


def moe_dispatch(xs, dest, n_sorted, tm=256):
    m, d = xs.shape
    tm = _tile(m, tm)
    init = jnp.zeros((n_sorted, d), xs.dtype)
    return pl.pallas_call(
        _dispatch_kernel,
        grid=(m // tm,),
        in_specs=[pl.BlockSpec((2 * tm,), lambda i: (i,), memory_space=pltpu.SMEM),
                  pl.BlockSpec((tm, d), lambda i: (i, 0)),
                  pl.BlockSpec(memory_space=pl.ANY)],
        out_specs=pl.BlockSpec(memory_space=pl.ANY),
        out_shape=jax.ShapeDtypeStruct((n_sorted, d), xs.dtype),
        scratch_shapes=[pltpu.SemaphoreType.DMA((2,))],
        input_output_aliases={2: 0},
        compiler_params=_params(("arbitrary",)),
        name="moe_dispatch",
    )(dest, xs, init)


def _combine_kernel(dest_ref, ys_ref, h_ref, wts_ref, o_ref, y_s, sems):
    i = pl.program_id(0)
    tm = h_ref.shape[0]

    def gather(tile, slot):
        def start(t, c):
            for k in range(2):
                src = dest_ref[2 * (tile * tm + t) + k]
                pltpu.make_async_copy(ys_ref.at[pl.ds(src, 1), :], y_s.at[slot, k, pl.ds(t, 1), :],
                                      sems.at[slot, k]).start(priority=k)
            return c

        lax.fori_loop(0, tm, start, 0, unroll=ROW_UNROLL)

    @pl.when(i == 0)
    def _():
        gather(0, 0)

    @pl.when(i + 1 < pl.num_programs(0))
    def _():
        gather(i + 1, (i + 1) % 2)

    slot = i % 2
    for k in range(2):
        pltpu.make_async_copy(ys_ref.at[pl.ds(0, tm), :], y_s.at[slot, k], sems.at[slot, k]).wait()
    wts = wts_ref[...]
    o_ref[...] = h_ref[...] + wts[:, 0:1] * y_s[slot, 0] + wts[:, 1:2] * y_s[slot, 1]


def moe_combine(ysorted, dest, h, wts, tm=256):
    m, d = h.shape
    tm = _tile(m, tm)
    return pl.pallas_call(
        _combine_kernel,
        grid_spec=pltpu.PrefetchScalarGridSpec(
            num_scalar_prefetch=1,
            grid=(m // tm,),
            in_specs=[pl.BlockSpec(memory_space=pl.ANY),
                      pl.BlockSpec((tm, d), lambda i, dest: (i, 0)),
                      pl.BlockSpec((tm, LANE), lambda i, dest: (i, 0))],
            out_specs=pl.BlockSpec((tm, d), lambda i, dest: (i, 0)),
            scratch_shapes=[pltpu.VMEM((2, 2, tm, d), F32), pltpu.SemaphoreType.DMA((2, 2))]),
        out_shape=jax.ShapeDtypeStruct((m, d), F32),
        compiler_params=_params(("arbitrary",)),
        name="moe_combine",
    )(dest, ysorted, h, wts)


def moe_ffn(h, nw, router, wg, wu, wd, tmg=1024):
    m, d = h.shape
    tmg = min(tmg, m)
    xs, idx, wts, cnt = moe_router(h, nw, router)
    counts = cnt[0, :N_EXPERTS].astype(jnp.int32)
    padded = (counts + tmg - 1) // tmg * tmg
    ends = jnp.cumsum(padded)
    offsets = ends - padded
    n_tiles = (2 * m) // tmg + N_EXPERTS
    n_sorted = n_tiles * tmg
    dest = (offsets[idx[:, 0:2]] + idx[:, 2:4]).reshape(-1)
    tile_start = jnp.arange(n_tiles, dtype=jnp.int32) * tmg
    tile_expert = jnp.minimum(jnp.sum(tile_start[:, None] >= ends[None, :], axis=1), N_EXPERTS - 1)
    n_active = (ends[-1] // tmg).reshape(1).astype(jnp.int32)
    xsorted = moe_dispatch(xs, dest, n_sorted)
    ysorted = ffn(xsorted, nw, tile_expert.astype(jnp.int32), n_active, wg, wu, wd, tmg, 512, False, "grouped_ffn")
    return moe_combine(ysorted, dest, h, wts)


def _ple_kernel(h_ref, p_ref, nw_ref, wg_ref, wp_ref, fw_ref, o_ref, *, final):
    h = h_ref[...]
    gate = _sigmoid(jnp.dot(_rms(h, nw_ref[...]).astype(BF16), wg_ref[...], preferred_element_type=F32))
    out = h + jnp.dot(p_ref[...].astype(BF16), wp_ref[...], preferred_element_type=F32) * gate
    o_ref[...] = _rms(out, fw_ref[...]) if final else out


def ple(h, p, nw, wg, wp, fw, final, tm=512):
    m, d = h.shape
    pd = p.shape[1]
    tm = _tile(m, tm)
    return pl.pallas_call(
        functools.partial(_ple_kernel, final=final),
        grid=(m // tm,),
        in_specs=[pl.BlockSpec((tm, d), lambda i: (i, 0)),
                  pl.BlockSpec((tm, pd), lambda i: (i, 0)),
                  pl.BlockSpec((1, d), lambda i: (0, 0)),
                  pl.BlockSpec((d, d), lambda i: (0, 0)),
                  pl.BlockSpec((pd, d), lambda i: (0, 0)),
                  pl.BlockSpec((1, d), lambda i: (0, 0))],
        out_specs=pl.BlockSpec((tm, d), lambda i: (i, 0)),
        out_shape=jax.ShapeDtypeStruct((m, d), F32),
        compiler_params=_params(("parallel",)),
        name="ple",
    )(h, p, nw.reshape(1, d), wg, wp, fw.reshape(1, d))


def _split_w_in(w):
    d = w.shape[0]
    dn_w, diff_w, ml_w = DN_HEADS * HEAD_W, DIFF_HEADS * HEAD_W, ML_HEADS * HEAD_W
    g0 = 3 * dn_w + 2 * ml_w + dn_w
    g1 = g0 + 2 * DN_HEADS
    g2 = g1 + 3 * diff_w + 2 * ml_w
    g3 = g2 + 2 * ML_HEADS
    assert w.shape[1] == g3 and g0 + g2 - g1 == N_BIG
    big = jnp.concatenate([w[:, :g0], w[:, g1:g2]], axis=1).astype(BF16)
    small = jnp.concatenate([w[:, g0:g1], w[:, g2:g3],
                             jnp.zeros((d, LANE - (g1 - g0) - (g3 - g2)), w.dtype)], axis=1).astype(BF16)
    return big, small


def kernel(x, p, attn_norm, w_in, conv_dn, conv_ml, dn_a_log, dn_dt_bias, dn_norm, diff_lq1, diff_lk1, diff_lq2, diff_lk2, diff_norm, ml_i_bias, ml_f_bias, ml_norm, w_out, ffn_norm, dense_w_gate, dense_w_up, dense_w_down, router, moe_w_gate, moe_w_up, moe_w_down, ple_norm, ple_proj, ple_gate, final_norm):
    batch, seq, d = x.shape
    depth = w_in.shape[0]
    m = batch * seq
    h = x.reshape(m, d)
    for i in range(depth):
        lambda_init = 0.8 - 0.6 * math.exp(-0.3 * i)
        w_big, w_small = _split_w_in(w_in[i])
        zb, zs = norm_proj(h, attn_norm[i], w_big, w_small)
        o_dn = deltanet(zb, zs, conv_dn[i], dn_a_log[i], dn_dt_bias[i], dn_norm[i], batch, seq)
        o_diff = diff_attention(zb, diff_lq1[i], diff_lk1[i], diff_lq2[i], diff_lk2[i], diff_norm[i],
                                lambda_init, batch, seq)
        o_ml = mlstm(zb, zs, conv_ml[i], ml_i_bias[i], ml_f_bias[i], ml_norm[i], batch, seq)
        h = out_proj(o_dn, o_diff, o_ml, w_out[i].astype(BF16), h)
        j = i // 2
        if i % 2 == 0:
            h = dense_ffn(h, ffn_norm[i], dense_w_gate[j], dense_w_up[j], dense_w_down[j])
        else:
            h = moe_ffn(h, ffn_norm[i], router[j], moe_w_gate[j], moe_w_up[j], moe_w_down[j])
        h = ple(h, p[i].reshape(m, -1), ple_norm[i], ple_gate[i].astype(BF16), ple_proj[i].astype(BF16),
                final_norm, final=(i == depth - 1))
    return h.reshape(batch, seq, d)
```

```python
import functools
import math

import numpy as np
import jax
import jax.numpy as jnp
from jax import lax
from jax.experimental import pallas as pl
from jax.experimental.pallas import tpu as pltpu

F32 = jnp.float32
BF16 = jnp.bfloat16
EPS = 1e-6
LANE = 128
NEG = -1e30

DN_HEADS = 4
DIFF_HEADS = 8
ML_HEADS = 4
HEAD_W = 128
DIFF_DH = 64
CHUNK = 64
CONV_K = 4
N_EXPERTS = 8
VMEM_LIMIT = 56 * 1024 * 1024

CB_DN_Q, CB_DN_K, CB_DN_V = 0, 4, 8
CB_ML_Q, CB_ML_K = 12, 16
CB_DN_Z = 20
CB_AQ, CB_AK, CB_AV = 24, 32, 40
CB_ML_V, CB_ML_O = 48, 52
N_BIG = 56 * LANE
SL_DN_B, SL_DN_A, SL_ML_I, SL_ML_F = 0, 4, 8, 12


def _params(sem):
    return pltpu.CompilerParams(dimension_semantics=sem, vmem_limit_bytes=VMEM_LIMIT)


def _dot(a, b):
    return jnp.dot(a.astype(BF16), b.astype(BF16), preferred_element_type=F32)


def _dot_nt(a, b):
    return lax.dot_general(a.astype(BF16), b.astype(BF16), (((1,), (1,)), ((), ())),
                           preferred_element_type=F32)


def _dot_tn(a, b):
    return lax.dot_general(a.astype(BF16), b.astype(BF16), (((0,), (0,)), ((), ())),
                           preferred_element_type=F32)


def _dot_hi(a, b):
    return jnp.dot(a, b, preferred_element_type=F32, precision=lax.Precision.HIGHEST)


def _rms(x, w):
    return x * lax.rsqrt(jnp.mean(x * x, axis=-1, keepdims=True) + EPS) * w


def _sigmoid(x):
    return 1.0 / (1.0 + jnp.exp(-x))


def _silu(x):
    return x * _sigmoid(x)


def _softplus(x):
    return jnp.maximum(x, 0.0) + jnp.log(1.0 + jnp.exp(-jnp.abs(x)))


def _log_sigmoid(x):
    return -_softplus(-x)


def _tile(m, t):
    t = min(m, t)
    assert m % t == 0
    return t


def _norm_proj_kernel(x_ref, nw_ref, w_ref, ws_ref, o_ref, os_ref, xs_ref):
    @pl.when(pl.program_id(1) == 0)
    def _():
        xs_ref[...] = _rms(x_ref[...], nw_ref[...]).astype(BF16)
        os_ref[...] = jnp.dot(xs_ref[...], ws_ref[...], preferred_element_type=F32)

    o_ref[...] = jnp.dot(xs_ref[...], w_ref[...], preferred_element_type=F32).astype(o_ref.dtype)


def norm_proj(x, nw, w_big, w_small, tm=1024, tn=1792):
    m, d = x.shape
    n = w_big.shape[1]
    tm, tn = _tile(m, tm), _tile(n, tn)
    return pl.pallas_call(
        _norm_proj_kernel,
        grid=(m // tm, n // tn),
        in_specs=[pl.BlockSpec((tm, d), lambda i, j: (i, 0)),
                  pl.BlockSpec((1, d), lambda i, j: (0, 0)),
                  pl.BlockSpec((d, tn), lambda i, j: (0, j)),
                  pl.BlockSpec((d, LANE), lambda i, j: (0, 0))],
        out_specs=[pl.BlockSpec((tm, tn), lambda i, j: (i, j)),
                   pl.BlockSpec((tm, LANE), lambda i, j: (i, 0))],
        out_shape=[jax.ShapeDtypeStruct((m, n), BF16), jax.ShapeDtypeStruct((m, LANE), F32)],
        scratch_shapes=[pltpu.VMEM((tm, d), BF16)],
        compiler_params=_params(("parallel", "arbitrary")),
        name="norm_proj",
    )(x, nw.reshape(1, d), w_big, w_small)


def _chunk_conv_silu(ref, w, ci, rows):
    cur = ref[rows, :].astype(F32)
    prev_rows = pl.ds(pl.multiple_of(jnp.maximum(ci * CHUNK - 16, 0), 16), 16)
    prev = jnp.where(ci > 0, ref[prev_rows, :].astype(F32)[8:16, :], 0.0)
    ext = jnp.concatenate([prev, cur], axis=0)
    y = cur * w[CONV_K - 1:CONV_K, :]
    for back in range(1, CONV_K):
        y = y + pltpu.roll(ext, back, axis=0)[8:8 + CHUNK, :] * w[CONV_K - 1 - back:CONV_K - back, :]
    return _silu(y)


def _dot2(l_bf16, x):
    hi = x.astype(BF16)
    lo = (x - hi.astype(F32)).astype(BF16)
    return (jnp.dot(l_bf16, hi, preferred_element_type=F32)
            + jnp.dot(l_bf16, lo, preferred_element_type=F32))


def _lane_rows(*rows):
    out = jnp.zeros((8, LANE), F32)
    for r, (lane0, vals) in enumerate(rows):
        out = out.at[r, lane0:lane0 + vals.shape[0]].set(vals.astype(F32))
    return out


def _chunk_masks():
    r = lax.broadcasted_iota(jnp.int32, (CHUNK, CHUNK), 0)
    c = lax.broadcasted_iota(jnp.int32, (CHUNK, CHUNK), 1)
    return r, c


def _chunk_rows(ci):
    return pl.ds(pl.multiple_of(ci * CHUNK, CHUNK), CHUNK)


assert DN_HEADS == ML_HEADS
HEADS = range(DN_HEADS)
COLS = [slice(h * HEAD_W, (h + 1) * HEAD_W) for h in HEADS]
UNROLL = 4


def _deltanet_kernel(gp_ref, q_ref, k_ref, v_ref, zg_ref, zs_ref, cw_ref, nw_ref, o_ref,
                     u_s, w_s, at_s, qg_s, kg_s, gl_s):
    n_chunks = q_ref.shape[0] // CHUNK
    width = DN_HEADS * HEAD_W
    r, c = _chunk_masks()
    low_incl = (r >= c).astype(BF16)
    strict_up = (r > c).astype(F32)
    tri_incl = r >= c
    tri_strict = r > c
    cw = cw_ref[...]
    rate_row = jnp.exp(gp_ref[0:1, :])
    bias_row = gp_ref[1:2, :]

    def prep_stages(cp):
        units, qc, kc, kb, kcb, vbeta, g = [], [], [], [], [], [], []
        for t in range(UNROLL):
            ci = UNROLL * cp + t
            rows = _chunk_rows(ci)
            qa = _chunk_conv_silu(q_ref, cw[:, :width], ci, rows)
            ka = _chunk_conv_silu(k_ref, cw[:, width:2 * width], ci, rows)
            va = _chunk_conv_silu(v_ref, cw[:, 2 * width:], ci, rows)
            zs = zs_ref[rows, :]
            sig = _sigmoid(zs)
            g_all = -rate_row * _softplus(zs + bias_row)
            for h in HEADS:
                units.append((ci, rows, h))
                qh, kh = qa[:, COLS[h]], ka[:, COLS[h]]
                qc.append(qh * (lax.rsqrt(jnp.sum(qh * qh, axis=-1, keepdims=True) + EPS) * (HEAD_W ** -0.5)))
                kc.append(kh * lax.rsqrt(jnp.sum(kh * kh, axis=-1, keepdims=True) + EPS))
                beta = sig[:, SL_DN_B + h:SL_DN_B + h + 1]
                g.append(g_all[:, SL_DN_A + h:SL_DN_A + h + 1])
                kb.append(kc[-1] * beta)
                kcb.append(kc[-1].astype(BF16))
                vbeta.append(va[:, COLS[h]] * beta)
        us = range(len(units))
        yield
        dmat = [_dot2(low_incl, jnp.broadcast_to(g[u], (CHUNK, CHUNK)) * strict_up) for u in us]
        kk = [_dot_nt(kb[u], kcb[u]) for u in us]
        qk = [_dot_nt(qc[u], kcb[u]) for u in us]
        gc = [dmat[u][:, :1] + g[u][:1, :] for u in us]
        decay = [jnp.where(tri_incl, jnp.exp(dmat[u]), 0.0) for u in us]
        eg = [jnp.exp(gc[u]) for u in us]
        a = [jnp.where(tri_strict, kk[u] * decay[u], 0.0).astype(BF16) for u in us]
        x = [jnp.concatenate([vbeta[u], kb[u] * eg[u]], axis=1) for u in us]
        yield
        ax = [_dot(a[u], x[u]) for u in us]
        p = [_dot(a[u], a[u]).astype(BF16) for u in us]
        x = [x[u] - ax[u] for u in us]
        for step in range(5):
            px = [_dot(p[u], x[u]) for u in us]
            if step < 4:
                p = [_dot(p[u], p[u]).astype(BF16) for u in us]
            x = [x[u] + px[u] for u in us]
            if step in (0, 2):
                yield
        for u, (ci, rows, h) in enumerate(units):
            at_s[rows, h * HEAD_W:h * HEAD_W + CHUNK] = (qk[u] * decay[u]).astype(BF16)
            u_s[rows, COLS[h]] = x[u][:, :HEAD_W]
            w_s[rows, COLS[h]] = x[u][:, HEAD_W:].astype(BF16)
            g_last = gc[u][CHUNK - 1:CHUNK, :]
            qg_s[rows, COLS[h]] = (qc[u] * eg[u]).astype(BF16)
            kg_s[rows, COLS[h]] = (kc[u] * jnp.exp(g_last - gc[u])).astype(BF16)
            gl_s[pl.ds(pl.multiple_of(ci * 8, 8), 8), COLS[h]] = jnp.broadcast_to(jnp.exp(g_last), (8, HEAD_W))

    assert n_chunks % UNROLL == 0 and UNROLL == 4
    n_groups = n_chunks // UNROLL
    nw = nw_ref[...]

    def scan(ci, states):
        rows = _chunk_rows(ci)
        sb = [states[h].astype(BF16) for h in HEADS]
        ws = [jnp.dot(w_s[rows, COLS[h]], sb[h], preferred_element_type=F32) for h in HEADS]
        qs = [jnp.dot(qg_s[rows, COLS[h]], sb[h], preferred_element_type=F32) for h in HEADS]
        vb = [(u_s[rows, COLS[h]] - ws[h]).astype(BF16) for h in HEADS]
        av = [jnp.dot(at_s[rows, h * HEAD_W:h * HEAD_W + CHUNK], vb[h], preferred_element_type=F32)
              for h in HEADS]
        kv = [_dot_tn(kg_s[rows, COLS[h]], vb[h]) for h in HEADS]
        out = []
        for h in HEADS:
            gl = gl_s[pl.ds(pl.multiple_of(ci * 8, 8), 1), COLS[h]]
            out.append(states[h] * gl + kv[h])
            o_ref[rows, COLS[h]] = (_rms(qs[h] + av[h], nw)
                                    * _silu(zg_ref[rows, COLS[h]].astype(F32))).astype(o_ref.dtype)
        return tuple(out)

    for _ in prep_stages(0):
        pass

    def body(cp, states):
        stages = prep_stages(cp + 1)
        for t in range(UNROLL):
            next(stages)
            states = scan(UNROLL * cp + t, states)
        for _ in stages:
            pass
        return states

    states = lax.fori_loop(0, n_groups - 1, body, tuple(jnp.zeros((HEAD_W, HEAD_W), F32) for _ in range(DN_HEADS)))
    for t in range(UNROLL):
        states = scan(UNROLL * (n_groups - 1) + t, states)


def deltanet(zb, zs, conv_dn, a_log, dt_bias, dn_norm, batch, seq):
    m = zb.shape[0]
    width = DN_HEADS * HEAD_W
    blk = lambda cb: pl.BlockSpec((seq, width), lambda b: (b, cb // DN_HEADS))
    return pl.pallas_call(
        _deltanet_kernel,
        grid=(batch,),
        in_specs=[pl.BlockSpec((8, LANE), lambda b: (0, 0)),
                  blk(CB_DN_Q), blk(CB_DN_K), blk(CB_DN_V), blk(CB_DN_Z),
                  pl.BlockSpec((seq, LANE), lambda b: (b, 0)),
                  pl.BlockSpec((CONV_K, 3 * width), lambda b: (0, 0)),
                  pl.BlockSpec((1, HEAD_W), lambda b: (0, 0))],
        out_specs=pl.BlockSpec((seq, width), lambda b: (b, 0)),
        out_shape=jax.ShapeDtypeStruct((m, width), BF16),
        scratch_shapes=[pltpu.VMEM((seq, width), F32), pltpu.VMEM((seq, width), BF16),
                        pltpu.VMEM((seq, width), BF16), pltpu.VMEM((seq, width), BF16),
                        pltpu.VMEM((seq, width), BF16),
                        pltpu.VMEM((seq // CHUNK * 8, width), F32)],
        compiler_params=_params(("parallel",)),
        name="deltanet",
    )(_lane_rows((SL_DN_A, a_log), (SL_DN_A, dt_bias)), zb, zb, zb, zb, zs, conv_dn,
      dn_norm.reshape(1, HEAD_W))


def _mlstm_kernel(gp_ref, q_ref, k_ref, v_ref, og_ref, zs_ref, cw_ref, nw_ref, o_ref):
    n_chunks = q_ref.shape[0] // CHUNK
    width = ML_HEADS * HEAD_W
    r, c = _chunk_masks()
    low_incl = (r >= c).astype(BF16)
    strict_up = (r > c).astype(F32)
    eye = (r == c).astype(F32)
    tri_incl = r >= c
    nw = nw_ref[...]
    cw = cw_ref[...]

    def local(ci):
        rows = _chunk_rows(ci)
        qa = _chunk_conv_silu(q_ref, cw[:, :width], ci, rows)
        ka = _chunk_conv_silu(k_ref, cw[:, width:], ci, rows) * (HEAD_W ** -0.5)
        zs = zs_ref[rows, :]
        ip_all = zs + gp_ref[0:1, :]
        lf_all = _log_sigmoid(zs + gp_ref[1:2, :])
        qc = [qa[:, COLS[h]] for h in HEADS]
        kc = [ka[:, COLS[h]] for h in HEADS]
        ip = [jnp.broadcast_to(ip_all[:, SL_ML_I + h:SL_ML_I + h + 1], (CHUNK, HEAD_W)) for h in HEADS]
        lf = [jnp.broadcast_to(lf_all[:, SL_ML_F + h:SL_ML_F + h + 1], (CHUNK, HEAD_W)) for h in HEADS]
        dl = [_dot2(low_incl, lf[h][:, :CHUNK] * strict_up + ip[h][:, :CHUNK] * eye) for h in HEADS]
        bc = [_dot2(low_incl, lf[h]) for h in HEADS]
        qk = [_dot_nt(qc[h], kc[h]) for h in HEADS]
        heads = []
        for h in HEADS:
            b = bc[h]
            dlog = jnp.where(tri_incl, dl[h], NEG)
            b_last = b[CHUNK - 1:CHUNK, :]
            a = b_last - b + ip[h]
            heads.append(dict(qc=qc[h], kc=kc[h], qk=qk[h], b=b, dlog=dlog, b_last=b_last, a=a,
                              dmax=jnp.max(dlog, axis=-1, keepdims=True),
                              a_max=jnp.max(a, axis=0, keepdims=True)))
        return rows, heads

    def advance(rows, heads, carry):
        vc = [v_ref[rows, COLS[h]] for h in HEADS]
        s, wk, inter, m_t, m_new, scale = [], [], [], [], [], []
        for h in HEADS:
            t = heads[h]
            m_st = carry[h][2]
            m_t.append(jnp.maximum(t["dmax"], t["b"] + m_st))
            s.append(t["qk"] * jnp.exp(t["dlog"] - m_t[h][:, :CHUNK]))
            inter.append(jnp.exp(t["b"] + m_st - m_t[h]))
            m_new.append(jnp.maximum(t["b_last"] + m_st, t["a_max"]))
            scale.append(jnp.exp(t["b_last"] + m_st - m_new[h]))
            wk.append(t["kc"] * jnp.exp(t["a"] - m_new[h]))
        sv = [_dot(s[h], vc[h]) for h in HEADS]
        kv = [_dot_tn(wk[h], vc[h]) for h in HEADS]
        qcs = [_dot(heads[h]["qc"], carry[h][0]) for h in HEADS]
        out = []
        for h in HEADS:
            c_st, n_st, _ = carry[h]
            qc = heads[h]["qc"]
            num = sv[h] + inter[h] * qcs[h]
            den = (jnp.sum(s[h], axis=-1, keepdims=True)
                   + inter[h] * jnp.sum(qc * n_st, axis=-1, keepdims=True))
            hh = num / jnp.maximum(jnp.abs(den), jnp.exp(-m_t[h]))
            out.append((scale[h] * c_st + kv[h],
                        scale[h] * n_st + jnp.sum(wk[h], axis=0, keepdims=True),
                        m_new[h]))
            o_ref[rows, COLS[h]] = (_sigmoid(og_ref[rows, COLS[h]].astype(F32))
                                    * _rms(hh, nw)).astype(o_ref.dtype)
        return tuple(out)

    init = tuple((jnp.zeros((HEAD_W, HEAD_W), F32), jnp.zeros((1, HEAD_W), F32), jnp.zeros((1, HEAD_W), F32))
                 for _ in range(ML_HEADS))
    lax.fori_loop(0, n_chunks, lambda ci, carry: advance(*local(ci), carry), init)


def mlstm(zb, zs, conv_ml, i_bias, f_bias, ml_norm, batch, seq):
    m = zb.shape[0]
    width = ML_HEADS * HEAD_W
    blk = lambda cb: pl.BlockSpec((seq, width), lambda b: (b, cb // ML_HEADS))
    return pl.pallas_call(
        _mlstm_kernel,
        grid=(batch,),
        in_specs=[pl.BlockSpec((8, LANE), lambda b: (0, 0)),
                  blk(CB_ML_Q), blk(CB_ML_K), blk(CB_ML_V), blk(CB_ML_O),
                  pl.BlockSpec((seq, LANE), lambda b: (b, 0)),
                  pl.BlockSpec((CONV_K, 2 * width), lambda b: (0, 0)),
                  pl.BlockSpec((1, HEAD_W), lambda b: (0, 0))],
        out_specs=pl.BlockSpec((seq, width), lambda b: (b, 0)),
        out_shape=jax.ShapeDtypeStruct((m, width), BF16),
        compiler_params=_params(("parallel",)),
        name="mlstm",
    )(_lane_rows((SL_ML_I, i_bias), (SL_ML_F, f_bias)), zb, zb, zb, zb, zs, conv_ml,
      ml_norm.reshape(1, HEAD_W))


LOG2E = math.log2(math.e)
LOG2E_HI = float(np.asarray(LOG2E, dtype=BF16))
LOG2E_LO = LOG2E - LOG2E_HI
SLAB = 128
RAMP = 256


def _with_lanes(x, base, vals):
    lane = lax.broadcasted_iota(jnp.int32, x.shape, 1)
    for i, v in enumerate(vals):
        x = jnp.where(lane == base + i, v, x)
    return x


def _diff_attn_kernel(lq1_ref, lk1_ref, lq2_ref, lk2_ref, q_ref, k_ref, v_ref, nw_ref, o_ref,
                      k0_s, k1_s, *, tq, lambda_init):
    h = pl.program_id(1)
    qi = pl.program_id(2)
    tk = RAMP
    per_pair = tq // (2 * tk)
    assert k_ref.shape[0] <= RAMP * RAMP and tq % (2 * tk) == 0
    slope = jnp.exp2(jnp.full((1, 1), -(h + 1).astype(F32) * (8.0 / DIFF_HEADS), F32))

    @pl.when(qi == 0)
    def _():
        kk = k_ref[...]
        pos = lax.broadcasted_iota(jnp.int32, kk.shape, 0)
        fine = ((pos & (RAMP - 1)).astype(F32) * slope).astype(BF16)
        coarse = ((pos - (pos & (RAMP - 1))).astype(F32) * slope).astype(BF16)
        k0_s[...] = _with_lanes(kk, DIFF_DH, (fine, fine, coarse, coarse))
        k1_s[...] = _with_lanes(kk, 0, (fine, fine, coarse, coarse))

    lam = (jnp.exp(jnp.sum(lq1_ref[...] * lk1_ref[...], axis=-1, keepdims=True))
           - jnp.exp(jnp.sum(lq2_ref[...] * lk2_ref[...], axis=-1, keepdims=True)) + lambda_init)
    lane = lax.broadcasted_iota(jnp.int32, (tq, HEAD_W), 1)
    q = q_ref[...].astype(F32) * (DIFF_DH ** -0.5 * LOG2E)
    l2e = (LOG2E_HI, LOG2E_LO, LOG2E_HI, LOG2E_LO)
    q0 = _with_lanes(jnp.where(lane < DIFF_DH, q, 0.0), DIFF_DH, l2e).astype(BF16)
    q1 = _with_lanes(jnp.where(lane >= DIFF_DH, q, 0.0), 0, l2e).astype(BF16)
    per_map = tq // SLAB
    n_slab = 2 * per_map
    q_slab = [qm[j * SLAB:(j + 1) * SLAB, :] for qm in (q0, q1) for j in range(per_map)]

    def run_pair(kp, diag_at, carry):
        def needed(i, blk):
            return diag_at is None or diag_at + blk * tk <= (i % per_map) * SLAB + SLAB - 1

        scores, values = [], []
        for blk in range(2):
            rows = pl.ds(pl.multiple_of((2 * kp + blk) * tk, tk), tk)
            kb = (k0_s[rows, :], k1_s[rows, :])
            values.append(v_ref[rows, :])
            scores.append([_dot_nt(q_slab[i], kb[i // per_map]) if needed(i, blk) else None
                           for i in range(n_slab)])
        for blk in range(2):
            out = []
            for i in range(n_slab):
                if not needed(i, blk):
                    out.append(carry[i])
                    continue
                m_i, l_i, acc = carry[i]
                s = scores[blk][i]
                if diag_at is not None and diag_at + (blk + 1) * tk - 1 > (i % per_map) * SLAB:
                    ri = lax.broadcasted_iota(jnp.int32, (SLAB, tk), 0) + (i % per_map) * SLAB
                    ci = lax.broadcasted_iota(jnp.int32, (SLAB, tk), 1) + (diag_at + blk * tk)
                    s = jnp.where(ci <= ri, s, NEG)
                m_new = jnp.maximum(m_i, jnp.max(s, axis=-1, keepdims=True))
                p = jnp.exp2(s - m_new)
                alpha = jnp.exp2(m_i - m_new)
                l_new = alpha * l_i + sum(p[:, j * LANE:(j + 1) * LANE] for j in range(tk // LANE))
                acc_new = alpha * acc + jnp.dot(p.astype(BF16), values[blk], preferred_element_type=F32)
                out.append((m_new, l_new, acc_new))
            carry = tuple(out)
        return carry

    init = tuple((jnp.full((SLAB, 1), NEG, F32), jnp.zeros((SLAB, LANE), F32), jnp.zeros((SLAB, HEAD_W), F32))
                 for _ in range(n_slab))
    def run_wide(kp, carry):
        rows = pl.ds(pl.multiple_of(2 * kp * tk, 2 * tk), 2 * tk)
        kb = (k0_s[rows, :], k1_s[rows, :])
        vb = v_ref[rows, :]
        scores = [_dot_nt(q_slab[i], kb[i // per_map]) for i in range(n_slab)]
        out = []
        for i in range(n_slab):
            m_i, l_i, acc = carry[i]
            s = scores[i]
            m_new = jnp.maximum(m_i, jnp.max(s, axis=-1, keepdims=True))
            p = jnp.exp2(s - m_new)
            alpha = jnp.exp2(m_i - m_new)
            l_new = alpha * l_i + sum(p[:, j * LANE:(j + 1) * LANE] for j in range(2 * tk // LANE))
            acc_new = alpha * acc + jnp.dot(p.astype(BF16), vb, preferred_element_type=F32)
            out.append((m_new, l_new, acc_new))
        return tuple(out)

    carry = lax.fori_loop(0, qi * per_pair, run_wide, init)
    for t in range(per_pair):
        carry = run_pair(qi * per_pair + t, t * 2 * tk, carry)
    nw = nw_ref[...]
    for j in range(per_map):
        (_, l0, a0), (_, l1, a1) = carry[j], carry[per_map + j]
        l0 = jnp.sum(l0, axis=-1, keepdims=True)
        l1 = jnp.sum(l1, axis=-1, keepdims=True)
        o = a0 / l0 - lam * (a1 / l1)
        o_ref[j * SLAB:(j + 1) * SLAB, :] = (_rms(o, nw) * (1.0 - lambda_init)).astype(o_ref.dtype)


def diff_attention(zb, lq1, lk1, lq2, lk2, diff_norm, lambda_init, batch, seq, tq=512):
    m = zb.shape[0]
    tq = _tile(seq, tq)
    nq = seq // tq
    vec = pl.BlockSpec((1, DIFF_DH), lambda b, h, i: (0, 0))
    return pl.pallas_call(
        functools.partial(_diff_attn_kernel, tq=tq, lambda_init=lambda_init),
        grid=(batch, DIFF_HEADS, nq),
        in_specs=[vec, vec, vec, vec,
                  pl.BlockSpec((tq, HEAD_W), lambda b, h, i: (b * nq + i, CB_AQ + h)),
                  pl.BlockSpec((seq, HEAD_W), lambda b, h, i: (b, CB_AK + h)),
                  pl.BlockSpec((seq, HEAD_W), lambda b, h, i: (b, CB_AV + h)),
                  pl.BlockSpec((1, HEAD_W), lambda b, h, i: (0, 0))],
        out_specs=pl.BlockSpec((tq, HEAD_W), lambda b, h, i: (b * nq + i, h)),
        out_shape=jax.ShapeDtypeStruct((m, DIFF_HEADS * HEAD_W), BF16),
        scratch_shapes=[pltpu.VMEM((seq, HEAD_W), BF16), pltpu.VMEM((seq, HEAD_W), BF16)],
        compiler_params=_params(("parallel", "parallel", "arbitrary")),
        name="diff_attn",
    )(lq1.reshape(1, -1), lk1.reshape(1, -1), lq2.reshape(1, -1), lk2.reshape(1, -1),
      zb, zb, zb, diff_norm.reshape(1, HEAD_W))


def _out_proj_kernel(a_ref, b_ref, c_ref, w_ref, h_ref, o_ref):
    r0, r1 = a_ref.shape[1], a_ref.shape[1] + b_ref.shape[1]
    acc = jnp.dot(a_ref[...], w_ref[:r0, :], preferred_element_type=F32)
    acc += jnp.dot(b_ref[...], w_ref[r0:r1, :], preferred_element_type=F32)
    acc += jnp.dot(c_ref[...], w_ref[r1:, :], preferred_element_type=F32)
    o_ref[...] = h_ref[...] + acc


def out_proj(o_dn, o_diff, o_ml, w, h, tm=512):
    m, d = h.shape
    tm = _tile(m, tm)
    row = lambda width: pl.BlockSpec((tm, width), lambda i: (i, 0))
    return pl.pallas_call(
        _out_proj_kernel,
        grid=(m // tm,),
        in_specs=[row(o_dn.shape[1]), row(o_diff.shape[1]), row(o_ml.shape[1]),
                  pl.BlockSpec(w.shape, lambda i: (0, 0)), row(d)],
        out_specs=row(d),
        out_shape=jax.ShapeDtypeStruct((m, d), F32),
        compiler_params=_params(("parallel",)),
        name="out_proj",
    )(o_dn, o_diff, o_ml, w, h)


RING = 2


def _ffn_kernel(te_ref, na_ref, x_hbm, nw_ref, wg_hbm, wu_hbm, wd_hbm, o_ref, x_b, xs_ref, wg_b, wu_b, wd_b,
                sems, x_sem, *, fuse_norm):
    i, f = pl.program_id(0), pl.program_id(1)
    nf = pl.num_programs(1)
    tm = x_b.shape[0]
    tf = wg_b.shape[2]
    step = i * nf + f
    total = na_ref[0] * nf

    def copies(s):
        e = te_ref[s // nf]
        cols = pl.ds(pl.multiple_of((s % nf) * tf, tf), tf)
        slot = s % RING
        return (pltpu.make_async_copy(wg_hbm.at[e, :, cols], wg_b.at[slot], sems.at[slot, 0]),
                pltpu.make_async_copy(wu_hbm.at[e, :, cols], wu_b.at[slot], sems.at[slot, 1]),
                pltpu.make_async_copy(wd_hbm.at[e, cols, :], wd_b.at[slot], sems.at[slot, 2]))

    def start(s):
        @pl.when(s < total)
        def _():
            for c in copies(s):
                c.start()

    def x_copy(tile):
        return pltpu.make_async_copy(x_hbm.at[pl.ds(pl.multiple_of(tile * tm, tm), tm), :], x_b, x_sem)

    @pl.when(step == 0)
    def _():
        x_copy(0).start()
        for s in range(RING - 1):
            start(s)

    @pl.when((f == 1) & (i + 1 < pl.num_programs(0)))
    def _():
        x_copy(i + 1).start()

    @pl.when(f == 0)
    def _():
        x_copy(i).wait()
        x = x_b[...]
        if fuse_norm:
            xs_ref[...] = _rms(x, nw_ref[...]).astype(BF16)
            o_ref[...] = x
        else:
            xs_ref[...] = x.astype(BF16)
            o_ref[...] = jnp.zeros_like(o_ref)

    @pl.when(i < na_ref[0])
    def _():
        start(step + RING - 1)
        for c in copies(step):
            c.wait()
        slot = step % RING
        xs = xs_ref[...]
        g = jnp.dot(xs, wg_b[slot].astype(BF16), preferred_element_type=F32)
        u = jnp.dot(xs, wu_b[slot].astype(BF16), preferred_element_type=F32)
        o_ref[...] += jnp.dot((_silu(g) * u).astype(BF16), wd_b[slot].astype(BF16), preferred_element_type=F32)


def ffn(x, nw, tile_expert, n_active, wg, wu, wd, tm, tf, fuse_norm, name):
    rows, d = x.shape
    dff = wg.shape[2]
    tf = _tile(dff, tf)
    any_spec = pl.BlockSpec(memory_space=pl.ANY)
    return pl.pallas_call(
        functools.partial(_ffn_kernel, fuse_norm=fuse_norm),
        grid_spec=pltpu.PrefetchScalarGridSpec(
            num_scalar_prefetch=2,
            grid=(rows // tm, dff // tf),
            in_specs=[any_spec,
                      pl.BlockSpec((1, d), lambda i, f, te, na: (0, 0)),
                      any_spec, any_spec, any_spec],
            out_specs=pl.BlockSpec((tm, d), lambda i, f, te, na: (i, 0), pipeline_mode=pl.Buffered(1)),
            scratch_shapes=[pltpu.VMEM((tm, d), F32), pltpu.VMEM((tm, d), BF16),
                            pltpu.VMEM((RING, d, tf), F32), pltpu.VMEM((RING, d, tf), F32),
                            pltpu.VMEM((RING, tf, d), F32), pltpu.SemaphoreType.DMA((RING, 3)),
                            pltpu.SemaphoreType.DMA(())]),
        out_shape=jax.ShapeDtypeStruct((rows, d), F32),
        compiler_params=_params(("arbitrary", "arbitrary")),
        name=name,
    )(tile_expert, n_active, x, nw.reshape(1, d), wg, wu, wd)


def dense_ffn(h, nw, wg, wu, wd, tm=1024, tf=512):
    tm = _tile(h.shape[0], tm)
    n_tiles = h.shape[0] // tm
    return ffn(h, nw, jnp.zeros((n_tiles,), jnp.int32), jnp.full((1,), n_tiles, jnp.int32),
               wg[None], wu[None], wd[None], tm, tf, True, "dense_ffn")


def _router_kernel(h_ref, nw_ref, wr_ref, xs_ref, idx_ref, wts_ref, cnt_ref, carry_ref):
    i = pl.program_id(0)
    tm = h_ref.shape[0]

    @pl.when(i == 0)
    def _():
        carry_ref[...] = jnp.zeros_like(carry_ref)

    xn = _rms(h_ref[...], nw_ref[...])
    xs_ref[...] = xn
    logits = _dot_hi(xn, wr_ref[...])
    lane = lax.broadcasted_iota(jnp.int32, logits.shape, 1)
    logits = jnp.where(lane < N_EXPERTS, logits, NEG)
    m1 = jnp.max(logits, axis=-1, keepdims=True)
    e1 = jnp.min(jnp.where(logits == m1, lane, LANE), axis=-1, keepdims=True)
    rest = jnp.where(lane == e1, NEG, logits)
    m2 = jnp.max(rest, axis=-1, keepdims=True)
    e2 = jnp.min(jnp.where(rest == m2, lane, LANE), axis=-1, keepdims=True)
    ex = jnp.exp(m2 - m1)
    w1 = 1.0 / (1.0 + ex)
    w2 = ex * w1
    oh1 = (lane == e1).astype(F32)
    oh2 = (lane == e2).astype(F32)
    oh = oh1 + oh2
    r = lax.broadcasted_iota(jnp.int32, (tm, tm), 0)
    c = lax.broadcasted_iota(jnp.int32, (tm, tm), 1)
    before = jnp.dot((r > c).astype(BF16), oh.astype(BF16), preferred_element_type=F32) + carry_ref[...]
    rank1 = jnp.sum(before * oh1, axis=-1, keepdims=True)
    rank2 = jnp.sum(before * oh2, axis=-1, keepdims=True)
    carry_ref[...] += jnp.sum(oh, axis=0, keepdims=True)
    cnt_ref[...] = jnp.broadcast_to(carry_ref[...], cnt_ref.shape)
    idx = jnp.where(lane == 0, e1, jnp.where(lane == 1, e2, 0))
    idx = jnp.where(lane == 2, rank1.astype(jnp.int32), jnp.where(lane == 3, rank2.astype(jnp.int32), idx))
    idx_ref[...] = idx
    wts_ref[...] = jnp.where(lane == 0, w1, jnp.where(lane == 1, w2, 0.0))


def moe_router(h, nw, router, tm=512):
    m, d = h.shape
    tm = _tile(m, tm)
    wr = jnp.zeros((d, LANE), F32).at[:, :N_EXPERTS].set(router)
    return pl.pallas_call(
        _router_kernel,
        grid=(m // tm,),
        in_specs=[pl.BlockSpec((tm, d), lambda i: (i, 0)),
                  pl.BlockSpec((1, d), lambda i: (0, 0)),
                  pl.BlockSpec((d, LANE), lambda i: (0, 0))],
        out_specs=[pl.BlockSpec((tm, d), lambda i: (i, 0)),
                   pl.BlockSpec((tm, LANE), lambda i: (i, 0)),
                   pl.BlockSpec((tm, LANE), lambda i: (i, 0)),
                   pl.BlockSpec((8, LANE), lambda i: (0, 0))],
        out_shape=[jax.ShapeDtypeStruct((m, d), F32),
                   jax.ShapeDtypeStruct((m, LANE), jnp.int32),
                   jax.ShapeDtypeStruct((m, LANE), F32),
                   jax.ShapeDtypeStruct((8, LANE), F32)],
        scratch_shapes=[pltpu.VMEM((1, LANE), F32)],
        compiler_params=_params(("arbitrary",)),
        name="moe_router",
    )(h, nw.reshape(1, d), wr)


ROW_UNROLL = 8


def _dispatch_kernel(dest_ref, xs_ref, init_ref, out_ref, sems):
    del init_ref
    tm = xs_ref.shape[0]

    def start(t, c):
        for k in range(2):
            pltpu.make_async_copy(xs_ref.at[pl.ds(t, 1), :], out_ref.at[pl.ds(dest_ref[2 * t + k], 1), :],
                                  sems.at[k]).start(priority=k)
        return c

    lax.fori_loop(0, tm, start, 0, unroll=ROW_UNROLL)
    for k in range(2):
        pltpu.make_async_copy(xs_ref, out_ref.at[pl.ds(0, tm), :], sems.at[k]).wait()


def moe_dispatch(xs, dest, n_sorted, tm=256):
    m, d = xs.shape
    tm = _tile(m, tm)
    init = jnp.zeros((n_sorted, d), xs.dtype)
    return pl.pallas_call(
        _dispatch_kernel,
        grid=(m // tm,),
        in_specs=[pl.BlockSpec((2 * tm,), lambda i: (i,), memory_space=pltpu.SMEM),
                  pl.BlockSpec((tm, d), lambda i: (i, 0)),
                  pl.BlockSpec(memory_space=pl.ANY)],
        out_specs=pl.BlockSpec(memory_space=pl.ANY),
        out_shape=jax.ShapeDtypeStruct((n_sorted, d), xs.dtype),
        scratch_shapes=[pltpu.SemaphoreType.DMA((2,))],
        input_output_aliases={2: 0},
        compiler_params=_params(("arbitrary",)),
        name="moe_dispatch",
    )(dest, xs, init)


def _combine_kernel(dest_ref, ys_ref, h_ref, wts_ref, o_ref, y_s, sems):
    i = pl.program_id(0)
    tm = h_ref.shape[0]

    def gather(tile, slot):
        def start(t, c):
            for k in range(2):
                src = dest_ref[2 * (tile * tm + t) + k]
                pltpu.make_async_copy(ys_ref.at[pl.ds(src, 1), :], y_s.at[slot, k, pl.ds(t, 1), :],
                                      sems.at[slot, k]).start(priority=k)
            return c

        lax.fori_loop(0, tm, start, 0, unroll=ROW_UNROLL)

    @pl.when(i == 0)
    def _():
        gather(0, 0)

    @pl.when(i + 1 < pl.num_programs(0))
    def _():
        gather(i + 1, (i + 1) % 2)

    slot = i % 2
    for k in range(2):
        pltpu.make_async_copy(ys_ref.at[pl.ds(0, tm), :], y_s.at[slot, k], sems.at[slot, k]).wait()
    wts = wts_ref[...]
    o_ref[...] = h_ref[...] + wts[:, 0:1] * y_s[slot, 0] + wts[:, 1:2] * y_s[slot, 1]


def moe_combine(ysorted, dest, h, wts, tm=256):
    m, d = h.shape
    tm = _tile(m, tm)
    return pl.pallas_call(
        _combine_kernel,
        grid_spec=pltpu.PrefetchScalarGridSpec(
            num_scalar_prefetch=1,
            grid=(m // tm,),
            in_specs=[pl.BlockSpec(memory_space=pl.ANY),
                      pl.BlockSpec((tm, d), lambda i, dest: (i, 0)),
                      pl.BlockSpec((tm, LANE), lambda i, dest: (i, 0))],
            out_specs=pl.BlockSpec((tm, d), lambda i, dest: (i, 0)),
            scratch_shapes=[pltpu.VMEM((2, 2, tm, d), F32), pltpu.SemaphoreType.DMA((2, 2))]),
        out_shape=jax.ShapeDtypeStruct((m, d), F32),
        compiler_params=_params(("arbitrary",)),
        name="moe_combine",
    )(dest, ysorted, h, wts)


def moe_ffn(h, nw, router, wg, wu, wd, tmg=1024):
    m, d = h.shape
    tmg = min(tmg, m)
    xs, idx, wts, cnt = moe_router(h, nw, router)
    counts = cnt[0, :N_EXPERTS].astype(jnp.int32)
    padded = (counts + tmg - 1) // tmg * tmg
    ends = jnp.cumsum(padded)
    offsets = ends - padded
    n_tiles = (2 * m) // tmg + N_EXPERTS
    n_sorted = n_tiles * tmg
    dest = (offsets[idx[:, 0:2]] + idx[:, 2:4]).reshape(-1)
    tile_start = jnp.arange(n_tiles, dtype=jnp.int32) * tmg
    tile_expert = jnp.minimum(jnp.sum(tile_start[:, None] >= ends[None, :], axis=1), N_EXPERTS - 1)
    n_active = (ends[-1] // tmg).reshape(1).astype(jnp.int32)
    xsorted = moe_dispatch(xs, dest, n_sorted)
    ysorted = ffn(xsorted, nw, tile_expert.astype(jnp.int32), n_active, wg, wu, wd, tmg, 512, False, "grouped_ffn")
    return moe_combine(ysorted, dest, h, wts)


def _ple_kernel(h_ref, p_ref, nw_ref, wg_ref, wp_ref, fw_ref, o_ref, *, final):
    h = h_ref[...]
    gate = _sigmoid(jnp.dot(_rms(h, nw_ref[...]).astype(BF16), wg_ref[...], preferred_element_type=F32))
    out = h + jnp.dot(p_ref[...].astype(BF16), wp_ref[...], preferred_element_type=F32) * gate
    o_ref[...] = _rms(out, fw_ref[...]) if final else out


def ple(h, p, nw, wg, wp, fw, final, tm=512):
    m, d = h.shape
    pd = p.shape[1]
    tm = _tile(m, tm)
    return pl.pallas_call(
        functools.partial(_ple_kernel, final=final),
        grid=(m // tm,),
        in_specs=[pl.BlockSpec((tm, d), lambda i: (i, 0)),
                  pl.BlockSpec((tm, pd), lambda i: (i, 0)),
                  pl.BlockSpec((1, d), lambda i: (0, 0)),
                  pl.BlockSpec((d, d), lambda i: (0, 0)),
                  pl.BlockSpec((pd, d), lambda i: (0, 0)),
                  pl.BlockSpec((1, d), lambda i: (0, 0))],
        out_specs=pl.BlockSpec((tm, d), lambda i: (i, 0)),
        out_shape=jax.ShapeDtypeStruct((m, d), F32),
        compiler_params=_params(("parallel",)),
        name="ple",
    )(h, p, nw.reshape(1, d), wg, wp, fw.reshape(1, d))


def _split_w_in(w):
    d = w.shape[0]
    dn_w, diff_w, ml_w = DN_HEADS * HEAD_W, DIFF_HEADS * HEAD_W, ML_HEADS * HEAD_W
    g0 = 3 * dn_w + 2 * ml_w + dn_w
    g1 = g0 + 2 * DN_HEADS
    g2 = g1 + 3 * diff_w + 2 * ml_w
    g3 = g2 + 2 * ML_HEADS
    assert w.shape[1] == g3 and g0 + g2 - g1 == N_BIG
    big = jnp.concatenate([w[:, :g0], w[:, g1:g2]], axis=1).astype(BF16)
    small = jnp.concatenate([w[:, g0:g1], w[:, g2:g3],
                             jnp.zeros((d, LANE - (g1 - g0) - (g3 - g2)), w.dtype)], axis=1).astype(BF16)
    return big, small


def kernel(x, p, attn_norm, w_in, conv_dn, conv_ml, dn_a_log, dn_dt_bias, dn_norm, diff_lq1, diff_lk1, diff_lq2, diff_lk2, diff_norm, ml_i_bias, ml_f_bias, ml_norm, w_out, ffn_norm, dense_w_gate, dense_w_up, dense_w_down, router, moe_w_gate, moe_w_up, moe_w_down, ple_norm, ple_proj, ple_gate, final_norm):
    batch, seq, d = x.shape
    depth = w_in.shape[0]
    m = batch * seq
    h = x.reshape(m, d)
    for i in range(depth):
        lambda_init = 0.8 - 0.6 * math.exp(-0.3 * i)
        w_big, w_small = _split_w_in(w_in[i])
        zb, zs = norm_proj(h, attn_norm[i], w_big, w_small)
        o_dn = deltanet(zb, zs, conv_dn[i], dn_a_log[i], dn_dt_bias[i], dn_norm[i], batch, seq)
        o_diff = diff_attention(zb, diff_lq1[i], diff_lk1[i], diff_lq2[i], diff_lk2[i], diff_norm[i],
                                lambda_init, batch, seq)
        o_ml = mlstm(zb, zs, conv_ml[i], ml_i_bias[i], ml_f_bias[i], ml_norm[i], batch, seq)
        h = out_proj(o_dn, o_diff, o_ml, w_out[i].astype(BF16), h)
        j = i // 2
        if i % 2 == 0:
            h = dense_ffn(h, ffn_norm[i], dense_w_gate[j], dense_w_up[j], dense_w_down[j])
        else:
            h = moe_ffn(h, ffn_norm[i], router[j], moe_w_gate[j], moe_w_up[j], moe_w_down[j])
        h = ple(h, p[i].reshape(m, -1), ple_norm[i], ple_gate[i].astype(BF16), ple_proj[i].astype(BF16),
                final_norm, final=(i == depth - 1))
    return h.reshape(batch, seq, d)
```
